```python
import math
import jax, jax.numpy as jnp
from jax import lax
import numpy as np

D_MODEL = 2048
BATCH = 2
SEQ = 4096
DEPTH = 1
DEC_BATCH = 128
DEC_SEQ = 8
PAST_LEN = 2048
PAGE_SIZE = 128

MIX_WIDTH = D_MODEL
RET_HEADS = 8
RET_DK = MIX_WIDTH // 2 // RET_HEADS
RET_DV = RET_DK
RET_THETA = 10000.0
RET_CHUNK = 128
DIFF_DK = 128
DIFF_DV = 2 * DIFF_DK
DIFF_HEADS = (MIX_WIDTH - RET_HEADS * RET_DV) // DIFF_DV
ROPE_THETA = 500000.0
ROPE_DIM = DIFF_DK // 4
MEM_TOKENS = 256
MEM_HEADS = 4
MEM_DH = 128
D_FF = 4 * D_MODEL
Q_BLOCK = 128
EPS = 1e-6
IN_COLS = 2 * RET_HEADS * RET_DK + 2 * RET_HEADS * RET_DV + 4 * DIFF_HEADS * DIFF_DK + DIFF_HEADS * DIFF_DV

kernel_name = 'hybrid_retention_diffattn_decoder_step'

F32 = jnp.float32


def rmsnorm(x, g):
    x32 = x.astype(F32)
    y = x32 * lax.rsqrt(jnp.mean(x32 * x32, axis=-1, keepdims=True) + EPS)
    return (y * g.astype(F32)).astype(x.dtype)


def rotary(x, pos, rot_dim, theta):
    half = rot_dim // 2
    inv = jnp.exp(-math.log(theta) * (2.0 * jnp.arange(half, dtype=F32) / rot_dim))
    ang = pos[:, None] * inv[None, :]
    cos = jnp.cos(ang)[None, :, None, :]
    sin = jnp.sin(ang)[None, :, None, :]
    xr = x[..., :rot_dim].astype(F32)
    x1, x2 = xr[..., :half], xr[..., half:]
    rot = jnp.concatenate([x1 * cos - x2 * sin, x2 * cos + x1 * sin], axis=-1).astype(x.dtype)
    return jnp.concatenate([rot, x[..., rot_dim:]], axis=-1)


def mixer_projections(h, w_in, pos):
    B, L, _ = h.shape
    z = h @ w_in
    widths = [RET_HEADS * RET_DK, RET_HEADS * RET_DK, RET_HEADS * RET_DV, RET_HEADS * RET_DV,
              2 * DIFF_HEADS * DIFF_DK, 2 * DIFF_HEADS * DIFF_DK, DIFF_HEADS * DIFF_DV]
    points, acc = [], 0
    for w in widths[:-1]:
        acc += w
        points.append(acc)
    rq, rk, rv, rg, dq, dk, dv = jnp.split(z, points, axis=-1)
    rq = rotary(rq.reshape(B, L, RET_HEADS, RET_DK), pos, RET_DK, RET_THETA)
    rk = rotary(rk.reshape(B, L, RET_HEADS, RET_DK), pos, RET_DK, RET_THETA) * (RET_DK ** -0.5)
    rv = rv.reshape(B, L, RET_HEADS, RET_DV)
    dq = rotary(dq.reshape(B, L, 2 * DIFF_HEADS, DIFF_DK), pos, ROPE_DIM, ROPE_THETA).reshape(B, L, DIFF_HEADS, 2, DIFF_DK)
    dk = rotary(dk.reshape(B, L, 2 * DIFF_HEADS, DIFF_DK), pos, ROPE_DIM, ROPE_THETA).reshape(B, L, DIFF_HEADS, 2, DIFF_DK)
    dv = dv.reshape(B, L, DIFF_HEADS, DIFF_DV)
    return rq, rk, rv, rg, dq, dk, dv


def retention_chunkwise(q, k, v, s0, chunk):
    B, L, H, _ = q.shape
    dv = v.shape[-1]
    n = L // chunk
    lg = jnp.log1p(-jnp.exp2(-5.0 - jnp.arange(H, dtype=F32)))
    idx = jnp.arange(chunk, dtype=F32)
    rel = idx[:, None] - idx[None, :]
    dmat = jnp.where(rel[None] >= 0, jnp.exp(jnp.maximum(rel, 0.0)[None] * lg[:, None, None]), 0.0)
    q_decay = jnp.exp((idx + 1.0)[:, None] * lg[None, :])[None, :, :, None]
    k_decay = jnp.exp((chunk - 1.0 - idx)[:, None] * lg[None, :])[None, :, :, None]
    chunk_decay = jnp.exp(chunk * lg)[None, :, None, None]

    def to_chunks(t):
        return t.astype(F32).reshape(B, n, chunk, H, t.shape[-1]).swapaxes(0, 1)

    def step(s, inp):
        qc, kc, vc = inp
        att = jnp.einsum('bihd,bjhd->bhij', qc, kc) * dmat[None]
        o = jnp.einsum('bhij,bjhe->bihe', att, vc) + jnp.einsum('bihd,bhde->bihe', qc, s) * q_decay
        s = s * chunk_decay + jnp.einsum('bjhd,bjhe->bhde', kc * k_decay, vc)
        return s, o

    sT, o = lax.scan(step, s0.astype(F32), (to_chunks(q), to_chunks(k), to_chunks(v)))
    return o.swapaxes(0, 1).reshape(B, L, H, dv), sT


def diff_attn_causal(q, k, v, lam, pos):
    B, L, H, _, d = q.shape
    blk = math.gcd(L, Q_BLOCK)
    nb = L // blk
    scale = d ** -0.5
    k32 = k.astype(F32)
    v32 = v.astype(F32)
    qb = q.astype(F32).reshape(B, nb, blk, H, 2, d).swapaxes(0, 1)
    pb = pos.reshape(nb, blk)

    def block(args):
        qi, pi = args
        s = jnp.einsum('bqhmd,bkhmd->bhmqk', qi, k32) * scale
        s = jnp.where((pi[:, None] >= pos[None, :])[None, None, None], s, -jnp.inf)
        p = jax.nn.softmax(s, axis=-1)
        a = p[:, :, 0] - lam * p[:, :, 1]
        return jnp.einsum('bhqk,bkhe->bqhe', a, v32)

    o = lax.map(block, (qb, pb))
    return o.swapaxes(0, 1).reshape(B, L, H, v.shape[-1])


def diff_attn_with_past(q, k, v, k_past, v_past, lam):
    L = q.shape[1]
    P = k_past.shape[1]
    q32 = q.astype(F32) * (q.shape[-1] ** -0.5)
    s_past = jnp.einsum('bqhmd,bkhmd->bhmqk', q32, k_past.astype(F32))
    s_new = jnp.einsum('bqhmd,bkhmd->bhmqk', q32, k.astype(F32))
    causal = jnp.tril(jnp.ones((L, L), dtype=bool))
    s_new = jnp.where(causal[None, None, None], s_new, -jnp.inf)
    p = jax.nn.softmax(jnp.concatenate([s_past, s_new], axis=-1), axis=-1)
    a = p[:, :, 0] - lam * p[:, :, 1]
    return (jnp.einsum('bhqk,bkhe->bqhe', a[..., :P], v_past.astype(F32))
            + jnp.einsum('bhqk,bkhe->bqhe', a[..., P:], v.astype(F32)))


def retention_out(o, gate):
    B, L, H, dv = o.shape
    y = o * lax.rsqrt(jnp.mean(o * o, axis=-1, keepdims=True) + EPS)
    return y.reshape(B, L, H * dv) * jax.nn.silu(gate.astype(F32))


def diff_head_out(o, g, lam_init):
    B, L, H, dv = o.shape
    y = o * lax.rsqrt(jnp.mean(o * o, axis=-1, keepdims=True) + EPS) * g.astype(F32) * (1.0 - lam_init)
    return y.reshape(B, L, H * dv)


def memory_kv(mem, g, wk, wv):
    B, M, _ = mem.shape
    m = rmsnorm(mem, g)
    return (m @ wk).reshape(B, M, MEM_HEADS, MEM_DH), (m @ wv).reshape(B, M, MEM_HEADS, MEM_DH)


def cross_attn(h, mk, mv, wq, wo):
    B, L, _ = h.shape
    q = (h @ wq).reshape(B, L, MEM_HEADS, MEM_DH).astype(F32) * (MEM_DH ** -0.5)
    p = jax.nn.softmax(jnp.einsum('blhd,bmhd->bhlm', q, mk.astype(F32)), axis=-1)
    o = jnp.einsum('bhlm,bmhd->blhd', p, mv.astype(F32)).reshape(B, L, MEM_HEADS * MEM_DH)
    return o.astype(h.dtype) @ wo


def sq_relu_mlp(h, w_up, w_down):
    return jnp.square(jax.nn.relu(h @ w_up)) @ w_down


def decoder_layer(x, pos, ret_s0, ret_chunk, k_past, v_past, mem_k, mem_v, layer_idx,
                  w_in, w_out, lq1, lk1, lq2, lk2, subln_g,
                  g_mix_pre, g_mix_post, g_mem_pre, g_mem_post, w_mem_q, w_mem_o,
                  g_mlp_pre, g_mlp_post, w_up, w_down):
    lam_init = 0.8 - 0.6 * math.exp(-0.3 * layer_idx)
    lam = (jnp.exp(jnp.sum(lq1.astype(F32) * lk1.astype(F32)))
           - jnp.exp(jnp.sum(lq2.astype(F32) * lk2.astype(F32))) + lam_init)
    h = rmsnorm(x, g_mix_pre)
    rq, rk, rv, rg, dq, dk, dv = mixer_projections(h, w_in, pos)
    ret_o, ret_sT = retention_chunkwise(rq, rk, rv, ret_s0, ret_chunk)
    if k_past is None:
        d_o = diff_attn_causal(dq, dk, dv, lam, pos)
    else:
        d_o = diff_attn_with_past(dq, dk, dv, k_past, v_past, lam)
    mix = jnp.concatenate([retention_out(ret_o, rg), diff_head_out(d_o, subln_g, lam_init)], axis=-1).astype(x.dtype)
    x = x + rmsnorm(mix @ w_out, g_mix_post)
    x = x + rmsnorm(cross_attn(rmsnorm(x, g_mem_pre), mem_k, mem_v, w_mem_q, w_mem_o), g_mem_post)
    x = x + rmsnorm(sq_relu_mlp(rmsnorm(x, g_mlp_pre), w_up, w_down), g_mlp_post)
    return x, ret_sT, dk, dv


def setup_inputs(seed: int = 0) -> dict:
    key = jax.random.key(seed)
    keys = jax.random.split(key, 40)
    counter = [0]

    def nk():
        counter[0] += 1
        return keys[counter[0] - 1]

    def nrm(shape, scale):
        return jax.random.normal(nk(), shape, F32) * scale

    def gain(shape):
        return 1.0 + nrm(shape, 0.02)

    n_pages = PAST_LEN // PAGE_SIZE
    n_used = DEC_BATCH * n_pages
    n_pool = n_used + n_used // 4
    page_table = jax.random.permutation(nk(), n_pool)[:n_used].reshape(DEC_BATCH, n_pages).astype(jnp.int32)
    return {
        'x_prompt': nrm((BATCH, SEQ, D_MODEL), 1.0),
        'x_sample': nrm((DEC_BATCH, DEC_SEQ, D_MODEL), 1.0),
        'mem_prompt': nrm((BATCH, MEM_TOKENS, D_MODEL), 1.0),
        'state_ret': nrm((DEPTH, DEC_BATCH, RET_HEADS, RET_DK, RET_DV), 0.3),
        'cache_diff_k': nrm((DEPTH, n_pool, PAGE_SIZE, DIFF_HEADS, 2, DIFF_DK), 1.0),
        'cache_diff_v': nrm((DEPTH, n_pool, PAGE_SIZE, DIFF_HEADS, DIFF_DV), 1.0),
        'cache_mem_k': nrm((DEPTH, DEC_BATCH, MEM_TOKENS, MEM_HEADS, MEM_DH), 1.0),
        'cache_mem_v': nrm((DEPTH, DEC_BATCH, MEM_TOKENS, MEM_HEADS, MEM_DH), 1.0),
        'page_table': page_table,
        'w_in': nrm((DEPTH, D_MODEL, IN_COLS), D_MODEL ** -0.5),
        'w_out': nrm((DEPTH, MIX_WIDTH, D_MODEL), MIX_WIDTH ** -0.5),
        'diff_lambda_q1': nrm((DEPTH, DIFF_DK), 0.1),
        'diff_lambda_k1': nrm((DEPTH, DIFF_DK), 0.1),
        'diff_lambda_q2': nrm((DEPTH, DIFF_DK), 0.1),
        'diff_lambda_k2': nrm((DEPTH, DIFF_DK), 0.1),
        'diff_subln_g': gain((DEPTH, DIFF_DV)),
        'norm_mix_pre': gain((DEPTH, D_MODEL)),
        'norm_mix_post': gain((DEPTH, D_MODEL)),
        'norm_mem_pre': gain((DEPTH, D_MODEL)),
        'norm_mem_post': gain((DEPTH, D_MODEL)),
        'norm_mlp_pre': gain((DEPTH, D_MODEL)),
        'norm_mlp_post': gain((DEPTH, D_MODEL)),
        'mem_norm_g': gain((DEPTH, D_MODEL)),
        'w_mem_q': nrm((DEPTH, D_MODEL, MEM_HEADS * MEM_DH), D_MODEL ** -0.5),
        'w_mem_k': nrm((DEPTH, D_MODEL, MEM_HEADS * MEM_DH), D_MODEL ** -0.5),
        'w_mem_v': nrm((DEPTH, D_MODEL, MEM_HEADS * MEM_DH), D_MODEL ** -0.5),
        'w_mem_o': nrm((DEPTH, MEM_HEADS * MEM_DH, D_MODEL), (MEM_HEADS * MEM_DH) ** -0.5),
        'w_mlp_up': nrm((DEPTH, D_MODEL, D_FF), D_MODEL ** -0.5),
        'w_mlp_down': nrm((DEPTH, D_FF, D_MODEL), D_FF ** -0.5),
    }


def reference(x_prompt, x_sample, mem_prompt, state_ret, cache_diff_k, cache_diff_v,
              cache_mem_k, cache_mem_v, page_table, w_in, w_out,
              diff_lambda_q1, diff_lambda_k1, diff_lambda_q2, diff_lambda_k2, diff_subln_g,
              norm_mix_pre, norm_mix_post, norm_mem_pre, norm_mem_post, norm_mlp_pre, norm_mlp_post,
              mem_norm_g, w_mem_q, w_mem_k, w_mem_v, w_mem_o, w_mlp_up, w_mlp_down):
    B, L_p, _ = x_prompt.shape
    B_s, L_s, _ = x_sample.shape
    n_pages = page_table.shape[1]
    page = cache_diff_k.shape[2]
    past_len = n_pages * page
    pos_p = jnp.arange(L_p, dtype=F32)
    pos_s = past_len + jnp.arange(L_s, dtype=F32)
    ret_chunk_p = math.gcd(L_p, RET_CHUNK)

    yp, ys = x_prompt, x_sample
    rp_l, kp_l, vp_l, mkp_l, mvp_l, rs_l, ks_l, vs_l = [], [], [], [], [], [], [], []
    for i in range(DEPTH):
        shared = (w_in[i], w_out[i], diff_lambda_q1[i], diff_lambda_k1[i], diff_lambda_q2[i],
                  diff_lambda_k2[i], diff_subln_g[i], norm_mix_pre[i], norm_mix_post[i],
                  norm_mem_pre[i], norm_mem_post[i], w_mem_q[i], w_mem_o[i],
                  norm_mlp_pre[i], norm_mlp_post[i], w_mlp_up[i], w_mlp_down[i])
        mk_p, mv_p = memory_kv(mem_prompt, mem_norm_g[i], w_mem_k[i], w_mem_v[i])
        s0_p = jnp.zeros((B, RET_HEADS, RET_DK, RET_DV), F32)
        yp, sp, kp, vp = decoder_layer(yp, pos_p, s0_p, ret_chunk_p, None, None, mk_p, mv_p, i, *shared)
        k_past = cache_diff_k[i][page_table].reshape(B_s, past_len, DIFF_HEADS, 2, DIFF_DK)
        v_past = cache_diff_v[i][page_table].reshape(B_s, past_len, DIFF_HEADS, DIFF_DV)
        ys, ss, ks, vs = decoder_layer(ys, pos_s, state_ret[i], L_s, k_past, v_past,
                                       cache_mem_k[i], cache_mem_v[i], i, *shared)
        rp_l.append(sp.astype(state_ret.dtype))
        kp_l.append(kp)
        vp_l.append(vp)
        mkp_l.append(mk_p)
        mvp_l.append(mv_p)
        rs_l.append(ss.astype(state_ret.dtype))
        ks_l.append(ks)
        vs_l.append(vs)
    new_ret_prompt = jnp.stack(rp_l)
    new_k_prompt = jnp.stack(kp_l)
    new_v_prompt = jnp.stack(vp_l)
    new_mem_k_prompt = jnp.stack(mkp_l)
    new_mem_v_prompt = jnp.stack(mvp_l)
    new_ret_sample = jnp.stack(rs_l)
    new_k_sample = jnp.stack(ks_l)
    new_v_sample = jnp.stack(vs_l)
    return (yp, ys, new_ret_prompt, new_k_prompt, new_v_prompt, new_mem_k_prompt, new_mem_v_prompt,
            new_ret_sample, new_k_sample, new_v_sample)
```

```python
import functools
import math

import jax
import jax.numpy as jnp
from jax import lax
from jax.experimental import pallas as pl
from jax.experimental.pallas import tpu as pltpu

F32 = jnp.float32
BF16 = jnp.bfloat16

LANES = 128
RET_HEADS = 8
RET_DK = 128
RET_DV = 128
RET_THETA = 10000.0
RET_CHUNK = 128
DIFF_HEADS = 4
DIFF_DK = 128
DIFF_DV = 256
ROPE_THETA = 500000.0
ROPE_DIM = DIFF_DK // 4
MEM_HEADS = 4
MEM_DH = 128
Q_BLOCK = 128
EPS = 1e-6
SEG = 1024
N_SEG = 7

VMEM_LIMIT = 56 * 1024 * 1024

_NT = (((1,), (1,)), ((), ()))
_TN = (((0,), (0,)), ((), ()))


def _params(sem, vmem=VMEM_LIMIT):
    return pltpu.CompilerParams(dimension_semantics=sem, vmem_limit_bytes=vmem)


def _rms(x, g):
    return x * lax.rsqrt(jnp.mean(x * x, axis=-1, keepdims=True) + EPS) * g


def _unit_rms(o):
    return o * lax.rsqrt(jnp.mean(o * o, axis=-1, keepdims=True) + EPS)


def _rope_tables(pos, rot_dim, theta):
    half = rot_dim // 2
    inv = jnp.exp(-math.log(theta) * (2.0 * jnp.arange(half, dtype=F32) / rot_dim))
    ang = pos[:, None] * inv[None, :]
    cos, sin = jnp.cos(ang), jnp.sin(ang)
    n = pos.shape[0]
    rest = LANES - rot_dim
    c = jnp.concatenate([cos, cos, jnp.ones((n, rest), F32)], axis=-1)
    s_lo = jnp.concatenate([-sin, jnp.zeros((n, LANES - half), F32)], axis=-1)
    s_hi = jnp.concatenate([jnp.zeros((n, half), F32), sin, jnp.zeros((n, rest), F32)], axis=-1)
    return c, s_lo, s_hi


def _inproj_kernel(x_ref, g_ref, w_ref, cr_ref, sr_ref, cd_ref, sdl_ref, sdh_ref,
                   rq_ref, rk_ref, rv_ref, rg_ref, dq_ref, dk_ref, dv_ref, h_scr, z_scr):
    j = pl.program_id(1)

    @pl.when(j == 0)
    def _():
        h_scr[...] = _rms(x_ref[...], g_ref[...]).astype(BF16)

    z_scr[...] = jnp.dot(h_scr[...], w_ref[...], preferred_element_type=F32)

    def ret_rot(out_ref, scale):
        c, s = cr_ref[...], sr_ref[...]
        for h in range(SEG // LANES):
            sl = slice(h * LANES, (h + 1) * LANES)
            z = z_scr[:, sl]
            r = z * c + pltpu.roll(z, RET_DK // 2, 1) * s
            if scale != 1.0:
                r = r * scale
            out_ref[:, sl] = r.astype(out_ref.dtype)

    def diff_rot(z):
        half = ROPE_DIM // 2
        return (z * cd_ref[...] + pltpu.roll(z, LANES - half, 1) * sdl_ref[...]
                + pltpu.roll(z, half, 1) * sdh_ref[...])

    @pl.when(j == 0)
    def _():
        ret_rot(rq_ref, 1.0)

    @pl.when(j == 1)
    def _():
        ret_rot(rk_ref, RET_DK ** -0.5)

    @pl.when(j == 2)
    def _():
        rv_ref[...] = z_scr[...].astype(rv_ref.dtype)

    @pl.when(j == 3)
    def _():
        rg_ref[...] = z_scr[...]

    @pl.when(j == 4)
    def _():
        for h in range(SEG // LANES):
            sl = slice(h * LANES, (h + 1) * LANES)
            dq_ref[:, sl] = (diff_rot(z_scr[:, sl]) * (DIFF_DK ** -0.5)).astype(dq_ref.dtype)

    @pl.when(j == 5)
    def _():
        for h in range(SEG // LANES):
            sl = slice(h * LANES, (h + 1) * LANES)
            dk_ref[:, sl] = diff_rot(z_scr[:, sl])

    @pl.when(j == 6)
    def _():
        dv_ref[...] = z_scr[...]


def _in_projection(x2d, g, w_bf, tabs, period_blocks, tm, act_dtype):
    T, D = x2d.shape
    cr, sr, cd, sdl, sdh = tabs
    tab_spec = pl.BlockSpec((tm, LANES), lambda i, j: (i % period_blocks, 0))
    out_spec = pl.BlockSpec((tm, SEG), lambda i, j: (i, 0))
    sds = lambda dt: jax.ShapeDtypeStruct((T, SEG), dt)
    return pl.pallas_call(
        _inproj_kernel,
        grid=(T // tm, N_SEG),
        in_specs=[
            pl.BlockSpec((tm, D), lambda i, j: (i, 0)),
            pl.BlockSpec((1, D), lambda i, j: (0, 0)),
            pl.BlockSpec((D, SEG), lambda i, j: (0, j)),
            tab_spec, tab_spec, tab_spec, tab_spec, tab_spec,
        ],
        out_specs=[out_spec] * N_SEG,
        out_shape=[sds(act_dtype), sds(act_dtype), sds(act_dtype), sds(F32),
                   sds(act_dtype), sds(F32), sds(F32)],
        scratch_shapes=[pltpu.VMEM((tm, D), BF16), pltpu.VMEM((tm, SEG), F32)],
        compiler_params=_params(("parallel", "arbitrary")),
        name="in_projection",
    )(x2d, g.reshape(1, D), w_bf, cr, sr, cd, sdl, sdh)


def _ret_decay_tables(chunk):
    lg = jnp.log1p(-jnp.exp2(-5.0 - jnp.arange(RET_HEADS, dtype=F32)))
    idx = jnp.arange(chunk, dtype=F32)
    rel = idx[:, None] - idx[None, :]
    dmat = jnp.where(rel[None] >= 0, jnp.exp(jnp.maximum(rel, 0.0)[None] * lg[:, None, None]), 0.0)
    q_decay = jnp.exp((idx + 1.0)[:, None] * lg[None, :])
    k_decay = jnp.exp((chunk - 1.0 - idx)[:, None] * lg[None, :])
    chunk_decay = jnp.exp(chunk * lg)
    widen = lambda t: jnp.repeat(t, RET_DK, axis=1)
    return dmat, widen(q_decay), widen(k_decay), chunk_decay


def _ret_head(q, k, v, s, dmat, qd, kd, cd, gate):
    qb, kb, vb = q.astype(BF16), k.astype(BF16), v.astype(BF16)
    att = lax.dot_general(qb, kb, _NT, preferred_element_type=F32) * dmat
    o = (jnp.dot(att.astype(BF16), vb, preferred_element_type=F32)
         + jnp.dot(qb, s.astype(BF16), preferred_element_type=F32) * qd)
    kdk = (k.astype(F32) * kd).astype(BF16)
    s_new = s * cd + lax.dot_general(kdk, vb, _TN, preferred_element_type=F32)
    y = _unit_rms(o) * (gate * jax.nn.sigmoid(gate))
    return y, s_new


def _ret_prompt_kernel(cd_ref, q_ref, k_ref, v_ref, g_ref, dmat_ref, qd_ref, kd_ref, mix_ref, s_ref):
    @pl.when(pl.program_id(1) == 0)
    def _():
        s_ref[...] = jnp.zeros_like(s_ref)

    for h in range(RET_HEADS):
        sl = slice(h * RET_DK, (h + 1) * RET_DK)
        y, s_new = _ret_head(q_ref[:, sl], k_ref[:, sl], v_ref[:, sl], s_ref[0, h], dmat_ref[h],
                             qd_ref[:, sl], kd_ref[:, sl], cd_ref[h], g_ref[:, sl])
        s_ref[0, h] = s_new
        mix_ref[:, sl] = y.astype(mix_ref.dtype)


def _retention_prompt(rq, rk, rv, rg, batch, seq, chunk):
    n = seq // chunk
    dmat, qd, kd, cd = _ret_decay_tables(chunk)
    row = pl.BlockSpec((chunk, SEG), lambda b, c: (b * n + c, 0))
    full2 = pl.BlockSpec((chunk, SEG), lambda b, c: (0, 0))
    return pl.pallas_call(
        _ret_prompt_kernel,
        grid=(batch, n),
        in_specs=[
            pl.BlockSpec(memory_space=pltpu.SMEM),
            row, row, row, row,
            pl.BlockSpec((RET_HEADS, chunk, chunk), lambda b, c: (0, 0, 0)),
            full2, full2,
        ],
        out_specs=[row, pl.BlockSpec((1, RET_HEADS, RET_DK, RET_DV), lambda b, c: (b, 0, 0, 0))],
        out_shape=[jax.ShapeDtypeStruct((batch * seq, SEG), BF16),
                   jax.ShapeDtypeStruct((batch, RET_HEADS, RET_DK, RET_DV), F32)],
        compiler_params=_params(("parallel", "arbitrary")),
        name="retention_prompt",
    )(cd, rq, rk, rv, rg, dmat, qd, kd)


def _ret_sample_kernel(cd_ref, q_ref, k_ref, v_ref, g_ref, s0_ref, dmat_ref, qd_ref, kd_ref,
                       mix_ref, s_ref, *, group, seq):
    rows = []
    for b in range(group):
        r = slice(b * seq, (b + 1) * seq)
        heads = []
        for h in range(RET_HEADS):
            sl = slice(h * RET_DK, (h + 1) * RET_DK)
            y, s_new = _ret_head(q_ref[r, sl], k_ref[r, sl], v_ref[r, sl], s0_ref[b, h], dmat_ref[h],
                                 qd_ref[:, sl], kd_ref[:, sl], cd_ref[h], g_ref[r, sl])
            s_ref[b, h] = s_new
            heads.append(y)
        rows.append(jnp.concatenate(heads, axis=1))
    mix_ref[...] = jnp.concatenate(rows, axis=0)


def _retention_sample(rq, rk, rv, rg, s0, batch, seq, group):
    dmat, qd, kd, cd = _ret_decay_tables(seq)
    row = pl.BlockSpec((group * seq, SEG), lambda i: (i, 0))
    tab = pl.BlockSpec((seq, SEG), lambda i: (0, 0))
    st = pl.BlockSpec((group, RET_HEADS, RET_DK, RET_DV), lambda i: (i, 0, 0, 0))
    return pl.pallas_call(
        functools.partial(_ret_sample_kernel, group=group, seq=seq),
        grid=(batch // group,),
        in_specs=[
            pl.BlockSpec(memory_space=pltpu.SMEM),
            row, row, row, row, st,
            pl.BlockSpec((RET_HEADS, seq, seq), lambda i: (0, 0, 0)),
            tab, tab,
        ],
        out_specs=[row, st],
        out_shape=[jax.ShapeDtypeStruct((batch * seq, SEG), F32),
                   jax.ShapeDtypeStruct((batch, RET_HEADS, RET_DK, RET_DV), F32)],
        compiler_params=_params(("parallel",)),
        name="retention_sample",
    )(cd, rq, rk, rv, rg, s0, dmat, qd, kd)


def _lambda_value(lam_ref, lam_init):
    a = jnp.sum(lam_ref[0:1, :] * lam_ref[1:2, :], axis=-1, keepdims=True)
    b = jnp.sum(lam_ref[2:3, :] * lam_ref[3:4, :], axis=-1, keepdims=True)
    return jnp.exp(a) - jnp.exp(b) + lam_init


def _softmax_update(s, m_old, l_old):
    m_new = jnp.maximum(m_old, jnp.max(s, axis=-1, keepdims=True))
    alpha = jnp.exp(m_old - m_new)
    p = jnp.exp(s - m_new)
    l_new = alpha * l_old + jnp.sum(p, axis=-1, keepdims=True)
    return p, alpha, m_new, l_new


def _diff_prompt_kernel(lam_ref, q_ref, k_ref, v_ref, g_ref, o_ref, m_scr, l_scr, acc_scr,
                        *, tq, tk, lam_init):
    qi = pl.program_id(2)
    ki = pl.program_id(3)
    last_k = (qi * tq + tq - 1) // tk

    @pl.when(ki == 0)
    def _():
        m_scr[...] = jnp.full_like(m_scr, -jnp.inf)
        l_scr[...] = jnp.zeros_like(l_scr)
        acc_scr[...] = jnp.zeros_like(acc_scr)

    def step(masked):
        vb = v_ref[...].astype(BF16)
        if masked:
            rowp = qi * tq + lax.broadcasted_iota(jnp.int32, (tq, tk), 0)
            colp = ki * tk + lax.broadcasted_iota(jnp.int32, (tq, tk), 1)
            keep = rowp >= colp
        for m in range(2):
            sl = slice(m * DIFF_DK, (m + 1) * DIFF_DK)
            s = lax.dot_general(q_ref[:, sl], k_ref[:, sl].astype(BF16), _NT, preferred_element_type=F32)
            if masked:
                s = jnp.where(keep, s, -jnp.inf)
            p, alpha, m_new, l_new = _softmax_update(s, m_scr[m], l_scr[m])
            m_scr[m] = m_new
            l_scr[m] = l_new
            acc_scr[m] = alpha * acc_scr[m] + jnp.dot(p.astype(BF16), vb, preferred_element_type=F32)

    @pl.when(ki < last_k)
    def _():
        step(False)

    @pl.when(ki == last_k)
    def _():
        step(True)
        lam = _lambda_value(lam_ref, lam_init)
        o = acc_scr[0] / l_scr[0] - lam * (acc_scr[1] / l_scr[1])
        o_ref[...] = (_unit_rms(o) * g_ref[...] * (1.0 - lam_init)).astype(o_ref.dtype)


def _diff_attention_prompt(dq, dk, dv, lam_rows, subln_g, batch, seq, lam_init, tq=512, tk=512):
    assert tq == tk
    nq, nk = seq // tq, seq // tk
    hw = 2 * DIFF_DK
    q_spec = pl.BlockSpec((tq, hw), lambda b, h, qi, ki: (b * nq + qi, h))
    kv_map = lambda b, h, qi, ki: (b * nk + jnp.minimum(ki, (qi * tq + tq - 1) // tk), h)
    kv_spec = pl.BlockSpec((tk, hw), kv_map)
    return pl.pallas_call(
        functools.partial(_diff_prompt_kernel, tq=tq, tk=tk, lam_init=lam_init),
        grid=(batch, DIFF_HEADS, nq, nk),
        in_specs=[
            pl.BlockSpec((4, DIFF_DK), lambda b, h, qi, ki: (0, 0)),
            q_spec, kv_spec, kv_spec,
            pl.BlockSpec((1, DIFF_DV), lambda b, h, qi, ki: (0, 0)),
        ],
        out_specs=q_spec,
        out_shape=jax.ShapeDtypeStruct((batch * seq, SEG), BF16),
        scratch_shapes=[pltpu.VMEM((2, tq, 1), F32), pltpu.VMEM((2, tq, 1), F32),
                        pltpu.VMEM((2, tq, DIFF_DV), F32)],
        compiler_params=_params(("parallel", "parallel", "parallel", "arbitrary")),
        name="diff_attention_prompt",
    )(lam_rows, dq, dk, dv, subln_g.reshape(1, DIFF_DV))


PAGES_PER_STEP = 8


def _diff_sample_kernel(pt_ref, lam_ref, q_ref, kn_ref, vn_ref, g_ref, *rest, seq, page, lam_init):
    k_refs = rest[:PAGES_PER_STEP]
    v_refs = rest[PAGES_PER_STEP:2 * PAGES_PER_STEP]
    o_ref, m_scr, l_scr, acc_scr = rest[2 * PAGES_PER_STEP:]
    step = pl.program_id(1)
    n_sub = 2 * DIFF_HEADS

    @pl.when(step == 0)
    def _():
        m_scr[...] = jnp.full_like(m_scr, -jnp.inf)
        l_scr[...] = jnp.zeros_like(l_scr)
        acc_scr[...] = jnp.zeros_like(acc_scr)

    def attend(scores_fn, pv_fn):
        for h in range(DIFF_HEADS):
            ps = []
            for m in range(2):
                i = 2 * h + m
                p, alpha, m_new, l_new = _softmax_update(scores_fn(i), m_scr[i], l_scr[i])
                m_scr[i] = m_new
                l_scr[i] = l_new
                acc_scr[i] = alpha * acc_scr[i]
                ps.append(p)
            pv = pv_fn(h, jnp.concatenate(ps, axis=0).astype(BF16))
            acc_scr[2 * h] += pv[:seq]
            acc_scr[2 * h + 1] += pv[seq:]

    q = q_ref[...]
    qs = [q[:, i * DIFF_DK:(i + 1) * DIFF_DK].astype(BF16) for i in range(n_sub)]

    def past_scores(i):
        sl = slice(i * DIFF_DK, (i + 1) * DIFF_DK)
        return jnp.concatenate(
            [lax.dot_general(qs[i], kr[:, sl].astype(BF16), _NT, preferred_element_type=F32)
             for kr in k_refs], axis=1)

    def past_pv(h, p):
        sl = slice(h * DIFF_DV, (h + 1) * DIFF_DV)
        out = None
        for n, vr in enumerate(v_refs):
            t = jnp.dot(p[:, n * page:(n + 1) * page], vr[:, sl].astype(BF16), preferred_element_type=F32)
            out = t if out is None else out + t
        return out

    attend(past_scores, past_pv)

    @pl.when(step == pl.num_programs(1) - 1)
    def _():
        kn, vn = kn_ref[...], vn_ref[...]
        causal = (lax.broadcasted_iota(jnp.int32, (seq, seq), 0)
                  >= lax.broadcasted_iota(jnp.int32, (seq, seq), 1))

        def new_scores(i):
            sl = slice(i * DIFF_DK, (i + 1) * DIFF_DK)
            s = lax.dot_general(qs[i], kn[:, sl].astype(BF16), _NT, preferred_element_type=F32)
            return jnp.where(causal, s, -jnp.inf)

        def new_pv(h, p):
            sl = slice(h * DIFF_DV, (h + 1) * DIFF_DV)
            return jnp.dot(p, vn[:, sl].astype(BF16), preferred_element_type=F32)

        attend(new_scores, new_pv)
        lam = _lambda_value(lam_ref, lam_init)
        outs = []
        for h in range(DIFF_HEADS):
            o = acc_scr[2 * h] / l_scr[2 * h] - lam * (acc_scr[2 * h + 1] / l_scr[2 * h + 1])
            outs.append(_unit_rms(o) * g_ref[...] * (1.0 - lam_init))
        o_ref[...] = jnp.concatenate(outs, axis=1)


def _diff_attention_sample(dq, dk_new, dv_new, cache_k, cache_v, page_table, lam_rows, subln_g,
                           batch, seq, lam_init):
    n_pool, page = cache_k.shape[0], cache_k.shape[1]
    n_pages = page_table.shape[1]
    assert n_pages % PAGES_PER_STEP == 0
    ck = cache_k.reshape(n_pool, page, SEG)
    cv = cache_v.reshape(n_pool, page, SEG)
    row = pl.BlockSpec((seq, SEG), lambda b, s, pt: (b, 0))

    def page_spec(n):
        return pl.BlockSpec((None, page, SEG), lambda b, s, pt: (pt[b, s * PAGES_PER_STEP + n], 0, 0))

    grid_spec = pltpu.PrefetchScalarGridSpec(
        num_scalar_prefetch=1,
        grid=(batch, n_pages // PAGES_PER_STEP),
        in_specs=[
            pl.BlockSpec((4, DIFF_DK), lambda b, s, pt: (0, 0)),
            row, row, row,
            pl.BlockSpec((1, DIFF_DV), lambda b, s, pt: (0, 0)),
        ] + [page_spec(n) for n in range(PAGES_PER_STEP)] * 2,
        out_specs=row,
        scratch_shapes=[pltpu.VMEM((2 * DIFF_HEADS, seq, 1), F32), pltpu.VMEM((2 * DIFF_HEADS, seq, 1), F32),
                        pltpu.VMEM((2 * DIFF_HEADS, seq, DIFF_DV), F32)],
    )
    return pl.pallas_call(
        functools.partial(_diff_sample_kernel, seq=seq, page=page, lam_init=lam_init),
        grid_spec=grid_spec,
        out_shape=jax.ShapeDtypeStruct((batch * seq, SEG), F32),
        compiler_params=_params(("parallel", "arbitrary")),
        name="diff_attention_sample",
    )(page_table, lam_rows, dq, dk_new, dv_new, subln_g.reshape(1, DIFF_DV),
      *([ck] * PAGES_PER_STEP), *([cv] * PAGES_PER_STEP))


def _norm_matmul_kernel(x_ref, g_ref, w_ref, o_ref, *, scale):
    h = _rms(x_ref[...], g_ref[...]).astype(BF16)
    z = jnp.dot(h, w_ref[...], preferred_element_type=F32)
    if scale != 1.0:
        z = z * scale
    o_ref[...] = z.astype(o_ref.dtype)


def _norm_matmul(x2d, g, w_bf, tm, out_dtype, scale=1.0):
    T, D = x2d.shape
    N = w_bf.shape[1]
    return pl.pallas_call(
        functools.partial(_norm_matmul_kernel, scale=scale),
        grid=(T // tm,),
        in_specs=[pl.BlockSpec((tm, D), lambda i: (i, 0)),
                  pl.BlockSpec((1, D), lambda i: (0, 0)),
                  pl.BlockSpec((D, N), lambda i: (0, 0))],
        out_specs=pl.BlockSpec((tm, N), lambda i: (i, 0)),
        out_shape=jax.ShapeDtypeStruct((T, N), out_dtype),
        compiler_params=_params(("parallel",)),
        name="norm_matmul",
    )(x2d, g.reshape(1, D), w_bf)


def _matmul_norm_res_kernel(*refs, n_in):
    a_refs = refs[:n_in]
    w_refs = refs[n_in:2 * n_in]
    g_ref, res_ref, o_ref = refs[2 * n_in:]
    z = None
    for a_ref, w_ref in zip(a_refs, w_refs):
        t = jnp.dot(a_ref[...].astype(BF16), w_ref[...], preferred_element_type=F32)
        z = t if z is None else z + t
    o_ref[...] = res_ref[...] + _rms(z, g_ref[...])


def _matmul_norm_residual(acts, ws_bf, g, res, tm):
    T, D = res.shape
    n_in = len(acts)
    in_specs = ([pl.BlockSpec((tm, a.shape[1]), lambda i: (i, 0)) for a in acts]
                + [pl.BlockSpec(w.shape, lambda i: (0, 0)) for w in ws_bf]
                + [pl.BlockSpec((1, D), lambda i: (0, 0)), pl.BlockSpec((tm, D), lambda i: (i, 0))])
    return pl.pallas_call(
        functools.partial(_matmul_norm_res_kernel, n_in=n_in),
        grid=(T // tm,),
        in_specs=in_specs,
        out_specs=pl.BlockSpec((tm, D), lambda i: (i, 0)),
        out_shape=jax.ShapeDtypeStruct((T, D), F32),
        compiler_params=_params(("parallel",)),
        name="matmul_norm_residual",
    )(*acts, *ws_bf, g.reshape(1, D), res)


def _softmax_rows(s):
    e = jnp.exp(s - jnp.max(s, axis=-1, keepdims=True))
    return e / jnp.sum(e, axis=-1, keepdims=True)


def _cross_prompt_kernel(q_ref, mk_ref, mv_ref, o_ref):
    for h in range(MEM_HEADS):
        sl = slice(h * MEM_DH, (h + 1) * MEM_DH)
        s = lax.dot_general(q_ref[:, sl], mk_ref[0, :, sl], _NT, preferred_element_type=F32)
        p = _softmax_rows(s).astype(BF16)
        o_ref[:, sl] = jnp.dot(p, mv_ref[0, :, sl], preferred_element_type=F32).astype(o_ref.dtype)


def _cross_attention_prompt(q, mk_bf, mv_bf, batch, seq, tq):
    nq = seq // tq
    W = MEM_HEADS * MEM_DH
    M = mk_bf.shape[1]
    kv = pl.BlockSpec((1, M, W), lambda b, i: (b, 0, 0))
    row = pl.BlockSpec((tq, W), lambda b, i: (b * nq + i, 0))
    return pl.pallas_call(
        _cross_prompt_kernel,
        grid=(batch, nq),
        in_specs=[row, kv, kv],
        out_specs=row,
        out_shape=jax.ShapeDtypeStruct((batch * seq, W), BF16),
        compiler_params=_params(("parallel", "parallel")),
        name="cross_attention_prompt",
    )(q, mk_bf, mv_bf)


def _cross_sample_kernel(q_ref, mk_ref, mv_ref, o_ref, *, group, seq):
    rows = []
    for b in range(group):
        heads = []
        for h in range(MEM_HEADS):
            sl = slice(h * MEM_DH, (h + 1) * MEM_DH)
            qb = q_ref[b * seq:(b + 1) * seq, sl].astype(BF16)
            s = lax.dot_general(qb, mk_ref[b, :, sl].astype(BF16), _NT, preferred_element_type=F32)
            p = _softmax_rows(s).astype(BF16)
            heads.append(jnp.dot(p, mv_ref[b, :, sl].astype(BF16), preferred_element_type=F32))
        rows.append(jnp.concatenate(heads, axis=1))
    o_ref[...] = jnp.concatenate(rows, axis=0)


def _cross_attention_sample(q, mem_k, mem_v, batch, seq, group):
    W = MEM_HEADS * MEM_DH
    M = mem_k.shape[1]
    kv = pl.BlockSpec((group, M, W), lambda i: (i, 0, 0))
    row = pl.BlockSpec((group * seq, W), lambda i: (i, 0))
    return pl.pallas_call(
        functools.partial(_cross_sample_kernel, group=group, seq=seq),
        grid=(batch // group,),
        in_specs=[row, kv, kv],
        out_specs=row,
        out_shape=jax.ShapeDtypeStruct((batch * seq, W), F32),
        compiler_params=_params(("parallel",)),
        name="cross_attention_sample",
    )(q, mem_k, mem_v)


def _mlp_kernel(x_ref, gpre_ref, wup_ref, wdn_ref, gpost_ref, o_ref, h_scr):
    j = pl.program_id(1)

    @pl.when(j == 0)
    def _():
        h_scr[...] = _rms(x_ref[...], gpre_ref[...]).astype(BF16)

    u = jnp.maximum(jnp.dot(h_scr[...], wup_ref[...], preferred_element_type=F32), 0.0)
    t = jnp.dot((u * u).astype(BF16), wdn_ref[...], preferred_element_type=F32)

    @pl.when(j == 0)
    def _():
        o_ref[...] = t

    @pl.when(j > 0)
    def _():
        o_ref[...] += t

    @pl.when(j == pl.num_programs(1) - 1)
    def _():
        o_ref[...] = x_ref[...] + _rms(o_ref[...], gpost_ref[...])


def _mlp(x2d, g_pre, w_up_bf, w_dn_bf, g_post, tm, tf):
    T, D = x2d.shape
    FF = w_up_bf.shape[1]
    vec = pl.BlockSpec((1, D), lambda i, j: (0, 0))
    return pl.pallas_call(
        _mlp_kernel,
        grid=(T // tm, FF // tf),
        in_specs=[pl.BlockSpec((tm, D), lambda i, j: (i, 0)), vec,
                  pl.BlockSpec((D, tf), lambda i, j: (0, j)),
                  pl.BlockSpec((tf, D), lambda i, j: (j, 0)), vec],
        out_specs=pl.BlockSpec((tm, D), lambda i, j: (i, 0)),
        out_shape=jax.ShapeDtypeStruct((T, D), F32),
        scratch_shapes=[pltpu.VMEM((tm, D), BF16)],
        compiler_params=_params(("parallel", "arbitrary")),
        name="mlp",
    )(x2d, g_pre.reshape(1, D), w_up_bf, w_dn_bf, g_post.reshape(1, D))


def _position_tables(pos, tm):
    reps = max(1, tm // pos.shape[0])
    pos = jnp.tile(pos, reps)
    cr, srl, srh = _rope_tables(pos, RET_DK, RET_THETA)
    cd, sdl, sdh = _rope_tables(pos, ROPE_DIM, ROPE_THETA)
    return (cr, srl + srh, cd, sdl, sdh), pos.shape[0] // tm


def _layer_tail(x2d, mix_parts, wts, cross_fn, tm):
    (w_out_parts, g_mix_post, g_mem_pre, w_mem_q, w_mem_o, g_mem_post,
     g_mlp_pre, w_up, w_down, g_mlp_post) = wts
    x1 = _matmul_norm_residual(mix_parts, w_out_parts, g_mix_post, x2d, tm)
    q = cross_fn(x1, g_mem_pre, w_mem_q)
    x2 = _matmul_norm_residual([q], [w_mem_o], g_mem_post, x1, tm)
    return _mlp(x2, g_mlp_pre, w_up, w_down, g_mlp_post, tm, 512)


def kernel(x_prompt, x_sample, mem_prompt, state_ret, cache_diff_k, cache_diff_v, cache_mem_k, cache_mem_v, page_table, w_in, w_out, diff_lambda_q1, diff_lambda_k1, diff_lambda_q2, diff_lambda_k2, diff_subln_g, norm_mix_pre, norm_mix_post, norm_mem_pre, norm_mem_post, norm_mlp_pre, norm_mlp_post, mem_norm_g, w_mem_q, w_mem_k, w_mem_v, w_mem_o, w_mlp_up, w_mlp_down):
    depth = w_in.shape[0]
    B, L_p, D = x_prompt.shape
    B_s, L_s, _ = x_sample.shape
    n_pages, page = page_table.shape[1], cache_diff_k.shape[2]
    past_len = n_pages * page
    M = mem_prompt.shape[1]
    W_MEM = MEM_HEADS * MEM_DH
    TM = 512
    ret_chunk_p = math.gcd(L_p, RET_CHUNK)

    tabs_p, per_p = _position_tables(jnp.arange(L_p, dtype=F32), TM)
    TM_IN_S = 256
    tabs_s, per_s = _position_tables(past_len + jnp.arange(L_s, dtype=F32), TM_IN_S)

    yp = x_prompt.reshape(B * L_p, D)
    ys = x_sample.reshape(B_s * L_s, D)
    mem2d = mem_prompt.reshape(B * M, D)
    outs = {k: [] for k in ("rp", "kp", "vp", "mkp", "mvp", "rs", "ks", "vs")}

    for i in range(depth):
        lam_init = 0.8 - 0.6 * math.exp(-0.3 * i)
        lam_rows = jnp.stack([diff_lambda_q1[i], diff_lambda_k1[i], diff_lambda_q2[i], diff_lambda_k2[i]])
        w_in_bf = w_in[i].astype(BF16)
        half = w_out.shape[1] // 2
        tail_w = ([w_out[i, :half].astype(BF16), w_out[i, half:].astype(BF16)], norm_mix_post[i],
                  norm_mem_pre[i], w_mem_q[i].astype(BF16), w_mem_o[i].astype(BF16), norm_mem_post[i],
                  norm_mlp_pre[i], w_mlp_up[i].astype(BF16), w_mlp_down[i].astype(BF16), norm_mlp_post[i])

        mk_p = _norm_matmul(mem2d, mem_norm_g[i], w_mem_k[i].astype(BF16), B * M, F32)
        mv_p = _norm_matmul(mem2d, mem_norm_g[i], w_mem_v[i].astype(BF16), B * M, F32)
        rq, rk, rv, rg, dq, dk, dv = _in_projection(yp, norm_mix_pre[i], w_in_bf, tabs_p, per_p, TM, BF16)
        mix_ret, s_p = _retention_prompt(rq, rk, rv, rg, B, L_p, ret_chunk_p)
        mix_diff = _diff_attention_prompt(dq, dk, dv, lam_rows, diff_subln_g[i], B, L_p, lam_init)
        mk_bf = mk_p.astype(BF16).reshape(B, M, W_MEM)
        mv_bf = mv_p.astype(BF16).reshape(B, M, W_MEM)

        def cross_p(x1, g, wq):
            q = _norm_matmul(x1, g, wq, TM, BF16, scale=MEM_DH ** -0.5)
            return _cross_attention_prompt(q, mk_bf, mv_bf, B, L_p, TM)

        yp = _layer_tail(yp, [mix_ret, mix_diff], tail_w, cross_p, TM)
        outs["rp"].append(s_p.astype(state_ret.dtype))
        outs["kp"].append(dk.reshape(B, L_p, DIFF_HEADS, 2, DIFF_DK))
        outs["vp"].append(dv.reshape(B, L_p, DIFF_HEADS, DIFF_DV))
        outs["mkp"].append(mk_p.reshape(B, M, MEM_HEADS, MEM_DH))
        outs["mvp"].append(mv_p.reshape(B, M, MEM_HEADS, MEM_DH))

        rq, rk, rv, rg, dq, dk, dv = _in_projection(ys, norm_mix_pre[i], w_in_bf, tabs_s, per_s, TM_IN_S, F32)
        mix_ret, s_s = _retention_sample(rq, rk, rv, rg, state_ret[i], B_s, L_s, 4)
        mix_diff = _diff_attention_sample(dq, dk, dv, cache_diff_k[i], cache_diff_v[i], page_table,
                                          lam_rows, diff_subln_g[i], B_s, L_s, lam_init)
        mem_k = cache_mem_k[i].reshape(B_s, M, W_MEM)
        mem_v = cache_mem_v[i].reshape(B_s, M, W_MEM)

        def cross_s(x1, g, wq):
            q = _norm_matmul(x1, g, wq, TM, F32, scale=MEM_DH ** -0.5)
            return _cross_attention_sample(q, mem_k, mem_v, B_s, L_s, 8)

        ys = _layer_tail(ys, [mix_ret, mix_diff], tail_w, cross_s, TM)
        outs["rs"].append(s_s.astype(state_ret.dtype))
        outs["ks"].append(dk.reshape(B_s, L_s, DIFF_HEADS, 2, DIFF_DK))
        outs["vs"].append(dv.reshape(B_s, L_s, DIFF_HEADS, DIFF_DV))

    st = lambda k: jnp.stack(outs[k])
    return (yp.reshape(B, L_p, D), ys.reshape(B_s, L_s, D), st("rp"), st("kp"), st("vp"),
            st("mkp"), st("mvp"), st("rs"), st("ks"), st("vs"))
```

```python
import functools
import math

import jax
import jax.numpy as jnp
from jax import lax
from jax.experimental import pallas as pl
from jax.experimental.pallas import tpu as pltpu

F32 = jnp.float32
BF16 = jnp.bfloat16

LANES = 128
RET_HEADS = 8
RET_DK = 128
RET_DV = 128
RET_THETA = 10000.0
RET_CHUNK = 128
DIFF_HEADS = 4
DIFF_DK = 128
DIFF_DV = 256
ROPE_THETA = 500000.0
ROPE_DIM = DIFF_DK // 4
MEM_HEADS = 4
MEM_DH = 128
Q_BLOCK = 128
EPS = 1e-6
LOG2E = math.log2(math.e)
SEG = 1024
N_SEG = 7

VMEM_LIMIT = 56 * 1024 * 1024

_NT = (((1,), (1,)), ((), ()))
_TN = (((0,), (0,)), ((), ()))


def _params(sem, vmem=VMEM_LIMIT):
    return pltpu.CompilerParams(dimension_semantics=sem, vmem_limit_bytes=vmem)


def _rms(x, g):
    return x * lax.rsqrt(jnp.mean(x * x, axis=-1, keepdims=True) + EPS) * g


def _unit_rms(o):
    return o * lax.rsqrt(jnp.mean(o * o, axis=-1, keepdims=True) + EPS)


def _rope_tables(pos, rot_dim, theta):
    half = rot_dim // 2
    inv = jnp.exp(-math.log(theta) * (2.0 * jnp.arange(half, dtype=F32) / rot_dim))
    ang = pos[:, None] * inv[None, :]
    cos, sin = jnp.cos(ang), jnp.sin(ang)
    n = pos.shape[0]
    rest = LANES - rot_dim
    c = jnp.concatenate([cos, cos, jnp.ones((n, rest), F32)], axis=-1)
    s_lo = jnp.concatenate([-sin, jnp.zeros((n, LANES - half), F32)], axis=-1)
    s_hi = jnp.concatenate([jnp.zeros((n, half), F32), sin, jnp.zeros((n, rest), F32)], axis=-1)
    return c, s_lo, s_hi


def _inproj_kernel(x_ref, g_ref, w_ref, cr_ref, sr_ref, cd_ref, sdl_ref, sdh_ref, *rest,
                   dq_scale, transposed):
    if transposed:
        rq_ref, rk_ref, rv_ref, rg_ref, dq_ref, dk_ref, dv_ref, dvt_ref, h_scr, z_scr = rest
    else:
        rq_ref, rk_ref, rv_ref, rg_ref, dq_ref, dk_ref, dv_ref, h_scr, z_scr = rest
    j = pl.program_id(1)
    heads = [slice(h * LANES, (h + 1) * LANES) for h in range(SEG // LANES)]

    @pl.when(j == 0)
    def _():
        h_scr[...] = _rms(x_ref[...], g_ref[...]).astype(BF16)

    z_scr[...] = jnp.dot(h_scr[...], w_ref[...], preferred_element_type=F32)

    def ret_rot(out_ref, scale):
        c, s = cr_ref[...], sr_ref[...]
        for sl in heads:
            z = z_scr[:, sl]
            r = z * c + pltpu.roll(z, RET_DK // 2, 1) * s
            if scale != 1.0:
                r = r * scale
            out_ref[:, sl] = r.astype(out_ref.dtype)

    def diff_rot(z):
        half = ROPE_DIM // 2
        return (z * cd_ref[...] + pltpu.roll(z, LANES - half, 1) * sdl_ref[...]
                + pltpu.roll(z, half, 1) * sdh_ref[...])

    @pl.when(j == 0)
    def _():
        ret_rot(rq_ref, 1.0)

    @pl.when(j == 1)
    def _():
        ret_rot(rk_ref, RET_DK ** -0.5)

    @pl.when(j == 2)
    def _():
        rv_ref[...] = z_scr[...].astype(rv_ref.dtype)

    @pl.when(j == 3)
    def _():
        rg_ref[...] = z_scr[...]

    @pl.when(j == 4)
    def _():
        for sl in heads:
            r = diff_rot(z_scr[:, sl]) * dq_scale
            if transposed:
                dq_ref[sl, :] = r.T.astype(dq_ref.dtype)
            else:
                dq_ref[:, sl] = r.astype(dq_ref.dtype)

    @pl.when(j == 5)
    def _():
        for sl in heads:
            dk_ref[:, sl] = diff_rot(z_scr[:, sl])

    @pl.when(j == 6)
    def _():
        dv_ref[...] = z_scr[...]
        if transposed:
            for sl in heads:
                dvt_ref[sl, :] = z_scr[:, sl].T.astype(dvt_ref.dtype)


def _in_projection(x2d, g, w_bf, tabs, period_blocks, tm, act_dtype, dq_scale, transposed):
    T, D = x2d.shape
    cr, sr, cd, sdl, sdh = tabs
    tab_spec = pl.BlockSpec((tm, LANES), lambda i, j: (i % period_blocks, 0))
    row_spec = pl.BlockSpec((tm, SEG), lambda i, j: (i, 0))
    col_spec = pl.BlockSpec((SEG, tm), lambda i, j: (0, i))
    sds = lambda dt: jax.ShapeDtypeStruct((T, SEG), dt)
    sds_t = lambda dt: jax.ShapeDtypeStruct((SEG, T), dt)
    out_specs = [row_spec] * 4 + [col_spec if transposed else row_spec, row_spec, row_spec]
    out_shape = [sds(act_dtype), sds(act_dtype), sds(act_dtype), sds(F32),
                 sds_t(act_dtype) if transposed else sds(act_dtype), sds(F32), sds(F32)]
    if transposed:
        out_specs.append(col_spec)
        out_shape.append(sds_t(act_dtype))
    return pl.pallas_call(
        functools.partial(_inproj_kernel, dq_scale=dq_scale, transposed=transposed),
        grid=(T // tm, N_SEG),
        in_specs=[
            pl.BlockSpec((tm, D), lambda i, j: (i, 0)),
            pl.BlockSpec((1, D), lambda i, j: (0, 0)),
            pl.BlockSpec((D, SEG), lambda i, j: (0, j)),
            tab_spec, tab_spec, tab_spec, tab_spec, tab_spec,
        ],
        out_specs=out_specs,
        out_shape=out_shape,
        scratch_shapes=[pltpu.VMEM((tm, D), BF16), pltpu.VMEM((tm, SEG), F32)],
        compiler_params=_params(("parallel", "arbitrary")),
        name="in_projection",
    )(x2d, g.reshape(1, D), w_bf, cr, sr, cd, sdl, sdh)


def _ret_decay_tables(chunk):
    lg = jnp.log1p(-jnp.exp2(-5.0 - jnp.arange(RET_HEADS, dtype=F32)))
    idx = jnp.arange(chunk, dtype=F32)
    rel = idx[:, None] - idx[None, :]
    dmat = jnp.where(rel[None] >= 0, jnp.exp(jnp.maximum(rel, 0.0)[None] * lg[:, None, None]), 0.0)
    q_decay = jnp.exp((idx + 1.0)[:, None] * lg[None, :])
    k_decay = jnp.exp((chunk - 1.0 - idx)[:, None] * lg[None, :])
    chunk_decay = jnp.exp(chunk * lg)
    widen = lambda t: jnp.repeat(t, RET_DK, axis=1)
    return dmat, widen(q_decay), widen(k_decay), chunk_decay


def _ret_head(q, k, v, s, dmat, qd, kd, cd, gate):
    qb, kb, vb = q.astype(BF16), k.astype(BF16), v.astype(BF16)
    att = lax.dot_general(qb, kb, _NT, preferred_element_type=F32) * dmat
    o = (jnp.dot(att.astype(BF16), vb, preferred_element_type=F32)
         + jnp.dot(qb, s.astype(BF16), preferred_element_type=F32) * qd)
    kdk = (k.astype(F32) * kd).astype(BF16)
    s_new = s * cd + lax.dot_general(kdk, vb, _TN, preferred_element_type=F32)
    y = _unit_rms(o) * (gate * jax.nn.sigmoid(gate))
    return y, s_new


def _ret_prompt_kernel(cd_ref, q_ref, k_ref, v_ref, g_ref, dmat_ref, qd_ref, kd_ref, mix_ref, s_ref):
    @pl.when(pl.program_id(1) == 0)
    def _():
        s_ref[...] = jnp.zeros_like(s_ref)

    for h in range(RET_HEADS):
        sl = slice(h * RET_DK, (h + 1) * RET_DK)
        y, s_new = _ret_head(q_ref[:, sl], k_ref[:, sl], v_ref[:, sl], s_ref[0, h], dmat_ref[h],
                             qd_ref[:, sl], kd_ref[:, sl], cd_ref[h], g_ref[:, sl])
        s_ref[0, h] = s_new
        mix_ref[:, sl] = y.astype(mix_ref.dtype)


def _retention_prompt(rq, rk, rv, rg, batch, seq, chunk):
    n = seq // chunk
    dmat, qd, kd, cd = _ret_decay_tables(chunk)
    row = pl.BlockSpec((chunk, SEG), lambda b, c: (b * n + c, 0))
    full2 = pl.BlockSpec((chunk, SEG), lambda b, c: (0, 0))
    return pl.pallas_call(
        _ret_prompt_kernel,
        grid=(batch, n),
        in_specs=[
            pl.BlockSpec(memory_space=pltpu.SMEM),
            row, row, row, row,
            pl.BlockSpec((RET_HEADS, chunk, chunk), lambda b, c: (0, 0, 0)),
            full2, full2,
        ],
        out_specs=[row, pl.BlockSpec((1, RET_HEADS, RET_DK, RET_DV), lambda b, c: (b, 0, 0, 0))],
        out_shape=[jax.ShapeDtypeStruct((batch * seq, SEG), BF16),
                   jax.ShapeDtypeStruct((batch, RET_HEADS, RET_DK, RET_DV), F32)],
        compiler_params=_params(("parallel", "arbitrary")),
        name="retention_prompt",
    )(cd, rq, rk, rv, rg, dmat, qd, kd)


def _ret_sample_kernel(cd_ref, q_ref, k_ref, v_ref, g_ref, s0_ref, dmat_ref, qd_ref, kd_ref,
                       mix_ref, s_ref, *, group, seq):
    rows = []
    for b in range(group):
        r = slice(b * seq, (b + 1) * seq)
        heads = []
        for h in range(RET_HEADS):
            sl = slice(h * RET_DK, (h + 1) * RET_DK)
            y, s_new = _ret_head(q_ref[r, sl], k_ref[r, sl], v_ref[r, sl], s0_ref[b, h], dmat_ref[h],
                                 qd_ref[:, sl], kd_ref[:, sl], cd_ref[h], g_ref[r, sl])
            s_ref[b, h] = s_new
            heads.append(y)
        rows.append(jnp.concatenate(heads, axis=1))
    mix_ref[...] = jnp.concatenate(rows, axis=0)


def _retention_sample(rq, rk, rv, rg, state, layer, batch, seq, group):
    dmat, qd, kd, cd = _ret_decay_tables(seq)
    row = pl.BlockSpec((group * seq, SEG), lambda i: (i, 0))
    tab = pl.BlockSpec((seq, SEG), lambda i: (0, 0))
    st = pl.BlockSpec((group, RET_HEADS, RET_DK, RET_DV), lambda i: (i, 0, 0, 0))
    st_in = pl.BlockSpec((None, group, RET_HEADS, RET_DK, RET_DV), lambda i: (layer, i, 0, 0, 0))
    return pl.pallas_call(
        functools.partial(_ret_sample_kernel, group=group, seq=seq),
        grid=(batch // group,),
        in_specs=[
            pl.BlockSpec(memory_space=pltpu.SMEM),
            row, row, row, row, st_in,
            pl.BlockSpec((RET_HEADS, seq, seq), lambda i: (0, 0, 0)),
            tab, tab,
        ],
        out_specs=[row, st],
        out_shape=[jax.ShapeDtypeStruct((batch * seq, SEG), F32),
                   jax.ShapeDtypeStruct((batch, RET_HEADS, RET_DK, RET_DV), F32)],
        compiler_params=_params(("parallel",)),
        name="retention_sample",
    )(cd, rq, rk, rv, rg, state, dmat, qd, kd)


def _lambda_value(lam_ref, lam_init):
    a = jnp.sum(lam_ref[0:1, :] * lam_ref[1:2, :], axis=-1, keepdims=True)
    b = jnp.sum(lam_ref[2:3, :] * lam_ref[3:4, :], axis=-1, keepdims=True)
    return jnp.exp(a) - jnp.exp(b) + lam_init


def _softmax_update(s, m_old, l_old):
    m_new = jnp.maximum(m_old, jnp.max(s, axis=-1, keepdims=True))
    alpha = jnp.exp(m_old - m_new)
    p = jnp.exp(s - m_new)
    l_new = alpha * l_old + jnp.sum(p, axis=-1, keepdims=True)
    return p, alpha, m_new, l_new


def _diff_prompt_kernel(qi_tab, ki_tab, lam_ref, qt_ref, k_ref, vt_ref, g_ref, o_ref, m_scr, l_scr, acc_scr,
                        *, tq, tk, lam_init):
    t = pl.program_id(2)
    qi, ki = qi_tab[t], ki_tab[t]

    @pl.when(ki == 0)
    def _():
        m_scr[...] = jnp.full_like(m_scr, -jnp.inf)
        l_scr[...] = jnp.zeros_like(l_scr)
        acc_scr[...] = jnp.zeros_like(acc_scr)

    def step(masked):
        vt = vt_ref[...]
        if masked:
            kpos = ki * tk + lax.broadcasted_iota(jnp.int32, (tk, tq), 0)
            qpos = qi * tq + lax.broadcasted_iota(jnp.int32, (tk, tq), 1)
            keep = qpos >= kpos
        for m in range(2):
            sl = slice(m * DIFF_DK, (m + 1) * DIFF_DK)
            st = jnp.dot(k_ref[:, sl].astype(BF16), qt_ref[sl, :], preferred_element_type=F32)
            if masked:
                st = jnp.where(keep, st, -jnp.inf)
            m_old = m_scr[m]
            m_new = jnp.maximum(m_old, jnp.max(st, axis=0, keepdims=True))
            alpha = jnp.exp2(m_old - m_new)
            p = jnp.exp2(st - m_new)
            l_scr[m] = alpha * l_scr[m] + jnp.sum(p, axis=0, keepdims=True)
            m_scr[m] = m_new
            acc_scr[m] = alpha * acc_scr[m] + jnp.dot(vt, p.astype(BF16), preferred_element_type=F32)

    @pl.when(ki < qi)
    def _():
        step(False)

    @pl.when(ki == qi)
    def _():
        step(True)
        lam = _lambda_value(lam_ref, lam_init)
        ot = acc_scr[0] * (1.0 / l_scr[0]) - lam * (acc_scr[1] * (1.0 / l_scr[1]))
        yt = ot * lax.rsqrt(jnp.mean(ot * ot, axis=0, keepdims=True) + EPS)
        o_ref[...] = (yt.T * g_ref[...] * (1.0 - lam_init)).astype(o_ref.dtype)


def _diff_attention_prompt(dq_t, dk, dv_t, lam_rows, subln_g, batch, seq, lam_init, tq=512, tk=512):
    assert tq == tk
    nq = seq // tq
    pairs = [(qi, ki) for qi in range(nq) for ki in range(qi + 1)]
    qi_tab = jnp.asarray([p[0] for p in pairs], jnp.int32)
    ki_tab = jnp.asarray([p[1] for p in pairs], jnp.int32)
    hw = 2 * DIFF_DK
    grid_spec = pltpu.PrefetchScalarGridSpec(
        num_scalar_prefetch=2,
        grid=(batch, DIFF_HEADS, len(pairs)),
        in_specs=[
            pl.BlockSpec((4, DIFF_DK), lambda b, h, t, qt, kt: (0, 0)),
            pl.BlockSpec((hw, tq), lambda b, h, t, qt, kt: (h, b * nq + qt[t])),
            pl.BlockSpec((tk, hw), lambda b, h, t, qt, kt: (b * nq + kt[t], h)),
            pl.BlockSpec((DIFF_DV, tk), lambda b, h, t, qt, kt: (h, b * nq + kt[t])),
            pl.BlockSpec((1, DIFF_DV), lambda b, h, t, qt, kt: (0, 0)),
        ],
        out_specs=pl.BlockSpec((tq, DIFF_DV), lambda b, h, t, qt, kt: (b * nq + qt[t], h)),
        scratch_shapes=[pltpu.VMEM((2, 1, tq), F32), pltpu.VMEM((2, 1, tq), F32),
                        pltpu.VMEM((2, DIFF_DV, tq), F32)],
    )
    return pl.pallas_call(
        functools.partial(_diff_prompt_kernel, tq=tq, tk=tk, lam_init=lam_init),
        grid_spec=grid_spec,
        out_shape=jax.ShapeDtypeStruct((batch * seq, SEG), BF16),
        compiler_params=_params(("parallel", "parallel", "arbitrary")),
        name="diff_attention_prompt",
    )(qi_tab, ki_tab, lam_rows, dq_t, dk, dv_t, subln_g.reshape(1, DIFF_DV))


PAGES_PER_STEP = 8


def _diff_sample_kernel(pt_ref, lam_ref, q_ref, kn_ref, vn_ref, g_ref, *rest, seq, page, lam_init):
    k_refs = rest[:PAGES_PER_STEP]
    v_refs = rest[PAGES_PER_STEP:2 * PAGES_PER_STEP]
    o_ref, m_scr, l_scr, acc_scr = rest[2 * PAGES_PER_STEP:]
    step = pl.program_id(1)
    n_sub = 2 * DIFF_HEADS

    @pl.when(step == 0)
    def _():
        m_scr[...] = jnp.full_like(m_scr, -jnp.inf)
        l_scr[...] = jnp.zeros_like(l_scr)
        acc_scr[...] = jnp.zeros_like(acc_scr)

    def attend(scores_fn, pv_fn):
        for h in range(DIFF_HEADS):
            ps = []
            for m in range(2):
                i = 2 * h + m
                p, alpha, m_new, l_new = _softmax_update(scores_fn(i), m_scr[i], l_scr[i])
                m_scr[i] = m_new
                l_scr[i] = l_new
                acc_scr[i] = alpha * acc_scr[i]
                ps.append(p)
            pv = pv_fn(h, jnp.concatenate(ps, axis=0).astype(BF16))
            acc_scr[2 * h] += pv[:seq]
            acc_scr[2 * h + 1] += pv[seq:]

    q = q_ref[...]
    qs = [q[:, i * DIFF_DK:(i + 1) * DIFF_DK].astype(BF16) for i in range(n_sub)]

    def past_scores(i):
        sl = slice(i * DIFF_DK, (i + 1) * DIFF_DK)
        return jnp.concatenate(
            [lax.dot_general(qs[i], kr[:, sl].astype(BF16), _NT, preferred_element_type=F32)
             for kr in k_refs], axis=1)

    def past_pv(h, p):
        sl = slice(h * DIFF_DV, (h + 1) * DIFF_DV)
        out = None
        for n, vr in enumerate(v_refs):
            t = jnp.dot(p[:, n * page:(n + 1) * page], vr[:, sl].astype(BF16), preferred_element_type=F32)
            out = t if out is None else out + t
        return out

    attend(past_scores, past_pv)

    @pl.when(step == pl.num_programs(1) - 1)
    def _():
        kn, vn = kn_ref[...], vn_ref[...]
        causal = (lax.broadcasted_iota(jnp.int32, (seq, seq), 0)
                  >= lax.broadcasted_iota(jnp.int32, (seq, seq), 1))

        def new_scores(i):
            sl = slice(i * DIFF_DK, (i + 1) * DIFF_DK)
            s = lax.dot_general(qs[i], kn[:, sl].astype(BF16), _NT, preferred_element_type=F32)
            return jnp.where(causal, s, -jnp.inf)

        def new_pv(h, p):
            sl = slice(h * DIFF_DV, (h + 1) * DIFF_DV)
            return jnp.dot(p, vn[:, sl].astype(BF16), preferred_element_type=F32)

        attend(new_scores, new_pv)
        lam = _lambda_value(lam_ref, lam_init)
        outs = []
        for h in range(DIFF_HEADS):
            o = acc_scr[2 * h] / l_scr[2 * h] - lam * (acc_scr[2 * h + 1] / l_scr[2 * h + 1])
            outs.append(_unit_rms(o) * g_ref[...] * (1.0 - lam_init))
        o_ref[...] = jnp.concatenate(outs, axis=1)


def _diff_attention_sample(dq, dk_new, dv_new, cache_k, cache_v, layer, page_table, lam_rows, subln_g,
                           batch, seq, lam_init):
    depth, n_pool, page = cache_k.shape[:3]
    n_pages = page_table.shape[1]
    assert n_pages % PAGES_PER_STEP == 0
    ck = cache_k.reshape(depth, n_pool, page, SEG)
    cv = cache_v.reshape(depth, n_pool, page, SEG)
    row = pl.BlockSpec((seq, SEG), lambda b, s, pt: (b, 0))

    def page_spec(n):
        return pl.BlockSpec((None, None, page, SEG),
                            lambda b, s, pt: (layer, pt[b, s * PAGES_PER_STEP + n], 0, 0))

    grid_spec = pltpu.PrefetchScalarGridSpec(
        num_scalar_prefetch=1,
        grid=(batch, n_pages // PAGES_PER_STEP),
        in_specs=[
            pl.BlockSpec((4, DIFF_DK), lambda b, s, pt: (0, 0)),
            row, row, row,
            pl.BlockSpec((1, DIFF_DV), lambda b, s, pt: (0, 0)),
        ] + [page_spec(n) for n in range(PAGES_PER_STEP)] * 2,
        out_specs=row,
        scratch_shapes=[pltpu.VMEM((2 * DIFF_HEADS, seq, 1), F32), pltpu.VMEM((2 * DIFF_HEADS, seq, 1), F32),
                        pltpu.VMEM((2 * DIFF_HEADS, seq, DIFF_DV), F32)],
    )
    return pl.pallas_call(
        functools.partial(_diff_sample_kernel, seq=seq, page=page, lam_init=lam_init),
        grid_spec=grid_spec,
        out_shape=jax.ShapeDtypeStruct((batch * seq, SEG), F32),
        compiler_params=_params(("parallel", "arbitrary")),
        name="diff_attention_sample",
    )(page_table, lam_rows, dq, dk_new, dv_new, subln_g.reshape(1, DIFF_DV),
      *([ck] * PAGES_PER_STEP), *([cv] * PAGES_PER_STEP))


def _norm_matmul_kernel(x_ref, g_ref, w_ref, o_ref, *, scale):
    h = _rms(x_ref[...], g_ref[...]).astype(BF16)
    z = jnp.dot(h, w_ref[...], preferred_element_type=F32)
    if scale != 1.0:
        z = z * scale
    o_ref[...] = z.astype(o_ref.dtype)


def _norm_matmul(x2d, g, w_bf, tm, out_dtype, scale=1.0):
    T, D = x2d.shape
    N = w_bf.shape[1]
    return pl.pallas_call(
        functools.partial(_norm_matmul_kernel, scale=scale),
        grid=(T // tm,),
        in_specs=[pl.BlockSpec((tm, D), lambda i: (i, 0)),
                  pl.BlockSpec((1, D), lambda i: (0, 0)),
                  pl.BlockSpec((D, N), lambda i: (0, 0))],
        out_specs=pl.BlockSpec((tm, N), lambda i: (i, 0)),
        out_shape=jax.ShapeDtypeStruct((T, N), out_dtype),
        compiler_params=_params(("parallel",)),
        name="norm_matmul",
    )(x2d, g.reshape(1, D), w_bf)


def _matmul_norm_res_kernel(*refs, n_in):
    a_refs = refs[:n_in]
    w_refs = refs[n_in:2 * n_in]
    g_ref, res_ref, o_ref = refs[2 * n_in:]
    z = None
    for a_ref, w_ref in zip(a_refs, w_refs):
        t = jnp.dot(a_ref[...].astype(BF16), w_ref[...], preferred_element_type=F32)
        z = t if z is None else z + t
    o_ref[...] = res_ref[...] + _rms(z, g_ref[...])


def _matmul_norm_residual(acts, ws_bf, g, res, tm):
    T, D = res.shape
    n_in = len(acts)
    in_specs = ([pl.BlockSpec((tm, a.shape[1]), lambda i: (i, 0)) for a in acts]
                + [pl.BlockSpec(w.shape, lambda i: (0, 0)) for w in ws_bf]
                + [pl.BlockSpec((1, D), lambda i: (0, 0)), pl.BlockSpec((tm, D), lambda i: (i, 0))])
    return pl.pallas_call(
        functools.partial(_matmul_norm_res_kernel, n_in=n_in),
        grid=(T // tm,),
        in_specs=in_specs,
        out_specs=pl.BlockSpec((tm, D), lambda i: (i, 0)),
        out_shape=jax.ShapeDtypeStruct((T, D), F32),
        compiler_params=_params(("parallel",)),
        name="matmul_norm_residual",
    )(*acts, *ws_bf, g.reshape(1, D), res)


def _softmax_rows(s):
    e = jnp.exp(s - jnp.max(s, axis=-1, keepdims=True))
    return e / jnp.sum(e, axis=-1, keepdims=True)


def _cross_prompt_kernel(q_ref, mk_ref, mv_ref, o_ref):
    for h in range(MEM_HEADS):
        sl = slice(h * MEM_DH, (h + 1) * MEM_DH)
        s = lax.dot_general(q_ref[:, sl], mk_ref[0, :, sl], _NT, preferred_element_type=F32)
        p = _softmax_rows(s).astype(BF16)
        o_ref[:, sl] = jnp.dot(p, mv_ref[0, :, sl], preferred_element_type=F32).astype(o_ref.dtype)


def _cross_attention_prompt(q, mk_bf, mv_bf, batch, seq, tq):
    nq = seq // tq
    W = MEM_HEADS * MEM_DH
    M = mk_bf.shape[1]
    kv = pl.BlockSpec((1, M, W), lambda b, i: (b, 0, 0))
    row = pl.BlockSpec((tq, W), lambda b, i: (b * nq + i, 0))
    return pl.pallas_call(
        _cross_prompt_kernel,
        grid=(batch, nq),
        in_specs=[row, kv, kv],
        out_specs=row,
        out_shape=jax.ShapeDtypeStruct((batch * seq, W), BF16),
        compiler_params=_params(("parallel", "parallel")),
        name="cross_attention_prompt",
    )(q, mk_bf, mv_bf)


def _cross_sample_kernel(q_ref, mk_ref, mv_ref, o_ref, *, group, seq):
    rows = []
    for b in range(group):
        heads = []
        for h in range(MEM_HEADS):
            sl = slice(h * MEM_DH, (h + 1) * MEM_DH)
            qb = q_ref[b * seq:(b + 1) * seq, sl].astype(BF16)
            s = lax.dot_general(qb, mk_ref[b, :, sl].astype(BF16), _NT, preferred_element_type=F32)
            p = _softmax_rows(s).astype(BF16)
            heads.append(jnp.dot(p, mv_ref[b, :, sl].astype(BF16), preferred_element_type=F32))
        rows.append(jnp.concatenate(heads, axis=1))
    o_ref[...] = jnp.concatenate(rows, axis=0)


def _cross_attention_sample(q, mem_k, mem_v, layer, batch, seq, group):
    W = MEM_HEADS * MEM_DH
    M = mem_k.shape[2]
    kv = pl.BlockSpec((None, group, M, W), lambda i: (layer, i, 0, 0))
    row = pl.BlockSpec((group * seq, W), lambda i: (i, 0))
    return pl.pallas_call(
        functools.partial(_cross_sample_kernel, group=group, seq=seq),
        grid=(batch // group,),
        in_specs=[row, kv, kv],
        out_specs=row,
        out_shape=jax.ShapeDtypeStruct((batch * seq, W), F32),
        compiler_params=_params(("parallel",)),
        name="cross_attention_sample",
    )(q, mem_k, mem_v)


def _mlp_kernel(x_ref, gpre_ref, wup_ref, wdn_ref, gpost_ref, o_ref, h_scr):
    j = pl.program_id(1)

    @pl.when(j == 0)
    def _():
        h_scr[...] = _rms(x_ref[...], gpre_ref[...]).astype(BF16)

    u = jnp.maximum(jnp.dot(h_scr[...], wup_ref[...], preferred_element_type=F32), 0.0)
    t = jnp.dot((u * u).astype(BF16), wdn_ref[...], preferred_element_type=F32)

    @pl.when(j == 0)
    def _():
        o_ref[...] = t

    @pl.when(j > 0)
    def _():
        o_ref[...] += t

    @pl.when(j == pl.num_programs(1) - 1)
    def _():
        o_ref[...] = x_ref[...] + _rms(o_ref[...], gpost_ref[...])


def _mlp(x2d, g_pre, w_up_bf, w_dn_bf, g_post, tm, tf):
    T, D = x2d.shape
    FF = w_up_bf.shape[1]
    vec = pl.BlockSpec((1, D), lambda i, j: (0, 0))
    return pl.pallas_call(
        _mlp_kernel,
        grid=(T // tm, FF // tf),
        in_specs=[pl.BlockSpec((tm, D), lambda i, j: (i, 0)), vec,
                  pl.BlockSpec((D, tf), lambda i, j: (0, j)),
                  pl.BlockSpec((tf, D), lambda i, j: (j, 0)), vec],
        out_specs=pl.BlockSpec((tm, D), lambda i, j: (i, 0)),
        out_shape=jax.ShapeDtypeStruct((T, D), F32),
        scratch_shapes=[pltpu.VMEM((tm, D), BF16)],
        compiler_params=_params(("parallel", "arbitrary")),
        name="mlp",
    )(x2d, g_pre.reshape(1, D), w_up_bf, w_dn_bf, g_post.reshape(1, D))


def _position_tables(pos, tm):
    reps = max(1, tm // pos.shape[0])
    pos = jnp.tile(pos, reps)
    cr, srl, srh = _rope_tables(pos, RET_DK, RET_THETA)
    cd, sdl, sdh = _rope_tables(pos, ROPE_DIM, ROPE_THETA)
    return (cr, srl + srh, cd, sdl, sdh), pos.shape[0] // tm


def _layer_tail(x2d, mix_parts, wts, cross_fn, tm):
    (w_out_parts, g_mix_post, g_mem_pre, w_mem_q, w_mem_o, g_mem_post,
     g_mlp_pre, w_up, w_down, g_mlp_post) = wts
    x1 = _matmul_norm_residual(mix_parts, w_out_parts, g_mix_post, x2d, tm)
    q = cross_fn(x1, g_mem_pre, w_mem_q)
    x2 = _matmul_norm_residual([q], [w_mem_o], g_mem_post, x1, tm)
    return _mlp(x2, g_mlp_pre, w_up, w_down, g_mlp_post, tm, 512)


def kernel(x_prompt, x_sample, mem_prompt, state_ret, cache_diff_k, cache_diff_v, cache_mem_k, cache_mem_v, page_table, w_in, w_out, diff_lambda_q1, diff_lambda_k1, diff_lambda_q2, diff_lambda_k2, diff_subln_g, norm_mix_pre, norm_mix_post, norm_mem_pre, norm_mem_post, norm_mlp_pre, norm_mlp_post, mem_norm_g, w_mem_q, w_mem_k, w_mem_v, w_mem_o, w_mlp_up, w_mlp_down):
    depth = w_in.shape[0]
    B, L_p, D = x_prompt.shape
    B_s, L_s, _ = x_sample.shape
    n_pages, page = page_table.shape[1], cache_diff_k.shape[2]
    past_len = n_pages * page
    M = mem_prompt.shape[1]
    W_MEM = MEM_HEADS * MEM_DH
    TM = 512
    ret_chunk_p = math.gcd(L_p, RET_CHUNK)

    tabs_p, per_p = _position_tables(jnp.arange(L_p, dtype=F32), TM)
    TM_IN_S = 256
    tabs_s, per_s = _position_tables(past_len + jnp.arange(L_s, dtype=F32), TM_IN_S)

    yp = x_prompt.reshape(B * L_p, D)
    ys = x_sample.reshape(B_s * L_s, D)
    mem2d = mem_prompt.reshape(B * M, D)
    outs = {k: [] for k in ("rp", "kp", "vp", "mkp", "mvp", "rs", "ks", "vs")}

    for i in range(depth):
        lam_init = 0.8 - 0.6 * math.exp(-0.3 * i)
        lam_rows = jnp.stack([diff_lambda_q1[i], diff_lambda_k1[i], diff_lambda_q2[i], diff_lambda_k2[i]])
        w_in_bf = w_in[i].astype(BF16)
        half = w_out.shape[1] // 2
        tail_w = ([w_out[i, :half].astype(BF16), w_out[i, half:].astype(BF16)], norm_mix_post[i],
                  norm_mem_pre[i], w_mem_q[i].astype(BF16), w_mem_o[i].astype(BF16), norm_mem_post[i],
                  norm_mlp_pre[i], w_mlp_up[i].astype(BF16), w_mlp_down[i].astype(BF16), norm_mlp_post[i])

        mk_p = _norm_matmul(mem2d, mem_norm_g[i], w_mem_k[i].astype(BF16), B * M, F32)
        mv_p = _norm_matmul(mem2d, mem_norm_g[i], w_mem_v[i].astype(BF16), B * M, F32)
        rq, rk, rv, rg, dq_t, dk, dv, dv_t = _in_projection(
            yp, norm_mix_pre[i], w_in_bf, tabs_p, per_p, TM, BF16, DIFF_DK ** -0.5 * LOG2E, True)
        mix_ret, s_p = _retention_prompt(rq, rk, rv, rg, B, L_p, ret_chunk_p)
        mix_diff = _diff_attention_prompt(dq_t, dk, dv_t, lam_rows, diff_subln_g[i], B, L_p, lam_init)
        mk_bf = mk_p.astype(BF16).reshape(B, M, W_MEM)
        mv_bf = mv_p.astype(BF16).reshape(B, M, W_MEM)

        def cross_p(x1, g, wq):
            q = _norm_matmul(x1, g, wq, TM, BF16, scale=MEM_DH ** -0.5)
            return _cross_attention_prompt(q, mk_bf, mv_bf, B, L_p, TM)

        yp = _layer_tail(yp, [mix_ret, mix_diff], tail_w, cross_p, TM)
        outs["rp"].append(s_p.astype(state_ret.dtype))
        outs["kp"].append(dk.reshape(B, L_p, DIFF_HEADS, 2, DIFF_DK))
        outs["vp"].append(dv.reshape(B, L_p, DIFF_HEADS, DIFF_DV))
        outs["mkp"].append(mk_p.reshape(B, M, MEM_HEADS, MEM_DH))
        outs["mvp"].append(mv_p.reshape(B, M, MEM_HEADS, MEM_DH))

        rq, rk, rv, rg, dq, dk, dv = _in_projection(ys, norm_mix_pre[i], w_in_bf, tabs_s, per_s, TM_IN_S, F32,
                                                    DIFF_DK ** -0.5, False)
        mix_ret, s_s = _retention_sample(rq, rk, rv, rg, state_ret, i, B_s, L_s, 4)
        mix_diff = _diff_attention_sample(dq, dk, dv, cache_diff_k, cache_diff_v, i, page_table,
                                          lam_rows, diff_subln_g[i], B_s, L_s, lam_init)
        mem_k = cache_mem_k.reshape(depth, B_s, M, W_MEM)
        mem_v = cache_mem_v.reshape(depth, B_s, M, W_MEM)

        def cross_s(x1, g, wq):
            q = _norm_matmul(x1, g, wq, TM, F32, scale=MEM_DH ** -0.5)
            return _cross_attention_sample(q, mem_k, mem_v, i, B_s, L_s, 8)

        ys = _layer_tail(ys, [mix_ret, mix_diff], tail_w, cross_s, TM)
        outs["rs"].append(s_s.astype(state_ret.dtype))
        outs["ks"].append(dk.reshape(B_s, L_s, DIFF_HEADS, 2, DIFF_DK))
        outs["vs"].append(dv.reshape(B_s, L_s, DIFF_HEADS, DIFF_DV))

    st = lambda k: outs[k][0][None] if depth == 1 else jnp.stack(outs[k])
    return (yp.reshape(B, L_p, D), ys.reshape(B_s, L_s, D), st("rp"), st("kp"), st("vp"),
            st("mkp"), st("mvp"), st("rs"), st("ks"), st("vs"))
```

```python
import functools
import math

import jax
import jax.numpy as jnp
from jax import lax
from jax.experimental import pallas as pl
from jax.experimental.pallas import tpu as pltpu

F32 = jnp.float32
BF16 = jnp.bfloat16

LANES = 128
RET_HEADS = 8
RET_DK = 128
RET_DV = 128
RET_THETA = 10000.0
RET_CHUNK = 128
DIFF_HEADS = 4
DIFF_DK = 128
DIFF_DV = 256
ROPE_THETA = 500000.0
ROPE_DIM = DIFF_DK // 4
MEM_HEADS = 4
MEM_DH = 128
Q_BLOCK = 128
EPS = 1e-6
LOG2E = math.log2(math.e)
SEG = 1024
N_SEG = 7

VMEM_LIMIT = 56 * 1024 * 1024

_NT = (((1,), (1,)), ((), ()))
_TN = (((0,), (0,)), ((), ()))


def _params(sem, vmem=VMEM_LIMIT):
    return pltpu.CompilerParams(dimension_semantics=sem, vmem_limit_bytes=vmem)


def _rms(x, g):
    return x * lax.rsqrt(jnp.mean(x * x, axis=-1, keepdims=True) + EPS) * g


def _unit_rms(o):
    return o * lax.rsqrt(jnp.mean(o * o, axis=-1, keepdims=True) + EPS)


def _rope_tables(pos, rot_dim, theta):
    half = rot_dim // 2
    inv = jnp.exp(-math.log(theta) * (2.0 * jnp.arange(half, dtype=F32) / rot_dim))
    ang = pos[:, None] * inv[None, :]
    cos, sin = jnp.cos(ang), jnp.sin(ang)
    n = pos.shape[0]
    rest = LANES - rot_dim
    c = jnp.concatenate([cos, cos, jnp.ones((n, rest), F32)], axis=-1)
    s_lo = jnp.concatenate([-sin, jnp.zeros((n, LANES - half), F32)], axis=-1)
    s_hi = jnp.concatenate([jnp.zeros((n, half), F32), sin, jnp.zeros((n, rest), F32)], axis=-1)
    return c, s_lo, s_hi


def _inproj_kernel(x_ref, g_ref, w_ref, cr_ref, sr_ref, cd_ref, sdl_ref, sdh_ref, *rest,
                   dq_scale, transposed):
    if transposed:
        rq_ref, rk_ref, rv_ref, rg_ref, dq_ref, dk_ref, dv_ref, dvt_ref, h_scr, z_scr = rest
    else:
        rq_ref, rk_ref, rv_ref, rg_ref, dq_ref, dk_ref, dv_ref, h_scr, z_scr = rest
    j = pl.program_id(1)
    heads = [slice(h * LANES, (h + 1) * LANES) for h in range(SEG // LANES)]

    @pl.when(j == 0)
    def _():
        h_scr[...] = _rms(x_ref[...], g_ref[...]).astype(BF16)

    z_scr[...] = jnp.dot(h_scr[...], w_ref[...], preferred_element_type=F32)

    def ret_rot(out_ref, scale):
        c, s = cr_ref[...], sr_ref[...]
        for sl in heads:
            z = z_scr[:, sl]
            r = z * c + pltpu.roll(z, RET_DK // 2, 1) * s
            if scale != 1.0:
                r = r * scale
            out_ref[:, sl] = r.astype(out_ref.dtype)

    def diff_rot(z):
        half = ROPE_DIM // 2
        return (z * cd_ref[...] + pltpu.roll(z, LANES - half, 1) * sdl_ref[...]
                + pltpu.roll(z, half, 1) * sdh_ref[...])

    @pl.when(j == 0)
    def _():
        ret_rot(rq_ref, 1.0)

    @pl.when(j == 1)
    def _():
        ret_rot(rk_ref, RET_DK ** -0.5)

    @pl.when(j == 2)
    def _():
        rv_ref[...] = z_scr[...].astype(rv_ref.dtype)

    @pl.when(j == 3)
    def _():
        rg_ref[...] = z_scr[...]

    @pl.when(j == 4)
    def _():
        for sl in heads:
            r = diff_rot(z_scr[:, sl]) * dq_scale
            if transposed:
                dq_ref[sl, :] = r.T.astype(dq_ref.dtype)
            else:
                dq_ref[:, sl] = r.astype(dq_ref.dtype)

    @pl.when(j == 5)
    def _():
        for sl in heads:
            dk_ref[:, sl] = diff_rot(z_scr[:, sl])

    @pl.when(j == 6)
    def _():
        dv_ref[...] = z_scr[...]
        if transposed:
            for sl in heads:
                dvt_ref[sl, :] = z_scr[:, sl].T.astype(dvt_ref.dtype)


def _in_projection(x2d, g, w_bf, tabs, period_blocks, tm, act_dtype, dq_scale, transposed):
    T, D = x2d.shape
    cr, sr, cd, sdl, sdh = tabs
    tab_spec = pl.BlockSpec((tm, LANES), lambda i, j: (i % period_blocks, 0))
    row_spec = pl.BlockSpec((tm, SEG), lambda i, j: (i, 0))
    col_spec = pl.BlockSpec((SEG, tm), lambda i, j: (0, i))
    sds = lambda dt: jax.ShapeDtypeStruct((T, SEG), dt)
    sds_t = lambda dt: jax.ShapeDtypeStruct((SEG, T), dt)
    out_specs = [row_spec] * 4 + [col_spec if transposed else row_spec, row_spec, row_spec]
    out_shape = [sds(act_dtype), sds(act_dtype), sds(act_dtype), sds(F32),
                 sds_t(act_dtype) if transposed else sds(act_dtype), sds(F32), sds(F32)]
    if transposed:
        out_specs.append(col_spec)
        out_shape.append(sds_t(act_dtype))
    return pl.pallas_call(
        functools.partial(_inproj_kernel, dq_scale=dq_scale, transposed=transposed),
        grid=(T // tm, N_SEG),
        in_specs=[
            pl.BlockSpec((tm, D), lambda i, j: (i, 0)),
            pl.BlockSpec((1, D), lambda i, j: (0, 0)),
            pl.BlockSpec((D, SEG), lambda i, j: (0, j)),
            tab_spec, tab_spec, tab_spec, tab_spec, tab_spec,
        ],
        out_specs=out_specs,
        out_shape=out_shape,
        scratch_shapes=[pltpu.VMEM((tm, D), BF16), pltpu.VMEM((tm, SEG), F32)],
        compiler_params=_params(("parallel", "arbitrary")),
        name="in_projection",
    )(x2d, g.reshape(1, D), w_bf, cr, sr, cd, sdl, sdh)


def _ret_decay_tables(chunk):
    lg = jnp.log1p(-jnp.exp2(-5.0 - jnp.arange(RET_HEADS, dtype=F32)))
    idx = jnp.arange(chunk, dtype=F32)
    rel = idx[:, None] - idx[None, :]
    dmat = jnp.where(rel[None] >= 0, jnp.exp(jnp.maximum(rel, 0.0)[None] * lg[:, None, None]), 0.0)
    q_decay = jnp.exp((idx + 1.0)[:, None] * lg[None, :])
    k_decay = jnp.exp((chunk - 1.0 - idx)[:, None] * lg[None, :])
    chunk_decay = jnp.exp(chunk * lg)
    widen = lambda t: jnp.repeat(t, RET_DK, axis=1)
    return dmat, widen(q_decay), widen(k_decay), chunk_decay


def _ret_head(q, k, v, s, dmat, qd, kd, cd, gate):
    qb, kb, vb = q.astype(BF16), k.astype(BF16), v.astype(BF16)
    att = lax.dot_general(qb, kb, _NT, preferred_element_type=F32) * dmat
    o = (jnp.dot(att.astype(BF16), vb, preferred_element_type=F32)
         + jnp.dot(qb, s.astype(BF16), preferred_element_type=F32) * qd)
    kdk = (k.astype(F32) * kd).astype(BF16)
    s_new = s * cd + lax.dot_general(kdk, vb, _TN, preferred_element_type=F32)
    y = _unit_rms(o) * (gate * jax.nn.sigmoid(gate))
    return y, s_new


def _ret_prompt_kernel(cd_ref, q_ref, k_ref, v_ref, g_ref, dmat_ref, qd_ref, kd_ref, mix_ref, s_ref):
    @pl.when(pl.program_id(1) == 0)
    def _():
        s_ref[...] = jnp.zeros_like(s_ref)

    for h in range(RET_HEADS):
        sl = slice(h * RET_DK, (h + 1) * RET_DK)
        y, s_new = _ret_head(q_ref[:, sl], k_ref[:, sl], v_ref[:, sl], s_ref[0, h], dmat_ref[h],
                             qd_ref[:, sl], kd_ref[:, sl], cd_ref[h], g_ref[:, sl])
        s_ref[0, h] = s_new
        mix_ref[:, sl] = y.astype(mix_ref.dtype)


def _retention_prompt(rq, rk, rv, rg, batch, seq, chunk):
    n = seq // chunk
    dmat, qd, kd, cd = _ret_decay_tables(chunk)
    row = pl.BlockSpec((chunk, SEG), lambda b, c: (b * n + c, 0))
    full2 = pl.BlockSpec((chunk, SEG), lambda b, c: (0, 0))
    return pl.pallas_call(
        _ret_prompt_kernel,
        grid=(batch, n),
        in_specs=[
            pl.BlockSpec(memory_space=pltpu.SMEM),
            row, row, row, row,
            pl.BlockSpec((RET_HEADS, chunk, chunk), lambda b, c: (0, 0, 0)),
            full2, full2,
        ],
        out_specs=[row, pl.BlockSpec((1, RET_HEADS, RET_DK, RET_DV), lambda b, c: (b, 0, 0, 0))],
        out_shape=[jax.ShapeDtypeStruct((batch * seq, SEG), BF16),
                   jax.ShapeDtypeStruct((batch, RET_HEADS, RET_DK, RET_DV), F32)],
        compiler_params=_params(("parallel", "arbitrary")),
        name="retention_prompt",
    )(cd, rq, rk, rv, rg, dmat, qd, kd)


def _ret_sample_kernel(cd_ref, q_ref, k_ref, v_ref, g_ref, s0_ref, dmat_ref, qd_ref, kd_ref,
                       mix_ref, s_ref, *, group, seq):
    rows = []
    for b in range(group):
        r = slice(b * seq, (b + 1) * seq)
        heads = []
        for h in range(RET_HEADS):
            sl = slice(h * RET_DK, (h + 1) * RET_DK)
            y, s_new = _ret_head(q_ref[r, sl], k_ref[r, sl], v_ref[r, sl], s0_ref[b, h], dmat_ref[h],
                                 qd_ref[:, sl], kd_ref[:, sl], cd_ref[h], g_ref[r, sl])
            s_ref[b, h] = s_new
            heads.append(y)
        rows.append(jnp.concatenate(heads, axis=1))
    mix_ref[...] = jnp.concatenate(rows, axis=0)


def _retention_sample(rq, rk, rv, rg, state, layer, batch, seq, group):
    dmat, qd, kd, cd = _ret_decay_tables(seq)
    row = pl.BlockSpec((group * seq, SEG), lambda i: (i, 0))
    tab = pl.BlockSpec((seq, SEG), lambda i: (0, 0))
    st = pl.BlockSpec((group, RET_HEADS, RET_DK, RET_DV), lambda i: (i, 0, 0, 0))
    st_in = pl.BlockSpec((None, group, RET_HEADS, RET_DK, RET_DV), lambda i: (layer, i, 0, 0, 0))
    return pl.pallas_call(
        functools.partial(_ret_sample_kernel, group=group, seq=seq),
        grid=(batch // group,),
        in_specs=[
            pl.BlockSpec(memory_space=pltpu.SMEM),
            row, row, row, row, st_in,
            pl.BlockSpec((RET_HEADS, seq, seq), lambda i: (0, 0, 0)),
            tab, tab,
        ],
        out_specs=[row, st],
        out_shape=[jax.ShapeDtypeStruct((batch * seq, SEG), F32),
                   jax.ShapeDtypeStruct((batch, RET_HEADS, RET_DK, RET_DV), F32)],
        compiler_params=_params(("parallel",)),
        name="retention_sample",
    )(cd, rq, rk, rv, rg, state, dmat, qd, kd)


def _lambda_value(lam_ref, lam_init):
    a = jnp.sum(lam_ref[0:1, :] * lam_ref[1:2, :], axis=-1, keepdims=True)
    b = jnp.sum(lam_ref[2:3, :] * lam_ref[3:4, :], axis=-1, keepdims=True)
    return jnp.exp(a) - jnp.exp(b) + lam_init


def _softmax_update(s, m_old, l_old):
    m_new = jnp.maximum(m_old, jnp.max(s, axis=-1, keepdims=True))
    alpha = jnp.exp(m_old - m_new)
    p = jnp.exp(s - m_new)
    l_new = alpha * l_old + jnp.sum(p, axis=-1, keepdims=True)
    return p, alpha, m_new, l_new


def _diff_prompt_kernel(qi_tab, ki_tab, lam_ref, qt_ref, k_ref, vt_ref, g_ref, o_ref, m_scr, l_scr, acc_scr,
                        *, tq, tk, lam_init):
    t = pl.program_id(2)
    qi, ki = qi_tab[t], ki_tab[t]

    @pl.when(ki == 0)
    def _():
        m_scr[...] = jnp.full_like(m_scr, -jnp.inf)
        l_scr[...] = jnp.zeros_like(l_scr)
        acc_scr[...] = jnp.zeros_like(acc_scr)

    def step(masked):
        vt = vt_ref[...]
        if masked:
            kpos = ki * tk + lax.broadcasted_iota(jnp.int32, (tk, tq), 0)
            qpos = qi * tq + lax.broadcasted_iota(jnp.int32, (tk, tq), 1)
            keep = qpos >= kpos
        for m in range(2):
            sl = slice(m * DIFF_DK, (m + 1) * DIFF_DK)
            st = jnp.dot(k_ref[:, sl].astype(BF16), qt_ref[sl, :], preferred_element_type=F32)
            if masked:
                st = jnp.where(keep, st, -jnp.inf)
            m_old = m_scr[m]
            m_new = jnp.maximum(m_old, jnp.max(st, axis=0, keepdims=True))
            alpha = jnp.exp2(m_old - m_new)
            p = jnp.exp2(st - m_new)
            l_scr[m] = alpha * l_scr[m] + jnp.sum(p, axis=0, keepdims=True)
            m_scr[m] = m_new
            acc_scr[m] = alpha * acc_scr[m] + jnp.dot(vt, p.astype(BF16), preferred_element_type=F32)

    @pl.when(ki < qi)
    def _():
        step(False)

    @pl.when(ki == qi)
    def _():
        step(True)
        lam = _lambda_value(lam_ref, lam_init)
        ot = acc_scr[0] * (1.0 / l_scr[0]) - lam * (acc_scr[1] * (1.0 / l_scr[1]))
        yt = ot * lax.rsqrt(jnp.mean(ot * ot, axis=0, keepdims=True) + EPS)
        o_ref[...] = (yt.T * g_ref[...] * (1.0 - lam_init)).astype(o_ref.dtype)


def _diff_attention_prompt(dq_t, dk, dv_t, lam_rows, subln_g, batch, seq, lam_init, tq=512, tk=512):
    assert tq == tk
    nq = seq // tq
    pairs = [(qi, ki) for qi in range(nq) for ki in range(qi + 1)]
    qi_tab = jnp.asarray([p[0] for p in pairs], jnp.int32)
    ki_tab = jnp.asarray([p[1] for p in pairs], jnp.int32)
    hw = 2 * DIFF_DK
    grid_spec = pltpu.PrefetchScalarGridSpec(
        num_scalar_prefetch=2,
        grid=(batch, DIFF_HEADS, len(pairs)),
        in_specs=[
            pl.BlockSpec((4, DIFF_DK), lambda b, h, t, qt, kt: (0, 0)),
            pl.BlockSpec((hw, tq), lambda b, h, t, qt, kt: (h, b * nq + qt[t])),
            pl.BlockSpec((tk, hw), lambda b, h, t, qt, kt: (b * nq + kt[t], h)),
            pl.BlockSpec((DIFF_DV, tk), lambda b, h, t, qt, kt: (h, b * nq + kt[t])),
            pl.BlockSpec((1, DIFF_DV), lambda b, h, t, qt, kt: (0, 0)),
        ],
        out_specs=pl.BlockSpec((tq, DIFF_DV), lambda b, h, t, qt, kt: (b * nq + qt[t], h)),
        scratch_shapes=[pltpu.VMEM((2, 1, tq), F32), pltpu.VMEM((2, 1, tq), F32),
                        pltpu.VMEM((2, DIFF_DV, tq), F32)],
    )
    return pl.pallas_call(
        functools.partial(_diff_prompt_kernel, tq=tq, tk=tk, lam_init=lam_init),
        grid_spec=grid_spec,
        out_shape=jax.ShapeDtypeStruct((batch * seq, SEG), BF16),
        compiler_params=_params(("parallel", "parallel", "arbitrary")),
        name="diff_attention_prompt",
    )(qi_tab, ki_tab, lam_rows, dq_t, dk, dv_t, subln_g.reshape(1, DIFF_DV))


PAGES_PER_STEP = 8


def _head_match_bias(n_rows, seq, n_cols, heads):
    row_h = jnp.arange(n_rows, dtype=jnp.int32)[:, None] // seq
    col_h = jnp.arange(n_cols, dtype=jnp.int32)[None, :] % heads
    return jnp.where(row_h == col_h, 0.0, -jnp.inf).astype(F32)


def _stack_heads(x, width, offset, stride):
    return jnp.concatenate(
        [x[:, offset + h * stride: offset + h * stride + width] for h in range(DIFF_HEADS)], axis=0)


def _diff_sample_kernel(pt_ref, lam_ref, bias_ref, q_ref, kn_ref, vn_ref, g_ref, *rest, seq, lam_init):
    k_refs = rest[:PAGES_PER_STEP]
    v_refs = rest[PAGES_PER_STEP:3 * PAGES_PER_STEP]
    o_ref, m_scr, l_scr, acc_scr = rest[3 * PAGES_PER_STEP:]
    step = pl.program_id(1)
    rows = DIFF_HEADS * seq
    kv_rows = k_refs[0].shape[0] // 2

    @pl.when(step == 0)
    def _():
        m_scr[...] = jnp.full_like(m_scr, -jnp.inf)
        l_scr[...] = jnp.zeros_like(l_scr)
        acc_scr[...] = jnp.zeros_like(acc_scr)

    q = q_ref[...]
    qs = [_stack_heads(q, DIFF_DK, m * DIFF_DK, 2 * DIFF_DK).astype(BF16) for m in range(2)]

    def update(m, s):
        p, alpha, m_new, l_new = _softmax_update(s, m_scr[m], l_scr[m])
        m_scr[m] = m_new
        l_scr[m] = l_new
        acc_scr[m] = alpha * acc_scr[m]
        return p

    bias = bias_ref[...]
    ps = []
    for m in range(2):
        s = jnp.concatenate(
            [lax.dot_general(qs[m], kr[pl.ds(m, kv_rows, stride=2), :].astype(BF16), _NT,
                             preferred_element_type=F32) + bias for kr in k_refs], axis=1)
        ps.append(update(m, s))
    p = jnp.concatenate(ps, axis=0).astype(BF16)
    for e in range(2):
        pv = None
        for n in range(PAGES_PER_STEP):
            v = v_refs[2 * n + e][...].reshape(kv_rows, LANES).astype(BF16)
            t = jnp.dot(p[:, n * kv_rows:(n + 1) * kv_rows], v, preferred_element_type=F32)
            pv = t if pv is None else pv + t
        for m in range(2):
            acc_scr[m, :, e * LANES:(e + 1) * LANES] += pv[m * rows:(m + 1) * rows]

    @pl.when(step == pl.num_programs(1) - 1)
    def _():
        kn, vn = kn_ref[...], vn_ref[...]
        causal = (lax.broadcasted_iota(jnp.int32, (seq, seq), 0)
                  >= lax.broadcasted_iota(jnp.int32, (seq, seq), 1))
        for m in range(2):
            s = jnp.concatenate(
                [jnp.where(causal,
                           lax.dot_general(qs[m][h * seq:(h + 1) * seq],
                                           kn[:, (2 * h + m) * DIFF_DK:(2 * h + m + 1) * DIFF_DK].astype(BF16),
                                           _NT, preferred_element_type=F32),
                           -jnp.inf) for h in range(DIFF_HEADS)], axis=0)
            pn = update(m, s).astype(BF16)
            acc_scr[m] += jnp.concatenate(
                [jnp.dot(pn[h * seq:(h + 1) * seq], vn[:, h * DIFF_DV:(h + 1) * DIFF_DV].astype(BF16),
                         preferred_element_type=F32) for h in range(DIFF_HEADS)], axis=0)
        lam = _lambda_value(lam_ref, lam_init)
        o = acc_scr[0] / l_scr[0] - lam * (acc_scr[1] / l_scr[1])
        y = _unit_rms(o) * g_ref[...] * (1.0 - lam_init)
        o_ref[...] = jnp.concatenate([y[h * seq:(h + 1) * seq] for h in range(DIFF_HEADS)], axis=1)


def _diff_attention_sample(dq, dk_new, dv_new, cache_k, cache_v, layer, page_table, lam_rows, subln_g,
                           batch, seq, lam_init):
    depth, n_pool, page = cache_k.shape[:3]
    n_pages = page_table.shape[1]
    assert n_pages % PAGES_PER_STEP == 0 and DIFF_DV == 2 * LANES
    k_rows = page * DIFF_HEADS * 2
    ck = cache_k.reshape(depth, n_pool, k_rows, DIFF_DK)
    rows = DIFF_HEADS * seq
    bias = _head_match_bias(rows, seq, page * DIFF_HEADS, DIFF_HEADS)
    row = pl.BlockSpec((seq, SEG), lambda b, s, pt: (b, 0))

    def k_spec(n):
        return pl.BlockSpec((None, None, k_rows, DIFF_DK),
                            lambda b, s, pt: (layer, pt[b, s * PAGES_PER_STEP + n], 0, 0))

    def v_spec(n, e):
        return pl.BlockSpec((None, None, page, DIFF_HEADS, LANES),
                            lambda b, s, pt: (layer, pt[b, s * PAGES_PER_STEP + n], 0, 0, e))

    grid_spec = pltpu.PrefetchScalarGridSpec(
        num_scalar_prefetch=1,
        grid=(batch, n_pages // PAGES_PER_STEP),
        in_specs=[
            pl.BlockSpec((4, DIFF_DK), lambda b, s, pt: (0, 0)),
            pl.BlockSpec(bias.shape, lambda b, s, pt: (0, 0)),
            row, row, row,
            pl.BlockSpec((1, DIFF_DV), lambda b, s, pt: (0, 0)),
        ] + [k_spec(n) for n in range(PAGES_PER_STEP)]
          + [v_spec(n, e) for n in range(PAGES_PER_STEP) for e in range(2)],
        out_specs=row,
        scratch_shapes=[pltpu.VMEM((2, rows, 1), F32), pltpu.VMEM((2, rows, 1), F32),
                        pltpu.VMEM((2, rows, DIFF_DV), F32)],
    )
    return pl.pallas_call(
        functools.partial(_diff_sample_kernel, seq=seq, lam_init=lam_init),
        grid_spec=grid_spec,
        out_shape=jax.ShapeDtypeStruct((batch * seq, SEG), F32),
        compiler_params=_params(("parallel", "arbitrary")),
        name="diff_attention_sample",
    )(page_table, lam_rows, bias, dq, dk_new, dv_new, subln_g.reshape(1, DIFF_DV),
      *([ck] * PAGES_PER_STEP), *([cache_v] * (2 * PAGES_PER_STEP)))


def _norm_matmul_kernel(x_ref, g_ref, w_ref, o_ref, *, scale):
    h = _rms(x_ref[...], g_ref[...]).astype(BF16)
    z = jnp.dot(h, w_ref[...], preferred_element_type=F32)
    if scale != 1.0:
        z = z * scale
    o_ref[...] = z.astype(o_ref.dtype)


def _norm_matmul(x2d, g, w_bf, tm, out_dtype, scale=1.0):
    T, D = x2d.shape
    N = w_bf.shape[1]
    return pl.pallas_call(
        functools.partial(_norm_matmul_kernel, scale=scale),
        grid=(T // tm,),
        in_specs=[pl.BlockSpec((tm, D), lambda i: (i, 0)),
                  pl.BlockSpec((1, D), lambda i: (0, 0)),
                  pl.BlockSpec((D, N), lambda i: (0, 0))],
        out_specs=pl.BlockSpec((tm, N), lambda i: (i, 0)),
        out_shape=jax.ShapeDtypeStruct((T, N), out_dtype),
        compiler_params=_params(("parallel",)),
        name="norm_matmul",
    )(x2d, g.reshape(1, D), w_bf)


def _matmul_norm_res_kernel(*refs, n_in):
    a_refs = refs[:n_in]
    w_refs = refs[n_in:2 * n_in]
    g_ref, res_ref, o_ref = refs[2 * n_in:]
    z = None
    for a_ref, w_ref in zip(a_refs, w_refs):
        t = jnp.dot(a_ref[...].astype(BF16), w_ref[...], preferred_element_type=F32)
        z = t if z is None else z + t
    o_ref[...] = res_ref[...] + _rms(z, g_ref[...])


def _matmul_norm_residual(acts, ws_bf, g, res, tm):
    T, D = res.shape
    n_in = len(acts)
    in_specs = ([pl.BlockSpec((tm, a.shape[1]), lambda i: (i, 0)) for a in acts]
                + [pl.BlockSpec(w.shape, lambda i: (0, 0)) for w in ws_bf]
                + [pl.BlockSpec((1, D), lambda i: (0, 0)), pl.BlockSpec((tm, D), lambda i: (i, 0))])
    return pl.pallas_call(
        functools.partial(_matmul_norm_res_kernel, n_in=n_in),
        grid=(T // tm,),
        in_specs=in_specs,
        out_specs=pl.BlockSpec((tm, D), lambda i: (i, 0)),
        out_shape=jax.ShapeDtypeStruct((T, D), F32),
        compiler_params=_params(("parallel",)),
        name="matmul_norm_residual",
    )(*acts, *ws_bf, g.reshape(1, D), res)


def _softmax_rows(s):
    e = jnp.exp(s - jnp.max(s, axis=-1, keepdims=True))
    return e / jnp.sum(e, axis=-1, keepdims=True)


def _cross_prompt_kernel(q_ref, mk_ref, mv_ref, o_ref):
    for h in range(MEM_HEADS):
        sl = slice(h * MEM_DH, (h + 1) * MEM_DH)
        s = lax.dot_general(q_ref[:, sl], mk_ref[0, :, sl], _NT, preferred_element_type=F32)
        p = _softmax_rows(s).astype(BF16)
        o_ref[:, sl] = jnp.dot(p, mv_ref[0, :, sl], preferred_element_type=F32).astype(o_ref.dtype)


def _cross_attention_prompt(q, mk_bf, mv_bf, batch, seq, tq):
    nq = seq // tq
    W = MEM_HEADS * MEM_DH
    M = mk_bf.shape[1]
    kv = pl.BlockSpec((1, M, W), lambda b, i: (b, 0, 0))
    row = pl.BlockSpec((tq, W), lambda b, i: (b * nq + i, 0))
    return pl.pallas_call(
        _cross_prompt_kernel,
        grid=(batch, nq),
        in_specs=[row, kv, kv],
        out_specs=row,
        out_shape=jax.ShapeDtypeStruct((batch * seq, W), BF16),
        compiler_params=_params(("parallel", "parallel")),
        name="cross_attention_prompt",
    )(q, mk_bf, mv_bf)


def _cross_sample_kernel(bias_ref, q_ref, mk_ref, mv_ref, o_ref, *, group, seq):
    bias = bias_ref[...]
    outs = []
    for b in range(group):
        qb = q_ref[b * seq:(b + 1) * seq, :]
        qh = jnp.concatenate([qb[:, h * MEM_DH:(h + 1) * MEM_DH] for h in range(MEM_HEADS)], axis=0)
        s = lax.dot_general(qh.astype(BF16), mk_ref[b].astype(BF16), _NT, preferred_element_type=F32) + bias
        p = _softmax_rows(s).astype(BF16)
        o = jnp.dot(p, mv_ref[b].astype(BF16), preferred_element_type=F32)
        outs.append(jnp.concatenate([o[h * seq:(h + 1) * seq] for h in range(MEM_HEADS)], axis=1))
    o_ref[...] = jnp.concatenate(outs, axis=0)


def _cross_attention_sample(q, mem_k, mem_v, layer, batch, seq, group):
    depth, _, M = mem_k.shape[:3]
    W = MEM_HEADS * MEM_DH
    mk = mem_k.reshape(depth, batch, M * MEM_HEADS, MEM_DH)
    mv = mem_v.reshape(depth, batch, M * MEM_HEADS, MEM_DH)
    bias = _head_match_bias(MEM_HEADS * seq, seq, M * MEM_HEADS, MEM_HEADS)
    kv = pl.BlockSpec((None, group, M * MEM_HEADS, MEM_DH), lambda i: (layer, i, 0, 0))
    row = pl.BlockSpec((group * seq, W), lambda i: (i, 0))
    return pl.pallas_call(
        functools.partial(_cross_sample_kernel, group=group, seq=seq),
        grid=(batch // group,),
        in_specs=[pl.BlockSpec(bias.shape, lambda i: (0, 0)), row, kv, kv],
        out_specs=row,
        out_shape=jax.ShapeDtypeStruct((batch * seq, W), F32),
        compiler_params=_params(("parallel",)),
        name="cross_attention_sample",
    )(bias, q, mk, mv)


def _mlp_kernel(x_ref, gpre_ref, wup_ref, wdn_ref, gpost_ref, o_ref, h_scr):
    j = pl.program_id(1)

    @pl.when(j == 0)
    def _():
        h_scr[...] = _rms(x_ref[...], gpre_ref[...]).astype(BF16)

        o_ref[...] = jnp.zeros_like(o_ref)

    u = jnp.maximum(jnp.dot(h_scr[...], wup_ref[...], preferred_element_type=F32), 0.0)
    o_ref[...] += jnp.dot((u * u).astype(BF16), wdn_ref[...], preferred_element_type=F32)

    @pl.when(j == pl.num_programs(1) - 1)
    def _():
        o_ref[...] = x_ref[...] + _rms(o_ref[...], gpost_ref[...])


def _mlp(x2d, g_pre, w_up_bf, w_dn_bf, g_post, tm, tf):
    T, D = x2d.shape
    FF = w_up_bf.shape[1]
    vec = pl.BlockSpec((1, D), lambda i, j: (0, 0))
    return pl.pallas_call(
        _mlp_kernel,
        grid=(T // tm, FF // tf),
        in_specs=[pl.BlockSpec((tm, D), lambda i, j: (i, 0)), vec,
                  pl.BlockSpec((D, tf), lambda i, j: (0, j)),
                  pl.BlockSpec((tf, D), lambda i, j: (j, 0)), vec],
        out_specs=pl.BlockSpec((tm, D), lambda i, j: (i, 0)),
        out_shape=jax.ShapeDtypeStruct((T, D), F32),
        scratch_shapes=[pltpu.VMEM((tm, D), BF16)],
        compiler_params=_params(("parallel", "arbitrary")),
        name="mlp",
    )(x2d, g_pre.reshape(1, D), w_up_bf, w_dn_bf, g_post.reshape(1, D))


def _position_tables(pos, tm):
    reps = max(1, tm // pos.shape[0])
    pos = jnp.tile(pos, reps)
    cr, srl, srh = _rope_tables(pos, RET_DK, RET_THETA)
    cd, sdl, sdh = _rope_tables(pos, ROPE_DIM, ROPE_THETA)
    return (cr, srl + srh, cd, sdl, sdh), pos.shape[0] // tm


def _layer_tail(x2d, mix_parts, wts, cross_fn, tm):
    (w_out_parts, g_mix_post, g_mem_pre, w_mem_q, w_mem_o, g_mem_post,
     g_mlp_pre, w_up, w_down, g_mlp_post) = wts
    x1 = _matmul_norm_residual(mix_parts, w_out_parts, g_mix_post, x2d, tm)
    q = cross_fn(x1, g_mem_pre, w_mem_q)
    x2 = _matmul_norm_residual([q], [w_mem_o], g_mem_post, x1, tm)
    return _mlp(x2, g_mlp_pre, w_up, w_down, g_mlp_post, tm, 512)


def kernel(x_prompt, x_sample, mem_prompt, state_ret, cache_diff_k, cache_diff_v, cache_mem_k, cache_mem_v, page_table, w_in, w_out, diff_lambda_q1, diff_lambda_k1, diff_lambda_q2, diff_lambda_k2, diff_subln_g, norm_mix_pre, norm_mix_post, norm_mem_pre, norm_mem_post, norm_mlp_pre, norm_mlp_post, mem_norm_g, w_mem_q, w_mem_k, w_mem_v, w_mem_o, w_mlp_up, w_mlp_down):
    depth = w_in.shape[0]
    B, L_p, D = x_prompt.shape
    B_s, L_s, _ = x_sample.shape
    n_pages, page = page_table.shape[1], cache_diff_k.shape[2]
    past_len = n_pages * page
    M = mem_prompt.shape[1]
    W_MEM = MEM_HEADS * MEM_DH
    TM = 512
    ret_chunk_p = math.gcd(L_p, RET_CHUNK)

    tabs_p, per_p = _position_tables(jnp.arange(L_p, dtype=F32), TM)
    TM_IN_S = 256
    tabs_s, per_s = _position_tables(past_len + jnp.arange(L_s, dtype=F32), TM_IN_S)

    yp = x_prompt.reshape(B * L_p, D)
    ys = x_sample.reshape(B_s * L_s, D)
    mem2d = mem_prompt.reshape(B * M, D)
    outs = {k: [] for k in ("rp", "kp", "vp", "mkp", "mvp", "rs", "ks", "vs")}

    for i in range(depth):
        lam_init = 0.8 - 0.6 * math.exp(-0.3 * i)
        lam_rows = jnp.stack([diff_lambda_q1[i], diff_lambda_k1[i], diff_lambda_q2[i], diff_lambda_k2[i]])
        w_in_bf = w_in[i].astype(BF16)
        half = w_out.shape[1] // 2
        tail_w = ([w_out[i, :half].astype(BF16), w_out[i, half:].astype(BF16)], norm_mix_post[i],
                  norm_mem_pre[i], w_mem_q[i].astype(BF16), w_mem_o[i].astype(BF16), norm_mem_post[i],
                  norm_mlp_pre[i], w_mlp_up[i].astype(BF16), w_mlp_down[i].astype(BF16), norm_mlp_post[i])

        mk_p = _norm_matmul(mem2d, mem_norm_g[i], w_mem_k[i].astype(BF16), B * M, F32)
        mv_p = _norm_matmul(mem2d, mem_norm_g[i], w_mem_v[i].astype(BF16), B * M, F32)
        rq, rk, rv, rg, dq_t, dk, dv, dv_t = _in_projection(
            yp, norm_mix_pre[i], w_in_bf, tabs_p, per_p, TM, BF16, DIFF_DK ** -0.5 * LOG2E, True)
        mix_ret, s_p = _retention_prompt(rq, rk, rv, rg, B, L_p, ret_chunk_p)
        mix_diff = _diff_attention_prompt(dq_t, dk, dv_t, lam_rows, diff_subln_g[i], B, L_p, lam_init)
        mk_bf = mk_p.astype(BF16).reshape(B, M, W_MEM)
        mv_bf = mv_p.astype(BF16).reshape(B, M, W_MEM)

        def cross_p(x1, g, wq):
            q = _norm_matmul(x1, g, wq, TM, BF16, scale=MEM_DH ** -0.5)
            return _cross_attention_prompt(q, mk_bf, mv_bf, B, L_p, TM)

        yp = _layer_tail(yp, [mix_ret, mix_diff], tail_w, cross_p, TM)
        outs["rp"].append(s_p.astype(state_ret.dtype))
        outs["kp"].append(dk.reshape(B, L_p, DIFF_HEADS, 2, DIFF_DK))
        outs["vp"].append(dv.reshape(B, L_p, DIFF_HEADS, DIFF_DV))
        outs["mkp"].append(mk_p.reshape(B, M, MEM_HEADS, MEM_DH))
        outs["mvp"].append(mv_p.reshape(B, M, MEM_HEADS, MEM_DH))

        rq, rk, rv, rg, dq, dk, dv = _in_projection(ys, norm_mix_pre[i], w_in_bf, tabs_s, per_s, TM_IN_S, F32,
                                                    DIFF_DK ** -0.5, False)
        mix_ret, s_s = _retention_sample(rq, rk, rv, rg, state_ret, i, B_s, L_s, 4)
        mix_diff = _diff_attention_sample(dq, dk, dv, cache_diff_k, cache_diff_v, i, page_table,
                                          lam_rows, diff_subln_g[i], B_s, L_s, lam_init)
        def cross_s(x1, g, wq):
            q = _norm_matmul(x1, g, wq, TM, F32, scale=MEM_DH ** -0.5)
            return _cross_attention_sample(q, cache_mem_k, cache_mem_v, i, B_s, L_s, 8)

        ys = _layer_tail(ys, [mix_ret, mix_diff], tail_w, cross_s, TM)
        outs["rs"].append(s_s.astype(state_ret.dtype))
        outs["ks"].append(dk.reshape(B_s, L_s, DIFF_HEADS, 2, DIFF_DK))
        outs["vs"].append(dv.reshape(B_s, L_s, DIFF_HEADS, DIFF_DV))

    st = lambda k: outs[k][0][None] if depth == 1 else jnp.stack(outs[k])
    return (yp.reshape(B, L_p, D), ys.reshape(B_s, L_s, D), st("rp"), st("kp"), st("vp"),
            st("mkp"), st("mvp"), st("rs"), st("ks"), st("vs"))
```

```python
import functools
import math

import jax
import jax.numpy as jnp
from jax import lax
from jax.experimental import pallas as pl
from jax.experimental.pallas import tpu as pltpu

F32 = jnp.float32
BF16 = jnp.bfloat16

LANES = 128
MXU_COLS = 256
RET_HEADS = 8
RET_DK = 128
RET_DV = 128
RET_THETA = 10000.0
RET_CHUNK = 128
DIFF_HEADS = 4
DIFF_DK = 128
DIFF_DV = 256
ROPE_THETA = 500000.0
ROPE_DIM = DIFF_DK // 4
MEM_HEADS = 4
MEM_DH = 128
Q_BLOCK = 128
EPS = 1e-6
LOG2E = math.log2(math.e)
SEG = 1024
N_SEG = 7

VMEM_LIMIT = 56 * 1024 * 1024

_NT = (((1,), (1,)), ((), ()))
_TN = (((0,), (0,)), ((), ()))


def _params(sem, vmem=VMEM_LIMIT):
    return pltpu.CompilerParams(dimension_semantics=sem, vmem_limit_bytes=vmem)


def _rms(x, g):
    return x * lax.rsqrt(jnp.mean(x * x, axis=-1, keepdims=True) + EPS) * g


def _unit_rms(o):
    return o * lax.rsqrt(jnp.mean(o * o, axis=-1, keepdims=True) + EPS)


def _rope_tables(pos, rot_dim, theta):
    half = rot_dim // 2
    inv = jnp.exp(-math.log(theta) * (2.0 * jnp.arange(half, dtype=F32) / rot_dim))
    ang = pos[:, None] * inv[None, :]
    cos, sin = jnp.cos(ang), jnp.sin(ang)
    n = pos.shape[0]
    rest = LANES - rot_dim
    c = jnp.concatenate([cos, cos, jnp.ones((n, rest), F32)], axis=-1)
    s_lo = jnp.concatenate([-sin, jnp.zeros((n, LANES - half), F32)], axis=-1)
    s_hi = jnp.concatenate([jnp.zeros((n, half), F32), sin, jnp.zeros((n, rest), F32)], axis=-1)
    return c, s_lo, s_hi


def _inproj_kernel(x_ref, g_ref, w_ref, cr_ref, sr_ref, cd_ref, sdl_ref, sdh_ref, *rest,
                   dq_scale, transposed):
    if transposed:
        rq_ref, rk_ref, rv_ref, rg_ref, dq_ref, dk_ref, dv_ref, dvt_ref, h_scr = rest
    else:
        rq_ref, rk_ref, rv_ref, rg_ref, dq_ref, dk_ref, dv_ref, h_scr = rest
    j = pl.program_id(1)
    heads = [slice(h * LANES, (h + 1) * LANES) for h in range(SEG // LANES)]

    @pl.when(j == 0)
    def _():
        h_scr[...] = _rms(x_ref[...], g_ref[...]).astype(BF16)

    def project():
        return jnp.dot(h_scr[...], w_ref[...], preferred_element_type=F32)

    def ret_rot(out_ref, scale):
        z_all = project()
        c, s = cr_ref[...], sr_ref[...]
        for sl in heads:
            z = z_all[:, sl]
            r = z * c + pltpu.roll(z, RET_DK // 2, 1) * s
            if scale != 1.0:
                r = r * scale
            out_ref[:, sl] = r.astype(out_ref.dtype)

    def diff_rot(z):
        half = ROPE_DIM // 2
        return (z * cd_ref[...] + pltpu.roll(z, LANES - half, 1) * sdl_ref[...]
                + pltpu.roll(z, half, 1) * sdh_ref[...])

    @pl.when(j == 0)
    def _():
        ret_rot(rq_ref, 1.0)

    @pl.when(j == 1)
    def _():
        ret_rot(rk_ref, RET_DK ** -0.5)

    @pl.when(j == 2)
    def _():
        rv_ref[...] = project().astype(rv_ref.dtype)

    @pl.when(j == 3)
    def _():
        rg_ref[...] = project()

    @pl.when(j == 4)
    def _():
        z_all = project()
        for sl in heads:
            r = diff_rot(z_all[:, sl]) * dq_scale
            if transposed:
                dq_ref[sl, :] = r.T.astype(dq_ref.dtype)
            else:
                dq_ref[:, sl] = r.astype(dq_ref.dtype)

    @pl.when(j == 5)
    def _():
        z_all = project()
        for sl in heads:
            dk_ref[:, sl] = diff_rot(z_all[:, sl])

    @pl.when(j == 6)
    def _():
        z_all = project()
        dv_ref[...] = z_all
        if transposed:
            for sl in heads:
                dvt_ref[sl, :] = z_all[:, sl].T.astype(dvt_ref.dtype)


def _in_projection(x2d, g, w_bf, tabs, period_blocks, tm, act_dtype, dq_scale, transposed):
    T, D = x2d.shape
    cr, sr, cd, sdl, sdh = tabs
    tab_spec = pl.BlockSpec((tm, LANES), lambda i, j: (i % period_blocks, 0))
    row_spec = pl.BlockSpec((tm, SEG), lambda i, j: (i, 0))
    col_spec = pl.BlockSpec((SEG, tm), lambda i, j: (0, i))
    sds = lambda dt: jax.ShapeDtypeStruct((T, SEG), dt)
    sds_t = lambda dt: jax.ShapeDtypeStruct((SEG, T), dt)
    out_specs = [row_spec] * 4 + [col_spec if transposed else row_spec, row_spec, row_spec]
    out_shape = [sds(act_dtype), sds(act_dtype), sds(act_dtype), sds(F32),
                 sds_t(act_dtype) if transposed else sds(act_dtype), sds(F32), sds(F32)]
    if transposed:
        out_specs.append(col_spec)
        out_shape.append(sds_t(act_dtype))
    return pl.pallas_call(
        functools.partial(_inproj_kernel, dq_scale=dq_scale, transposed=transposed),
        grid=(T // tm, N_SEG),
        in_specs=[
            pl.BlockSpec((tm, D), lambda i, j: (i, 0)),
            pl.BlockSpec((1, D), lambda i, j: (0, 0)),
            pl.BlockSpec((D, SEG), lambda i, j: (0, j)),
            tab_spec, tab_spec, tab_spec, tab_spec, tab_spec,
        ],
        out_specs=out_specs,
        out_shape=out_shape,
        scratch_shapes=[pltpu.VMEM((tm, D), BF16)],
        compiler_params=_params(("parallel", "arbitrary")),
        name="in_projection",
    )(x2d, g.reshape(1, D), w_bf, cr, sr, cd, sdl, sdh)


def _ret_decay_tables(chunk):
    lg = jnp.log1p(-jnp.exp2(-5.0 - jnp.arange(RET_HEADS, dtype=F32)))
    idx = jnp.arange(chunk, dtype=F32)
    rel = idx[:, None] - idx[None, :]
    dmat = jnp.where(rel[None] >= 0, jnp.exp(jnp.maximum(rel, 0.0)[None] * lg[:, None, None]), 0.0)
    q_decay = jnp.exp((idx + 1.0)[:, None] * lg[None, :])
    k_decay = jnp.exp((chunk - 1.0 - idx)[:, None] * lg[None, :])
    chunk_decay = jnp.exp(chunk * lg)
    widen = lambda t: jnp.repeat(t, RET_DK, axis=1)
    return dmat, widen(q_decay), widen(k_decay), chunk_decay


def _ret_head(q, k, v, s, dmat, qd, kd, cd, gate):
    qb, kb, vb = q.astype(BF16), k.astype(BF16), v.astype(BF16)
    att = lax.dot_general(qb, kb, _NT, preferred_element_type=F32) * dmat
    o = (jnp.dot(att.astype(BF16), vb, preferred_element_type=F32)
         + jnp.dot(qb, s.astype(BF16), preferred_element_type=F32) * qd)
    kdk = (k.astype(F32) * kd).astype(BF16)
    s_new = s * cd + lax.dot_general(kdk, vb, _TN, preferred_element_type=F32)
    y = _unit_rms(o) * (gate * jax.nn.sigmoid(gate))
    return y, s_new


def _ret_prompt_kernel(cd_ref, q_ref, k_ref, v_ref, g_ref, dmat_ref, qd_ref, kd_ref, mix_ref, s_ref):
    @pl.when(pl.program_id(1) == 0)
    def _():
        s_ref[...] = jnp.zeros_like(s_ref)

    for h in range(RET_HEADS):
        sl = slice(h * RET_DK, (h + 1) * RET_DK)
        y, s_new = _ret_head(q_ref[:, sl], k_ref[:, sl], v_ref[:, sl], s_ref[0, h], dmat_ref[h],
                             qd_ref[:, sl], kd_ref[:, sl], cd_ref[h], g_ref[:, sl])
        s_ref[0, h] = s_new
        mix_ref[:, sl] = y.astype(mix_ref.dtype)


def _retention_prompt(rq, rk, rv, rg, batch, seq, chunk):
    n = seq // chunk
    dmat, qd, kd, cd = _ret_decay_tables(chunk)
    row = pl.BlockSpec((chunk, SEG), lambda b, c: (b * n + c, 0))
    full2 = pl.BlockSpec((chunk, SEG), lambda b, c: (0, 0))
    return pl.pallas_call(
        _ret_prompt_kernel,
        grid=(batch, n),
        in_specs=[
            pl.BlockSpec(memory_space=pltpu.SMEM),
            row, row, row, row,
            pl.BlockSpec((RET_HEADS, chunk, chunk), lambda b, c: (0, 0, 0)),
            full2, full2,
        ],
        out_specs=[row, pl.BlockSpec((1, RET_HEADS, RET_DK, RET_DV), lambda b, c: (b, 0, 0, 0))],
        out_shape=[jax.ShapeDtypeStruct((batch * seq, SEG), BF16),
                   jax.ShapeDtypeStruct((batch, RET_HEADS, RET_DK, RET_DV), F32)],
        compiler_params=_params(("parallel", "arbitrary")),
        name="retention_prompt",
    )(cd, rq, rk, rv, rg, dmat, qd, kd)


def _ret_sample_kernel(cd_ref, q_ref, k_ref, v_ref, g_ref, s0_ref, dmat_ref, qd_ref, kd_ref,
                       mix_ref, s_ref, *, group, seq):
    rows = []
    for b in range(group):
        r = slice(b * seq, (b + 1) * seq)
        heads = []
        for h in range(RET_HEADS):
            sl = slice(h * RET_DK, (h + 1) * RET_DK)
            y, s_new = _ret_head(q_ref[r, sl], k_ref[r, sl], v_ref[r, sl], s0_ref[b, h], dmat_ref[h],
                                 qd_ref[:, sl], kd_ref[:, sl], cd_ref[h], g_ref[r, sl])
            s_ref[b, h] = s_new
            heads.append(y)
        rows.append(jnp.concatenate(heads, axis=1))
    mix_ref[...] = jnp.concatenate(rows, axis=0)


def _retention_sample(rq, rk, rv, rg, state, layer, batch, seq, group):
    dmat, qd, kd, cd = _ret_decay_tables(seq)
    row = pl.BlockSpec((group * seq, SEG), lambda i: (i, 0))
    tab = pl.BlockSpec((seq, SEG), lambda i: (0, 0))
    st = pl.BlockSpec((group, RET_HEADS, RET_DK, RET_DV), lambda i: (i, 0, 0, 0))
    st_in = pl.BlockSpec((None, group, RET_HEADS, RET_DK, RET_DV), lambda i: (layer, i, 0, 0, 0))
    return pl.pallas_call(
        functools.partial(_ret_sample_kernel, group=group, seq=seq),
        grid=(batch // group,),
        in_specs=[
            pl.BlockSpec(memory_space=pltpu.SMEM),
            row, row, row, row, st_in,
            pl.BlockSpec((RET_HEADS, seq, seq), lambda i: (0, 0, 0)),
            tab, tab,
        ],
        out_specs=[row, st],
        out_shape=[jax.ShapeDtypeStruct((batch * seq, SEG), F32),
                   jax.ShapeDtypeStruct((batch, RET_HEADS, RET_DK, RET_DV), F32)],
        compiler_params=_params(("parallel",)),
        name="retention_sample",
    )(cd, rq, rk, rv, rg, state, dmat, qd, kd)


def _lambda_value(lam_ref, lam_init):
    a = jnp.sum(lam_ref[0:1, :] * lam_ref[1:2, :], axis=-1, keepdims=True)
    b = jnp.sum(lam_ref[2:3, :] * lam_ref[3:4, :], axis=-1, keepdims=True)
    return jnp.exp(a) - jnp.exp(b) + lam_init


def _softmax_update(s, m_old, l_old):
    m_new = jnp.maximum(m_old, jnp.max(s, axis=-1, keepdims=True))
    alpha = jnp.exp(m_old - m_new)
    p = jnp.exp(s - m_new)
    l_new = alpha * l_old + jnp.sum(p, axis=-1, keepdims=True)
    return p, alpha, m_new, l_new


def _diff_prompt_kernel(qi_tab, ki_tab, lam_ref, qt_ref, k_ref, vt_ref, g_ref, o_ref, m_scr, l_scr, acc_scr,
                        *, tq, tk, lam_init):
    t = pl.program_id(1)
    qi, ki = qi_tab[t], ki_tab[t]

    @pl.when(ki == 0)
    def _():
        m_scr[...] = jnp.full_like(m_scr, -jnp.inf)
        l_scr[...] = jnp.zeros_like(l_scr)
        acc_scr[...] = jnp.zeros_like(acc_scr)

    def step(masked):
        if masked:
            kpos = ki * tk + lax.broadcasted_iota(jnp.int32, (tk, tq), 0)
            qpos = qi * tq + lax.broadcasted_iota(jnp.int32, (tk, tq), 1)
            keep = qpos >= kpos
        for h in range(DIFF_HEADS):
            vt = vt_ref[h * DIFF_DV:(h + 1) * DIFF_DV, :]
            for m in range(2):
                i = 2 * h + m
                sl = slice(i * DIFF_DK, (i + 1) * DIFF_DK)
                kb = k_ref[:, sl].astype(BF16)
                for c in range(tq // MXU_COLS):
                    cs = slice(c * MXU_COLS, (c + 1) * MXU_COLS)
                    st = jnp.dot(kb, qt_ref[sl, cs], preferred_element_type=F32)
                    if masked:
                        st = jnp.where(keep[:, cs], st, -jnp.inf)
                    m_old = m_scr[i, :, cs]
                    m_new = jnp.maximum(m_old, jnp.max(st, axis=0, keepdims=True))
                    alpha = jnp.exp2(m_old - m_new)
                    p = jnp.exp2(st - m_new)
                    l_scr[i, :, cs] = alpha * l_scr[i, :, cs] + jnp.sum(p, axis=0, keepdims=True)
                    m_scr[i, :, cs] = m_new
                    acc_scr[i, :, cs] = (alpha * acc_scr[i, :, cs]
                                         + jnp.dot(vt, p.astype(BF16), preferred_element_type=F32))

    @pl.when(ki < qi)
    def _():
        step(False)

    @pl.when(ki == qi)
    def _():
        step(True)
        lam = _lambda_value(lam_ref, lam_init)
        for h in range(DIFF_HEADS):
            a, b = 2 * h, 2 * h + 1
            ot = acc_scr[a] * (1.0 / l_scr[a]) - lam * (acc_scr[b] * (1.0 / l_scr[b]))
            yt = ot * lax.rsqrt(jnp.mean(ot * ot, axis=0, keepdims=True) + EPS)
            o_ref[:, h * DIFF_DV:(h + 1) * DIFF_DV] = (yt.T * g_ref[...] * (1.0 - lam_init)).astype(o_ref.dtype)


def _diff_attention_prompt(dq_t, dk, dv_t, lam_rows, subln_g, batch, seq, lam_init, tq=512, tk=512):
    assert tq == tk
    nq = seq // tq
    pairs = [(qi, ki) for qi in range(nq) for ki in range(qi + 1)]
    qi_tab = jnp.asarray([p[0] for p in pairs], jnp.int32)
    ki_tab = jnp.asarray([p[1] for p in pairs], jnp.int32)
    n_sub = 2 * DIFF_HEADS
    grid_spec = pltpu.PrefetchScalarGridSpec(
        num_scalar_prefetch=2,
        grid=(batch, len(pairs)),
        in_specs=[
            pl.BlockSpec((4, DIFF_DK), lambda b, t, qt, kt: (0, 0)),
            pl.BlockSpec((SEG, tq), lambda b, t, qt, kt: (0, b * nq + qt[t])),
            pl.BlockSpec((tk, SEG), lambda b, t, qt, kt: (b * nq + kt[t], 0)),
            pl.BlockSpec((SEG, tk), lambda b, t, qt, kt: (0, b * nq + kt[t])),
            pl.BlockSpec((1, DIFF_DV), lambda b, t, qt, kt: (0, 0)),
        ],
        out_specs=pl.BlockSpec((tq, SEG), lambda b, t, qt, kt: (b * nq + qt[t], 0)),
        scratch_shapes=[pltpu.VMEM((n_sub, 1, tq), F32), pltpu.VMEM((n_sub, 1, tq), F32),
                        pltpu.VMEM((n_sub, DIFF_DV, tq), F32)],
    )
    return pl.pallas_call(
        functools.partial(_diff_prompt_kernel, tq=tq, tk=tk, lam_init=lam_init),
        grid_spec=grid_spec,
        out_shape=jax.ShapeDtypeStruct((batch * seq, SEG), BF16),
        compiler_params=_params(("parallel", "arbitrary")),
        name="diff_attention_prompt",
    )(qi_tab, ki_tab, lam_rows, dq_t, dk, dv_t, subln_g.reshape(1, DIFF_DV))


PAGES_PER_STEP = 8


def _head_match_bias(n_rows, seq, n_cols, heads):
    row_h = jnp.arange(n_rows, dtype=jnp.int32)[:, None] // seq
    col_h = jnp.arange(n_cols, dtype=jnp.int32)[None, :] % heads
    return jnp.where(row_h == col_h, 0.0, -jnp.inf).astype(F32)


def _stack_heads(x, width, offset, stride):
    return jnp.concatenate(
        [x[:, offset + h * stride: offset + h * stride + width] for h in range(DIFF_HEADS)], axis=0)


def _diff_sample_kernel(pt_ref, lam_ref, bias_ref, q_ref, kn_ref, vn_ref, g_ref, *rest, seq, lam_init):
    k_refs = rest[:PAGES_PER_STEP]
    v_refs = rest[PAGES_PER_STEP:3 * PAGES_PER_STEP]
    o_ref, m_scr, l_scr, acc_scr = rest[3 * PAGES_PER_STEP:]
    step = pl.program_id(1)
    rows = DIFF_HEADS * seq
    kv_rows = k_refs[0].shape[0] // 2

    @pl.when(step == 0)
    def _():
        m_scr[...] = jnp.full_like(m_scr, -jnp.inf)
        l_scr[...] = jnp.zeros_like(l_scr)
        acc_scr[...] = jnp.zeros_like(acc_scr)

    q = q_ref[...]
    qs = [_stack_heads(q, DIFF_DK, m * DIFF_DK, 2 * DIFF_DK).astype(BF16) for m in range(2)]

    def update(m, s):
        p, alpha, m_new, l_new = _softmax_update(s, m_scr[m], l_scr[m])
        m_scr[m] = m_new
        l_scr[m] = l_new
        acc_scr[m] = alpha * acc_scr[m]
        return p

    bias = bias_ref[...]
    ps = []
    for m in range(2):
        s = jnp.concatenate(
            [lax.dot_general(qs[m], kr[pl.ds(m, kv_rows, stride=2), :].astype(BF16), _NT,
                             preferred_element_type=F32) + bias for kr in k_refs], axis=1)
        ps.append(update(m, s))
    p = jnp.concatenate(ps, axis=0).astype(BF16)
    for e in range(2):
        pv = None
        for n in range(PAGES_PER_STEP):
            v = v_refs[2 * n + e][...].reshape(kv_rows, LANES).astype(BF16)
            t = jnp.dot(p[:, n * kv_rows:(n + 1) * kv_rows], v, preferred_element_type=F32)
            pv = t if pv is None else pv + t
        for m in range(2):
            acc_scr[m, :, e * LANES:(e + 1) * LANES] += pv[m * rows:(m + 1) * rows]

    @pl.when(step == pl.num_programs(1) - 1)
    def _():
        kn, vn = kn_ref[...], vn_ref[...]
        causal = (lax.broadcasted_iota(jnp.int32, (seq, seq), 0)
                  >= lax.broadcasted_iota(jnp.int32, (seq, seq), 1))
        for m in range(2):
            s = jnp.concatenate(
                [jnp.where(causal,
                           lax.dot_general(qs[m][h * seq:(h + 1) * seq],
                                           kn[:, (2 * h + m) * DIFF_DK:(2 * h + m + 1) * DIFF_DK].astype(BF16),
                                           _NT, preferred_element_type=F32),
                           -jnp.inf) for h in range(DIFF_HEADS)], axis=0)
            pn = update(m, s).astype(BF16)
            acc_scr[m] += jnp.concatenate(
                [jnp.dot(pn[h * seq:(h + 1) * seq], vn[:, h * DIFF_DV:(h + 1) * DIFF_DV].astype(BF16),
                         preferred_element_type=F32) for h in range(DIFF_HEADS)], axis=0)
        lam = _lambda_value(lam_ref, lam_init)
        o = acc_scr[0] / l_scr[0] - lam * (acc_scr[1] / l_scr[1])
        y = _unit_rms(o) * g_ref[...] * (1.0 - lam_init)
        o_ref[...] = jnp.concatenate([y[h * seq:(h + 1) * seq] for h in range(DIFF_HEADS)], axis=1)


def _diff_attention_sample(dq, dk_new, dv_new, cache_k, cache_v, layer, page_table, lam_rows, subln_g,
                           batch, seq, lam_init):
    depth, n_pool, page = cache_k.shape[:3]
    n_pages = page_table.shape[1]
    assert n_pages % PAGES_PER_STEP == 0 and DIFF_DV == 2 * LANES
    k_rows = page * DIFF_HEADS * 2
    ck = cache_k.reshape(depth, n_pool, k_rows, DIFF_DK)
    rows = DIFF_HEADS * seq
    bias = _head_match_bias(rows, seq, page * DIFF_HEADS, DIFF_HEADS)
    row = pl.BlockSpec((seq, SEG), lambda b, s, pt: (b, 0))

    def k_spec(n):
        return pl.BlockSpec((None, None, k_rows, DIFF_DK),
                            lambda b, s, pt: (layer, pt[b, s * PAGES_PER_STEP + n], 0, 0))

    def v_spec(n, e):
        return pl.BlockSpec((None, None, page, DIFF_HEADS, LANES),
                            lambda b, s, pt: (layer, pt[b, s * PAGES_PER_STEP + n], 0, 0, e))

    grid_spec = pltpu.PrefetchScalarGridSpec(
        num_scalar_prefetch=1,
        grid=(batch, n_pages // PAGES_PER_STEP),
        in_specs=[
            pl.BlockSpec((4, DIFF_DK), lambda b, s, pt: (0, 0)),
            pl.BlockSpec(bias.shape, lambda b, s, pt: (0, 0)),
            row, row, row,
            pl.BlockSpec((1, DIFF_DV), lambda b, s, pt: (0, 0)),
        ] + [k_spec(n) for n in range(PAGES_PER_STEP)]
          + [v_spec(n, e) for n in range(PAGES_PER_STEP) for e in range(2)],
        out_specs=row,
        scratch_shapes=[pltpu.VMEM((2, rows, 1), F32), pltpu.VMEM((2, rows, 1), F32),
                        pltpu.VMEM((2, rows, DIFF_DV), F32)],
    )
    return pl.pallas_call(
        functools.partial(_diff_sample_kernel, seq=seq, lam_init=lam_init),
        grid_spec=grid_spec,
        out_shape=jax.ShapeDtypeStruct((batch * seq, SEG), F32),
        compiler_params=_params(("parallel", "arbitrary")),
        name="diff_attention_sample",
    )(page_table, lam_rows, bias, dq, dk_new, dv_new, subln_g.reshape(1, DIFF_DV),
      *([ck] * PAGES_PER_STEP), *([cache_v] * (2 * PAGES_PER_STEP)))


def _norm_matmul_kernel(x_ref, g_ref, w_ref, o_ref, *, scale):
    h = _rms(x_ref[...], g_ref[...]).astype(BF16)
    z = jnp.dot(h, w_ref[...], preferred_element_type=F32)
    if scale != 1.0:
        z = z * scale
    o_ref[...] = z.astype(o_ref.dtype)


def _norm_matmul(x2d, g, w_bf, tm, out_dtype, scale=1.0):
    T, D = x2d.shape
    N = w_bf.shape[1]
    return pl.pallas_call(
        functools.partial(_norm_matmul_kernel, scale=scale),
        grid=(T // tm,),
        in_specs=[pl.BlockSpec((tm, D), lambda i: (i, 0)),
                  pl.BlockSpec((1, D), lambda i: (0, 0)),
                  pl.BlockSpec((D, N), lambda i: (0, 0))],
        out_specs=pl.BlockSpec((tm, N), lambda i: (i, 0)),
        out_shape=jax.ShapeDtypeStruct((T, N), out_dtype),
        compiler_params=_params(("parallel",)),
        name="norm_matmul",
    )(x2d, g.reshape(1, D), w_bf)


def _matmul_norm_res_kernel(*refs, n_in):
    a_refs = refs[:n_in]
    w_refs = refs[n_in:2 * n_in]
    g_ref, res_ref, o_ref = refs[2 * n_in:]
    z = None
    for a_ref, w_ref in zip(a_refs, w_refs):
        t = jnp.dot(a_ref[...].astype(BF16), w_ref[...], preferred_element_type=F32)
        z = t if z is None else z + t
    o_ref[...] = res_ref[...] + _rms(z, g_ref[...])


def _matmul_norm_residual(acts, ws_bf, g, res, tm):
    T, D = res.shape
    n_in = len(acts)
    in_specs = ([pl.BlockSpec((tm, a.shape[1]), lambda i: (i, 0)) for a in acts]
                + [pl.BlockSpec(w.shape, lambda i: (0, 0)) for w in ws_bf]
                + [pl.BlockSpec((1, D), lambda i: (0, 0)), pl.BlockSpec((tm, D), lambda i: (i, 0))])
    return pl.pallas_call(
        functools.partial(_matmul_norm_res_kernel, n_in=n_in),
        grid=(T // tm,),
        in_specs=in_specs,
        out_specs=pl.BlockSpec((tm, D), lambda i: (i, 0)),
        out_shape=jax.ShapeDtypeStruct((T, D), F32),
        compiler_params=_params(("parallel",)),
        name="matmul_norm_residual",
    )(*acts, *ws_bf, g.reshape(1, D), res)


def _softmax_rows(s):
    e = jnp.exp(s - jnp.max(s, axis=-1, keepdims=True))
    return e / jnp.sum(e, axis=-1, keepdims=True)


def _cross_prompt_kernel(q_ref, mk_ref, mv_ref, o_ref):
    for h in range(MEM_HEADS):
        sl = slice(h * MEM_DH, (h + 1) * MEM_DH)
        s = lax.dot_general(q_ref[:, sl], mk_ref[0, :, sl], _NT, preferred_element_type=F32)
        p = _softmax_rows(s).astype(BF16)
        o_ref[:, sl] = jnp.dot(p, mv_ref[0, :, sl], preferred_element_type=F32).astype(o_ref.dtype)


def _cross_attention_prompt(q, mk_bf, mv_bf, batch, seq, tq):
    nq = seq // tq
    W = MEM_HEADS * MEM_DH
    M = mk_bf.shape[1]
    kv = pl.BlockSpec((1, M, W), lambda b, i: (b, 0, 0))
    row = pl.BlockSpec((tq, W), lambda b, i: (b * nq + i, 0))
    return pl.pallas_call(
        _cross_prompt_kernel,
        grid=(batch, nq),
        in_specs=[row, kv, kv],
        out_specs=row,
        out_shape=jax.ShapeDtypeStruct((batch * seq, W), BF16),
        compiler_params=_params(("parallel", "parallel")),
        name="cross_attention_prompt",
    )(q, mk_bf, mv_bf)


def _cross_sample_kernel(bias_ref, q_ref, mk_ref, mv_ref, o_ref, *, group, seq):
    bias = bias_ref[...]
    outs = []
    for b in range(group):
        qb = q_ref[b * seq:(b + 1) * seq, :]
        qh = jnp.concatenate([qb[:, h * MEM_DH:(h + 1) * MEM_DH] for h in range(MEM_HEADS)], axis=0)
        s = lax.dot_general(qh.astype(BF16), mk_ref[b].astype(BF16), _NT, preferred_element_type=F32) + bias
        p = _softmax_rows(s).astype(BF16)
        o = jnp.dot(p, mv_ref[b].astype(BF16), preferred_element_type=F32)
        outs.append(jnp.concatenate([o[h * seq:(h + 1) * seq] for h in range(MEM_HEADS)], axis=1))
    o_ref[...] = jnp.concatenate(outs, axis=0)


def _cross_attention_sample(q, mem_k, mem_v, layer, batch, seq, group):
    depth, _, M = mem_k.shape[:3]
    W = MEM_HEADS * MEM_DH
    mk = mem_k.reshape(depth, batch, M * MEM_HEADS, MEM_DH)
    mv = mem_v.reshape(depth, batch, M * MEM_HEADS, MEM_DH)
    bias = _head_match_bias(MEM_HEADS * seq, seq, M * MEM_HEADS, MEM_HEADS)
    kv = pl.BlockSpec((None, group, M * MEM_HEADS, MEM_DH), lambda i: (layer, i, 0, 0))
    row = pl.BlockSpec((group * seq, W), lambda i: (i, 0))
    return pl.pallas_call(
        functools.partial(_cross_sample_kernel, group=group, seq=seq),
        grid=(batch // group,),
        in_specs=[pl.BlockSpec(bias.shape, lambda i: (0, 0)), row, kv, kv],
        out_specs=row,
        out_shape=jax.ShapeDtypeStruct((batch * seq, W), F32),
        compiler_params=_params(("parallel",)),
        name="cross_attention_sample",
    )(bias, q, mk, mv)


def _mlp_kernel(x_ref, gpre_ref, wup_ref, wdn_ref, gpost_ref, o_ref, h_scr):
    j = pl.program_id(1)

    @pl.when(j == 0)
    def _():
        h_scr[...] = _rms(x_ref[...], gpre_ref[...]).astype(BF16)

        o_ref[...] = jnp.zeros_like(o_ref)

    u = jnp.maximum(jnp.dot(h_scr[...], wup_ref[...], preferred_element_type=F32), 0.0)
    o_ref[...] += jnp.dot((u * u).astype(BF16), wdn_ref[...], preferred_element_type=F32)

    @pl.when(j == pl.num_programs(1) - 1)
    def _():
        o_ref[...] = x_ref[...] + _rms(o_ref[...], gpost_ref[...])


def _mlp(x2d, g_pre, w_up_bf, w_dn_bf, g_post, tm, tf):
    T, D = x2d.shape
    FF = w_up_bf.shape[1]
    vec = pl.BlockSpec((1, D), lambda i, j: (0, 0))
    return pl.pallas_call(
        _mlp_kernel,
        grid=(T // tm, FF // tf),
        in_specs=[pl.BlockSpec((tm, D), lambda i, j: (i, 0)), vec,
                  pl.BlockSpec((D, tf), lambda i, j: (0, j)),
                  pl.BlockSpec((tf, D), lambda i, j: (j, 0)), vec],
        out_specs=pl.BlockSpec((tm, D), lambda i, j: (i, 0)),
        out_shape=jax.ShapeDtypeStruct((T, D), F32),
        scratch_shapes=[pltpu.VMEM((tm, D), BF16)],
        compiler_params=_params(("parallel", "arbitrary")),
        name="mlp",
    )(x2d, g_pre.reshape(1, D), w_up_bf, w_dn_bf, g_post.reshape(1, D))


def _position_tables(pos, tm):
    reps = max(1, tm // pos.shape[0])
    pos = jnp.tile(pos, reps)
    cr, srl, srh = _rope_tables(pos, RET_DK, RET_THETA)
    cd, sdl, sdh = _rope_tables(pos, ROPE_DIM, ROPE_THETA)
    return (cr, srl + srh, cd, sdl, sdh), pos.shape[0] // tm


def _layer_tail(x2d, mix_parts, wts, cross_fn, tm):
    (w_out_parts, g_mix_post, g_mem_pre, w_mem_q, w_mem_o, g_mem_post,
     g_mlp_pre, w_up, w_down, g_mlp_post) = wts
    x1 = _matmul_norm_residual(mix_parts, w_out_parts, g_mix_post, x2d, tm)
    q = cross_fn(x1, g_mem_pre, w_mem_q)
    x2 = _matmul_norm_residual([q], [w_mem_o], g_mem_post, x1, tm)
    return _mlp(x2, g_mlp_pre, w_up, w_down, g_mlp_post, tm, 1024)


def kernel(x_prompt, x_sample, mem_prompt, state_ret, cache_diff_k, cache_diff_v, cache_mem_k, cache_mem_v, page_table, w_in, w_out, diff_lambda_q1, diff_lambda_k1, diff_lambda_q2, diff_lambda_k2, diff_subln_g, norm_mix_pre, norm_mix_post, norm_mem_pre, norm_mem_post, norm_mlp_pre, norm_mlp_post, mem_norm_g, w_mem_q, w_mem_k, w_mem_v, w_mem_o, w_mlp_up, w_mlp_down):
    depth = w_in.shape[0]
    B, L_p, D = x_prompt.shape
    B_s, L_s, _ = x_sample.shape
    n_pages, page = page_table.shape[1], cache_diff_k.shape[2]
    past_len = n_pages * page
    M = mem_prompt.shape[1]
    W_MEM = MEM_HEADS * MEM_DH
    TM = 512
    ret_chunk_p = math.gcd(L_p, RET_CHUNK)

    tabs_p, per_p = _position_tables(jnp.arange(L_p, dtype=F32), TM)
    TM_IN_S = 256
    tabs_s, per_s = _position_tables(past_len + jnp.arange(L_s, dtype=F32), TM_IN_S)

    yp = x_prompt.reshape(B * L_p, D)
    ys = x_sample.reshape(B_s * L_s, D)
    mem2d = mem_prompt.reshape(B * M, D)
    outs = {k: [] for k in ("rp", "kp", "vp", "mkp", "mvp", "rs", "ks", "vs")}

    for i in range(depth):
        lam_init = 0.8 - 0.6 * math.exp(-0.3 * i)
        lam_rows = jnp.stack([diff_lambda_q1[i], diff_lambda_k1[i], diff_lambda_q2[i], diff_lambda_k2[i]])
        w_in_bf = w_in[i].astype(BF16)
        half = w_out.shape[1] // 2
        tail_w = ([w_out[i, :half].astype(BF16), w_out[i, half:].astype(BF16)], norm_mix_post[i],
                  norm_mem_pre[i], w_mem_q[i].astype(BF16), w_mem_o[i].astype(BF16), norm_mem_post[i],
                  norm_mlp_pre[i], w_mlp_up[i].astype(BF16), w_mlp_down[i].astype(BF16), norm_mlp_post[i])

        mk_p = _norm_matmul(mem2d, mem_norm_g[i], w_mem_k[i].astype(BF16), B * M, F32)
        mv_p = _norm_matmul(mem2d, mem_norm_g[i], w_mem_v[i].astype(BF16), B * M, F32)
        rq, rk, rv, rg, dq_t, dk, dv, dv_t = _in_projection(
            yp, norm_mix_pre[i], w_in_bf, tabs_p, per_p, TM, BF16, DIFF_DK ** -0.5 * LOG2E, True)
        mix_ret, s_p = _retention_prompt(rq, rk, rv, rg, B, L_p, ret_chunk_p)
        mix_diff = _diff_attention_prompt(dq_t, dk, dv_t, lam_rows, diff_subln_g[i], B, L_p, lam_init)
        mk_bf = mk_p.astype(BF16).reshape(B, M, W_MEM)
        mv_bf = mv_p.astype(BF16).reshape(B, M, W_MEM)

        def cross_p(x1, g, wq):
            q = _norm_matmul(x1, g, wq, TM, BF16, scale=MEM_DH ** -0.5)
            return _cross_attention_prompt(q, mk_bf, mv_bf, B, L_p, TM)

        yp = _layer_tail(yp, [mix_ret, mix_diff], tail_w, cross_p, TM)
        outs["rp"].append(s_p.astype(state_ret.dtype))
        outs["kp"].append(dk.reshape(B, L_p, DIFF_HEADS, 2, DIFF_DK))
        outs["vp"].append(dv.reshape(B, L_p, DIFF_HEADS, DIFF_DV))
        outs["mkp"].append(mk_p.reshape(B, M, MEM_HEADS, MEM_DH))
        outs["mvp"].append(mv_p.reshape(B, M, MEM_HEADS, MEM_DH))

        rq, rk, rv, rg, dq, dk, dv = _in_projection(ys, norm_mix_pre[i], w_in_bf, tabs_s, per_s, TM_IN_S, F32,
                                                    DIFF_DK ** -0.5, False)
        mix_ret, s_s = _retention_sample(rq, rk, rv, rg, state_ret, i, B_s, L_s, 4)
        mix_diff = _diff_attention_sample(dq, dk, dv, cache_diff_k, cache_diff_v, i, page_table,
                                          lam_rows, diff_subln_g[i], B_s, L_s, lam_init)
        def cross_s(x1, g, wq):
            q = _norm_matmul(x1, g, wq, TM, F32, scale=MEM_DH ** -0.5)
            return _cross_attention_sample(q, cache_mem_k, cache_mem_v, i, B_s, L_s, 8)

        ys = _layer_tail(ys, [mix_ret, mix_diff], tail_w, cross_s, TM)
        outs["rs"].append(s_s.astype(state_ret.dtype))
        outs["ks"].append(dk.reshape(B_s, L_s, DIFF_HEADS, 2, DIFF_DK))
        outs["vs"].append(dv.reshape(B_s, L_s, DIFF_HEADS, DIFF_DV))

    st = lambda k: outs[k][0][None] if depth == 1 else jnp.stack(outs[k])
    return (yp.reshape(B, L_p, D), ys.reshape(B_s, L_s, D), st("rp"), st("kp"), st("vp"),
            st("mkp"), st("mvp"), st("rs"), st("ks"), st("vs"))
```

```python
import functools
import math

import jax
import jax.numpy as jnp
from jax import lax
from jax.experimental import pallas as pl
from jax.experimental.pallas import tpu as pltpu

F32 = jnp.float32
BF16 = jnp.bfloat16

LANES = 128
MXU_COLS = 256
RET_HEADS = 8
RET_DK = 128
RET_DV = 128
RET_THETA = 10000.0
RET_CHUNK = 2 * 128
DIFF_HEADS = 4
DIFF_DK = 128
DIFF_DV = 256
ROPE_THETA = 500000.0
ROPE_DIM = DIFF_DK // 4
MEM_HEADS = 4
MEM_DH = 128
Q_BLOCK = 128
EPS = 1e-6
LOG2E = math.log2(math.e)
SEG = 1024
N_SEG = 7
MLP_FF_TILE = 1024

VMEM_LIMIT = 56 * 1024 * 1024

_NT = (((1,), (1,)), ((), ()))
_TN = (((0,), (0,)), ((), ()))


def _params(sem, vmem=VMEM_LIMIT):
    return pltpu.CompilerParams(dimension_semantics=sem, vmem_limit_bytes=vmem)


def _rms(x, g):
    return x * lax.rsqrt(jnp.mean(x * x, axis=-1, keepdims=True) + EPS) * g


def _unit_rms(o):
    return o * lax.rsqrt(jnp.mean(o * o, axis=-1, keepdims=True) + EPS)


def _rope_tables(pos, rot_dim, theta):
    half = rot_dim // 2
    inv = jnp.exp(-math.log(theta) * (2.0 * jnp.arange(half, dtype=F32) / rot_dim))
    ang = pos[:, None] * inv[None, :]
    cos, sin = jnp.cos(ang), jnp.sin(ang)
    n = pos.shape[0]
    rest = LANES - rot_dim
    c = jnp.concatenate([cos, cos, jnp.ones((n, rest), F32)], axis=-1)
    s_lo = jnp.concatenate([-sin, jnp.zeros((n, LANES - half), F32)], axis=-1)
    s_hi = jnp.concatenate([jnp.zeros((n, half), F32), sin, jnp.zeros((n, rest), F32)], axis=-1)
    return c, s_lo, s_hi


def _inproj_kernel(x_ref, g_ref, w_ref, cr_ref, sr_ref, cd_ref, sdl_ref, sdh_ref, *rest,
                   dq_scale, transposed):
    if transposed:
        rq_ref, rk_ref, rv_ref, rg_ref, dq_ref, dk_ref, dv_ref, dvt_ref, h_scr = rest
    else:
        rq_ref, rk_ref, rv_ref, rg_ref, dq_ref, dk_ref, dv_ref, h_scr = rest
    j = pl.program_id(1)
    heads = [slice(h * LANES, (h + 1) * LANES) for h in range(SEG // LANES)]

    @pl.when(j == 0)
    def _():
        h_scr[...] = _rms(x_ref[...], g_ref[...]).astype(BF16)

    def project():
        return jnp.dot(h_scr[...], w_ref[...], preferred_element_type=F32)

    def ret_rot(out_ref, scale):
        z_all = project()
        c, s = cr_ref[...], sr_ref[...]
        for sl in heads:
            z = z_all[:, sl]
            r = z * c + pltpu.roll(z, RET_DK // 2, 1) * s
            if scale != 1.0:
                r = r * scale
            out_ref[:, sl] = r.astype(out_ref.dtype)

    def diff_rot(z):
        half = ROPE_DIM // 2
        return (z * cd_ref[...] + pltpu.roll(z, LANES - half, 1) * sdl_ref[...]
                + pltpu.roll(z, half, 1) * sdh_ref[...])

    @pl.when(j == 0)
    def _():
        ret_rot(rq_ref, 1.0)

    @pl.when(j == 1)
    def _():
        ret_rot(rk_ref, RET_DK ** -0.5)

    @pl.when(j == 2)
    def _():
        rv_ref[...] = project().astype(rv_ref.dtype)

    @pl.when(j == 3)
    def _():
        rg_ref[...] = project()

    @pl.when(j == 4)
    def _():
        z_all = project()
        for sl in heads:
            r = diff_rot(z_all[:, sl]) * dq_scale
            if transposed:
                dq_ref[sl, :] = r.T.astype(dq_ref.dtype)
            else:
                dq_ref[:, sl] = r.astype(dq_ref.dtype)

    @pl.when(j == 5)
    def _():
        z_all = project()
        for sl in heads:
            dk_ref[:, sl] = diff_rot(z_all[:, sl])

    @pl.when(j == 6)
    def _():
        z_all = project()
        dv_ref[...] = z_all
        if transposed:
            for sl in heads:
                dvt_ref[sl, :] = z_all[:, sl].T.astype(dvt_ref.dtype)


def _in_projection(x2d, g, w_bf, tabs, period_blocks, tm, act_dtype, dq_scale, transposed):
    T, D = x2d.shape
    cr, sr, cd, sdl, sdh = tabs
    tab_spec = pl.BlockSpec((tm, LANES), lambda i, j: (i % period_blocks, 0))
    row_spec = pl.BlockSpec((tm, SEG), lambda i, j: (i, 0))
    col_spec = pl.BlockSpec((SEG, tm), lambda i, j: (0, i))
    sds = lambda dt: jax.ShapeDtypeStruct((T, SEG), dt)
    sds_t = lambda dt: jax.ShapeDtypeStruct((SEG, T), dt)
    out_specs = [row_spec] * 4 + [col_spec if transposed else row_spec, row_spec, row_spec]
    out_shape = [sds(act_dtype), sds(act_dtype), sds(act_dtype), sds(F32),
                 sds_t(act_dtype) if transposed else sds(act_dtype), sds(F32), sds(F32)]
    if transposed:
        out_specs.append(col_spec)
        out_shape.append(sds_t(act_dtype))
    return pl.pallas_call(
        functools.partial(_inproj_kernel, dq_scale=dq_scale, transposed=transposed),
        grid=(T // tm, N_SEG),
        in_specs=[
            pl.BlockSpec((tm, D), lambda i, j: (i, 0)),
            pl.BlockSpec((1, D), lambda i, j: (0, 0)),
            pl.BlockSpec((D, SEG), lambda i, j: (0, j)),
            tab_spec, tab_spec, tab_spec, tab_spec, tab_spec,
        ],
        out_specs=out_specs,
        out_shape=out_shape,
        scratch_shapes=[pltpu.VMEM((tm, D), BF16)],
        compiler_params=_params(("parallel", "arbitrary")),
        name="in_projection",
    )(x2d, g.reshape(1, D), w_bf, cr, sr, cd, sdl, sdh)


def _ret_decay_tables(chunk):
    lg = jnp.log1p(-jnp.exp2(-5.0 - jnp.arange(RET_HEADS, dtype=F32)))
    idx = jnp.arange(chunk, dtype=F32)
    rel = idx[:, None] - idx[None, :]
    dmat = jnp.where(rel[None] >= 0, jnp.exp(jnp.maximum(rel, 0.0)[None] * lg[:, None, None]), 0.0)
    q_decay = jnp.exp((idx + 1.0)[:, None] * lg[None, :])
    k_decay = jnp.exp((chunk - 1.0 - idx)[:, None] * lg[None, :])
    chunk_decay = jnp.exp(chunk * lg)
    widen = lambda t: jnp.repeat(t, RET_DK, axis=1)
    return dmat, widen(q_decay), widen(k_decay), chunk_decay


def _ret_head(q, k, v, s, dmat, qd, kd, cd, gate):
    qb, kb, vb = q.astype(BF16), k.astype(BF16), v.astype(BF16)
    att = lax.dot_general(qb, kb, _NT, preferred_element_type=F32) * dmat
    o = (jnp.dot(att.astype(BF16), vb, preferred_element_type=F32)
         + jnp.dot(qb, s.astype(BF16), preferred_element_type=F32) * qd)
    kdk = (k.astype(F32) * kd).astype(BF16)
    s_new = s * cd + lax.dot_general(kdk, vb, _TN, preferred_element_type=F32)
    y = _unit_rms(o) * (gate * jax.nn.sigmoid(gate))
    return y, s_new


def _ret_prompt_kernel(cd_ref, q_ref, k_ref, v_ref, g_ref, dmat_ref, qd_ref, kd_ref, mix_ref, s_ref):
    @pl.when(pl.program_id(1) == 0)
    def _():
        s_ref[...] = jnp.zeros_like(s_ref)

    for h in range(RET_HEADS):
        sl = slice(h * RET_DK, (h + 1) * RET_DK)
        y, s_new = _ret_head(q_ref[:, sl], k_ref[:, sl], v_ref[:, sl], s_ref[0, h], dmat_ref[h],
                             qd_ref[:, sl], kd_ref[:, sl], cd_ref[h], g_ref[:, sl])
        s_ref[0, h] = s_new
        mix_ref[:, sl] = y.astype(mix_ref.dtype)


def _retention_prompt(rq, rk, rv, rg, batch, seq, chunk):
    n = seq // chunk
    dmat, qd, kd, cd = _ret_decay_tables(chunk)
    row = pl.BlockSpec((chunk, SEG), lambda b, c: (b * n + c, 0))
    full2 = pl.BlockSpec((chunk, SEG), lambda b, c: (0, 0))
    return pl.pallas_call(
        _ret_prompt_kernel,
        grid=(batch, n),
        in_specs=[
            pl.BlockSpec(memory_space=pltpu.SMEM),
            row, row, row, row,
            pl.BlockSpec((RET_HEADS, chunk, chunk), lambda b, c: (0, 0, 0)),
            full2, full2,
        ],
        out_specs=[row, pl.BlockSpec((1, RET_HEADS, RET_DK, RET_DV), lambda b, c: (b, 0, 0, 0))],
        out_shape=[jax.ShapeDtypeStruct((batch * seq, SEG), BF16),
                   jax.ShapeDtypeStruct((batch, RET_HEADS, RET_DK, RET_DV), F32)],
        compiler_params=_params(("parallel", "arbitrary")),
        name="retention_prompt",
    )(cd, rq, rk, rv, rg, dmat, qd, kd)


def _ret_sample_kernel(cd_ref, q_ref, k_ref, v_ref, g_ref, s0_ref, dmat_ref, qd_ref, kd_ref,
                       mix_ref, s_ref, *, group, seq):
    rows = []
    for b in range(group):
        r = slice(b * seq, (b + 1) * seq)
        heads = []
        for h in range(RET_HEADS):
            sl = slice(h * RET_DK, (h + 1) * RET_DK)
            y, s_new = _ret_head(q_ref[r, sl], k_ref[r, sl], v_ref[r, sl], s0_ref[b, h], dmat_ref[h],
                                 qd_ref[:, sl], kd_ref[:, sl], cd_ref[h], g_ref[r, sl])
            s_ref[b, h] = s_new
            heads.append(y)
        rows.append(jnp.concatenate(heads, axis=1))
    mix_ref[...] = jnp.concatenate(rows, axis=0)


def _retention_sample(rq, rk, rv, rg, state, layer, batch, seq, group):
    dmat, qd, kd, cd = _ret_decay_tables(seq)
    row = pl.BlockSpec((group * seq, SEG), lambda i: (i, 0))
    tab = pl.BlockSpec((seq, SEG), lambda i: (0, 0))
    st = pl.BlockSpec((group, RET_HEADS, RET_DK, RET_DV), lambda i: (i, 0, 0, 0))
    st_in = pl.BlockSpec((None, group, RET_HEADS, RET_DK, RET_DV), lambda i: (layer, i, 0, 0, 0))
    return pl.pallas_call(
        functools.partial(_ret_sample_kernel, group=group, seq=seq),
        grid=(batch // group,),
        in_specs=[
            pl.BlockSpec(memory_space=pltpu.SMEM),
            row, row, row, row, st_in,
            pl.BlockSpec((RET_HEADS, seq, seq), lambda i: (0, 0, 0)),
            tab, tab,
        ],
        out_specs=[row, st],
        out_shape=[jax.ShapeDtypeStruct((batch * seq, SEG), F32),
                   jax.ShapeDtypeStruct((batch, RET_HEADS, RET_DK, RET_DV), F32)],
        compiler_params=_params(("parallel",)),
        name="retention_sample",
    )(cd, rq, rk, rv, rg, state, dmat, qd, kd)


def _lambda_value(lam_ref, lam_init):
    a = jnp.sum(lam_ref[0:1, :] * lam_ref[1:2, :], axis=-1, keepdims=True)
    b = jnp.sum(lam_ref[2:3, :] * lam_ref[3:4, :], axis=-1, keepdims=True)
    return jnp.exp(a) - jnp.exp(b) + lam_init


def _softmax_update(s, m_old, l_old):
    m_new = jnp.maximum(m_old, jnp.max(s, axis=-1, keepdims=True))
    alpha = jnp.exp(m_old - m_new)
    p = jnp.exp(s - m_new)
    l_new = alpha * l_old + jnp.sum(p, axis=-1, keepdims=True)
    return p, alpha, m_new, l_new


def _diff_prompt_kernel(qi_tab, ki_tab, lam_ref, qt_ref, k_ref, vt_ref, g_ref, o_ref, m_scr, l_scr, acc_scr,
                        *, tq, tk, lam_init):
    t = pl.program_id(1)
    qi, ki = qi_tab[t], ki_tab[t]

    @pl.when(ki == 0)
    def _():
        m_scr[...] = jnp.full_like(m_scr, -jnp.inf)
        l_scr[...] = jnp.zeros_like(l_scr)
        acc_scr[...] = jnp.zeros_like(acc_scr)

    def step(masked):
        if masked:
            kpos = ki * tk + lax.broadcasted_iota(jnp.int32, (tk, tq), 0)
            qpos = qi * tq + lax.broadcasted_iota(jnp.int32, (tk, tq), 1)
            keep = qpos >= kpos
        for h in range(DIFF_HEADS):
            vt = vt_ref[h * DIFF_DV:(h + 1) * DIFF_DV, :]
            for m in range(2):
                i = 2 * h + m
                sl = slice(i * DIFF_DK, (i + 1) * DIFF_DK)
                kb = k_ref[:, sl].astype(BF16)
                for c in range(tq // MXU_COLS):
                    cs = slice(c * MXU_COLS, (c + 1) * MXU_COLS)
                    st = jnp.dot(kb, qt_ref[sl, cs], preferred_element_type=F32)
                    if masked:
                        st = jnp.where(keep[:, cs], st, -jnp.inf)
                    m_old = m_scr[i, :, cs]
                    m_new = jnp.maximum(m_old, jnp.max(st, axis=0, keepdims=True))
                    alpha = jnp.exp2(m_old - m_new)
                    p = jnp.exp2(st - m_new)
                    l_scr[i, :, cs] = alpha * l_scr[i, :, cs] + jnp.sum(p, axis=0, keepdims=True)
                    m_scr[i, :, cs] = m_new
                    acc_scr[i, :, cs] = (alpha * acc_scr[i, :, cs]
                                         + jnp.dot(vt, p.astype(BF16), preferred_element_type=F32))

    @pl.when(ki < qi)
    def _():
        step(False)

    @pl.when(ki == qi)
    def _():
        step(True)
        lam = _lambda_value(lam_ref, lam_init)
        for h in range(DIFF_HEADS):
            a, b = 2 * h, 2 * h + 1
            ot = acc_scr[a] * (1.0 / l_scr[a]) - lam * (acc_scr[b] * (1.0 / l_scr[b]))
            yt = ot * lax.rsqrt(jnp.mean(ot * ot, axis=0, keepdims=True) + EPS)
            o_ref[:, h * DIFF_DV:(h + 1) * DIFF_DV] = (yt.T * g_ref[...] * (1.0 - lam_init)).astype(o_ref.dtype)


def _diff_attention_prompt(dq_t, dk, dv_t, lam_rows, subln_g, batch, seq, lam_init, tq=512, tk=512):
    assert tq == tk
    nq = seq // tq
    pairs = [(qi, ki) for qi in range(nq) for ki in range(qi + 1)]
    qi_tab = jnp.asarray([p[0] for p in pairs], jnp.int32)
    ki_tab = jnp.asarray([p[1] for p in pairs], jnp.int32)
    n_sub = 2 * DIFF_HEADS
    grid_spec = pltpu.PrefetchScalarGridSpec(
        num_scalar_prefetch=2,
        grid=(batch, len(pairs)),
        in_specs=[
            pl.BlockSpec((4, DIFF_DK), lambda b, t, qt, kt: (0, 0)),
            pl.BlockSpec((SEG, tq), lambda b, t, qt, kt: (0, b * nq + qt[t])),
            pl.BlockSpec((tk, SEG), lambda b, t, qt, kt: (b * nq + kt[t], 0)),
            pl.BlockSpec((SEG, tk), lambda b, t, qt, kt: (0, b * nq + kt[t])),
            pl.BlockSpec((1, DIFF_DV), lambda b, t, qt, kt: (0, 0)),
        ],
        out_specs=pl.BlockSpec((tq, SEG), lambda b, t, qt, kt: (b * nq + qt[t], 0)),
        scratch_shapes=[pltpu.VMEM((n_sub, 1, tq), F32), pltpu.VMEM((n_sub, 1, tq), F32),
                        pltpu.VMEM((n_sub, DIFF_DV, tq), F32)],
    )
    return pl.pallas_call(
        functools.partial(_diff_prompt_kernel, tq=tq, tk=tk, lam_init=lam_init),
        grid_spec=grid_spec,
        out_shape=jax.ShapeDtypeStruct((batch * seq, SEG), BF16),
        compiler_params=_params(("parallel", "arbitrary")),
        name="diff_attention_prompt",
    )(qi_tab, ki_tab, lam_rows, dq_t, dk, dv_t, subln_g.reshape(1, DIFF_DV))


PAGES_PER_STEP = 8


def _head_match_bias(n_rows, seq, n_cols, heads):
    row_h = jnp.arange(n_rows, dtype=jnp.int32)[:, None] // seq
    col_h = jnp.arange(n_cols, dtype=jnp.int32)[None, :] % heads
    return jnp.where(row_h == col_h, 0.0, -jnp.inf).astype(F32)


def _stack_heads(x, width, offset, stride):
    return jnp.concatenate(
        [x[:, offset + h * stride: offset + h * stride + width] for h in range(DIFF_HEADS)], axis=0)


def _diff_sample_kernel(pt_ref, lam_ref, bias_ref, q_ref, kn_ref, vn_ref, g_ref, *rest, seq, lam_init):
    k_refs = rest[:PAGES_PER_STEP]
    v_refs = rest[PAGES_PER_STEP:2 * PAGES_PER_STEP]
    o_ref, m_scr, l_scr, acc_scr = rest[2 * PAGES_PER_STEP:]
    step = pl.program_id(1)
    rows = DIFF_HEADS * seq
    kv_rows = k_refs[0].shape[0] // 2

    @pl.when(step == 0)
    def _():
        m_scr[...] = jnp.full_like(m_scr, -jnp.inf)
        l_scr[...] = jnp.zeros_like(l_scr)
        acc_scr[...] = jnp.zeros_like(acc_scr)

    q = q_ref[...]
    qs = [_stack_heads(q, DIFF_DK, m * DIFF_DK, 2 * DIFF_DK).astype(BF16) for m in range(2)]

    def update(m, s):
        p, alpha, m_new, l_new = _softmax_update(s, m_scr[m], l_scr[m])
        m_scr[m] = m_new
        l_scr[m] = l_new
        acc_scr[m] = alpha * acc_scr[m]
        return p

    bias = bias_ref[...]
    ps = []
    for m in range(2):
        s = jnp.concatenate(
            [lax.dot_general(qs[m], kr[pl.ds(m, kv_rows, stride=2), :].astype(BF16), _NT,
                             preferred_element_type=F32) + bias for kr in k_refs], axis=1)
        ps.append(update(m, s))
    p = jnp.concatenate(ps, axis=0).astype(BF16)
    for e in range(2):
        pv = None
        for n in range(PAGES_PER_STEP):
            v = v_refs[n][:, :, e * LANES:(e + 1) * LANES].reshape(kv_rows, LANES).astype(BF16)
            t = jnp.dot(p[:, n * kv_rows:(n + 1) * kv_rows], v, preferred_element_type=F32)
            pv = t if pv is None else pv + t
        for m in range(2):
            acc_scr[m, :, e * LANES:(e + 1) * LANES] += pv[m * rows:(m + 1) * rows]

    @pl.when(step == pl.num_programs(1) - 1)
    def _():
        kn, vn = kn_ref[...], vn_ref[...]
        causal = (lax.broadcasted_iota(jnp.int32, (seq, seq), 0)
                  >= lax.broadcasted_iota(jnp.int32, (seq, seq), 1))
        for m in range(2):
            s = jnp.concatenate(
                [jnp.where(causal,
                           lax.dot_general(qs[m][h * seq:(h + 1) * seq],
                                           kn[:, (2 * h + m) * DIFF_DK:(2 * h + m + 1) * DIFF_DK].astype(BF16),
                                           _NT, preferred_element_type=F32),
                           -jnp.inf) for h in range(DIFF_HEADS)], axis=0)
            pn = update(m, s).astype(BF16)
            acc_scr[m] += jnp.concatenate(
                [jnp.dot(pn[h * seq:(h + 1) * seq], vn[:, h * DIFF_DV:(h + 1) * DIFF_DV].astype(BF16),
                         preferred_element_type=F32) for h in range(DIFF_HEADS)], axis=0)
        lam = _lambda_value(lam_ref, lam_init)
        o = acc_scr[0] / l_scr[0] - lam * (acc_scr[1] / l_scr[1])
        y = _unit_rms(o) * g_ref[...] * (1.0 - lam_init)
        o_ref[...] = jnp.concatenate([y[h * seq:(h + 1) * seq] for h in range(DIFF_HEADS)], axis=1)


def _diff_attention_sample(dq, dk_new, dv_new, cache_k, cache_v, layer, page_table, lam_rows, subln_g,
                           batch, seq, lam_init):
    depth, n_pool, page = cache_k.shape[:3]
    n_pages = page_table.shape[1]
    assert n_pages % PAGES_PER_STEP == 0 and DIFF_DV == 2 * LANES
    k_rows = page * DIFF_HEADS * 2
    ck = cache_k.reshape(depth, n_pool, k_rows, DIFF_DK)
    rows = DIFF_HEADS * seq
    bias = _head_match_bias(rows, seq, page * DIFF_HEADS, DIFF_HEADS)
    row = pl.BlockSpec((seq, SEG), lambda b, s, pt: (b, 0))

    def k_spec(n):
        return pl.BlockSpec((None, None, k_rows, DIFF_DK),
                            lambda b, s, pt: (layer, pt[b, s * PAGES_PER_STEP + n], 0, 0))

    def v_spec(n):
        return pl.BlockSpec((None, None, page, DIFF_HEADS, DIFF_DV),
                            lambda b, s, pt: (layer, pt[b, s * PAGES_PER_STEP + n], 0, 0, 0))

    grid_spec = pltpu.PrefetchScalarGridSpec(
        num_scalar_prefetch=1,
        grid=(batch, n_pages // PAGES_PER_STEP),
        in_specs=[
            pl.BlockSpec((4, DIFF_DK), lambda b, s, pt: (0, 0)),
            pl.BlockSpec(bias.shape, lambda b, s, pt: (0, 0)),
            row, row, row,
            pl.BlockSpec((1, DIFF_DV), lambda b, s, pt: (0, 0)),
        ] + [k_spec(n) for n in range(PAGES_PER_STEP)]
          + [v_spec(n) for n in range(PAGES_PER_STEP)],
        out_specs=row,
        scratch_shapes=[pltpu.VMEM((2, rows, 1), F32), pltpu.VMEM((2, rows, 1), F32),
                        pltpu.VMEM((2, rows, DIFF_DV), F32)],
    )
    return pl.pallas_call(
        functools.partial(_diff_sample_kernel, seq=seq, lam_init=lam_init),
        grid_spec=grid_spec,
        out_shape=jax.ShapeDtypeStruct((batch * seq, SEG), F32),
        compiler_params=_params(("parallel", "arbitrary")),
        name="diff_attention_sample",
    )(page_table, lam_rows, bias, dq, dk_new, dv_new, subln_g.reshape(1, DIFF_DV),
      *([ck] * PAGES_PER_STEP), *([cache_v] * PAGES_PER_STEP))


def _norm_matmul_kernel(x_ref, g_ref, w_ref, o_ref, *, scale):
    h = _rms(x_ref[...], g_ref[...]).astype(BF16)
    z = jnp.dot(h, w_ref[...], preferred_element_type=F32)
    if scale != 1.0:
        z = z * scale
    o_ref[...] = z.astype(o_ref.dtype)


def _norm_matmul(x2d, g, w_bf, tm, out_dtype, scale=1.0):
    T, D = x2d.shape
    N = w_bf.shape[1]
    return pl.pallas_call(
        functools.partial(_norm_matmul_kernel, scale=scale),
        grid=(T // tm,),
        in_specs=[pl.BlockSpec((tm, D), lambda i: (i, 0)),
                  pl.BlockSpec((1, D), lambda i: (0, 0)),
                  pl.BlockSpec((D, N), lambda i: (0, 0))],
        out_specs=pl.BlockSpec((tm, N), lambda i: (i, 0)),
        out_shape=jax.ShapeDtypeStruct((T, N), out_dtype),
        compiler_params=_params(("parallel",)),
        name="norm_matmul",
    )(x2d, g.reshape(1, D), w_bf)


def _matmul_norm_res_kernel(*refs, n_in):
    a_refs = refs[:n_in]
    w_refs = refs[n_in:2 * n_in]
    g_ref, res_ref, o_ref = refs[2 * n_in:]
    z = None
    for a_ref, w_ref in zip(a_refs, w_refs):
        t = jnp.dot(a_ref[...].astype(BF16), w_ref[...], preferred_element_type=F32)
        z = t if z is None else z + t
    o_ref[...] = res_ref[...] + _rms(z, g_ref[...])


def _matmul_norm_residual(acts, ws_bf, g, res, tm):
    T, D = res.shape
    n_in = len(acts)
    in_specs = ([pl.BlockSpec((tm, a.shape[1]), lambda i: (i, 0)) for a in acts]
                + [pl.BlockSpec(w.shape, lambda i: (0, 0)) for w in ws_bf]
                + [pl.BlockSpec((1, D), lambda i: (0, 0)), pl.BlockSpec((tm, D), lambda i: (i, 0))])
    return pl.pallas_call(
        functools.partial(_matmul_norm_res_kernel, n_in=n_in),
        grid=(T // tm,),
        in_specs=in_specs,
        out_specs=pl.BlockSpec((tm, D), lambda i: (i, 0)),
        out_shape=jax.ShapeDtypeStruct((T, D), F32),
        compiler_params=_params(("parallel",)),
        name="matmul_norm_residual",
    )(*acts, *ws_bf, g.reshape(1, D), res)


def _softmax_rows(s):
    e = jnp.exp(s - jnp.max(s, axis=-1, keepdims=True))
    return e / jnp.sum(e, axis=-1, keepdims=True)


def _cross_sample_kernel(bias_ref, q_ref, mk_ref, mv_ref, o_ref, *, group, seq):
    bias = bias_ref[...]
    outs = []
    for b in range(group):
        qb = q_ref[b * seq:(b + 1) * seq, :]
        qh = jnp.concatenate([qb[:, h * MEM_DH:(h + 1) * MEM_DH] for h in range(MEM_HEADS)], axis=0)
        s = lax.dot_general(qh.astype(BF16), mk_ref[b].astype(BF16), _NT, preferred_element_type=F32) + bias
        p = _softmax_rows(s).astype(BF16)
        o = jnp.dot(p, mv_ref[b].astype(BF16), preferred_element_type=F32)
        outs.append(jnp.concatenate([o[h * seq:(h + 1) * seq] for h in range(MEM_HEADS)], axis=1))
    o_ref[...] = jnp.concatenate(outs, axis=0)


def _cross_attention_sample(q, mem_k, mem_v, layer, batch, seq, group):
    depth, _, M = mem_k.shape[:3]
    W = MEM_HEADS * MEM_DH
    mk = mem_k.reshape(depth, batch, M * MEM_HEADS, MEM_DH)
    mv = mem_v.reshape(depth, batch, M * MEM_HEADS, MEM_DH)
    bias = _head_match_bias(MEM_HEADS * seq, seq, M * MEM_HEADS, MEM_HEADS)
    kv = pl.BlockSpec((None, group, M * MEM_HEADS, MEM_DH), lambda i: (layer, i, 0, 0))
    row = pl.BlockSpec((group * seq, W), lambda i: (i, 0))
    return pl.pallas_call(
        functools.partial(_cross_sample_kernel, group=group, seq=seq),
        grid=(batch // group,),
        in_specs=[pl.BlockSpec(bias.shape, lambda i: (0, 0)), row, kv, kv],
        out_specs=row,
        out_shape=jax.ShapeDtypeStruct((batch * seq, W), F32),
        compiler_params=_params(("parallel",)),
        name="cross_attention_sample",
    )(bias, q, mk, mv)


def _mlp_kernel(x_ref, gpre_ref, wup_ref, wdn_ref, gpost_ref, o_ref, h_scr):
    j = pl.program_id(1)

    @pl.when(j == 0)
    def _():
        h_scr[...] = _rms(x_ref[...], gpre_ref[...]).astype(BF16)

        o_ref[...] = jnp.zeros_like(o_ref)

    u = jnp.maximum(jnp.dot(h_scr[...], wup_ref[...], preferred_element_type=F32), 0.0)
    o_ref[...] += jnp.dot((u * u).astype(BF16), wdn_ref[...], preferred_element_type=F32)

    @pl.when(j == pl.num_programs(1) - 1)
    def _():
        o_ref[...] = x_ref[...] + _rms(o_ref[...], gpost_ref[...])


def _mlp(x2d, g_pre, w_up_bf, w_dn_bf, g_post, tm, tf):
    T, D = x2d.shape
    FF = w_up_bf.shape[1]
    vec = pl.BlockSpec((1, D), lambda i, j: (0, 0))
    return pl.pallas_call(
        _mlp_kernel,
        grid=(T // tm, FF // tf),
        in_specs=[pl.BlockSpec((tm, D), lambda i, j: (i, 0)), vec,
                  pl.BlockSpec((D, tf), lambda i, j: (0, j)),
                  pl.BlockSpec((tf, D), lambda i, j: (j, 0)), vec],
        out_specs=pl.BlockSpec((tm, D), lambda i, j: (i, 0)),
        out_shape=jax.ShapeDtypeStruct((T, D), F32),
        scratch_shapes=[pltpu.VMEM((tm, D), BF16)],
        compiler_params=_params(("parallel", "arbitrary")),
        name="mlp",
    )(x2d, g_pre.reshape(1, D), w_up_bf, w_dn_bf, g_post.reshape(1, D))


def _position_tables(pos, tm):
    reps = max(1, tm // pos.shape[0])
    pos = jnp.tile(pos, reps)
    cr, srl, srh = _rope_tables(pos, RET_DK, RET_THETA)
    cd, sdl, sdh = _rope_tables(pos, ROPE_DIM, ROPE_THETA)
    return (cr, srl + srh, cd, sdl, sdh), pos.shape[0] // tm


def _layer_tail(x2d, mix_parts, wts, cross_fn, tm):
    (w_out_parts, g_mix_post, g_mem_pre, w_mem_q, w_mem_o, g_mem_post,
     g_mlp_pre, w_up, w_down, g_mlp_post) = wts
    x1 = _matmul_norm_residual(mix_parts, w_out_parts, g_mix_post, x2d, tm)
    q = cross_fn(x1, g_mem_pre, w_mem_q)
    x2 = _matmul_norm_residual([q], [w_mem_o], g_mem_post, x1, tm)
    return _mlp(x2, g_mlp_pre, w_up, w_down, g_mlp_post, tm, MLP_FF_TILE)


def _out_cross_kernel(mr_ref, md_ref, x_ref, wo1_ref, wo2_ref, g1_ref, gpre_ref, wq_ref, mk_ref, mv_ref,
                      wmo_ref, g2_ref, o_ref):
    z = (jnp.dot(mr_ref[...], wo1_ref[...], preferred_element_type=F32)
         + jnp.dot(md_ref[...], wo2_ref[...], preferred_element_type=F32))
    x1 = x_ref[...] + _rms(z, g1_ref[...])
    h = _rms(x1, gpre_ref[...]).astype(BF16)
    q = (jnp.dot(h, wq_ref[...], preferred_element_type=F32) * (MEM_DH ** -0.5)).astype(BF16)
    outs = []
    for hd in range(MEM_HEADS):
        sl = slice(hd * MEM_DH, (hd + 1) * MEM_DH)
        s = lax.dot_general(q[:, sl], mk_ref[0, :, sl], _NT, preferred_element_type=F32)
        p = _softmax_rows(s).astype(BF16)
        outs.append(jnp.dot(p, mv_ref[0, :, sl], preferred_element_type=F32).astype(BF16))
    y = jnp.dot(jnp.concatenate(outs, axis=1), wmo_ref[...], preferred_element_type=F32)
    o_ref[...] = x1 + _rms(y, g2_ref[...])


def _out_proj_cross_block(mix_parts, x2d, wts, mk_bf, mv_bf, batch, seq, tm):
    (w_out_parts, g_mix_post, g_mem_pre, w_mem_q, w_mem_o, g_mem_post) = wts[:6]
    T, D = x2d.shape
    n = seq // tm
    M, W = mk_bf.shape[1], mk_bf.shape[2]
    row = lambda width: pl.BlockSpec((tm, width), lambda b, i: (b * n + i, 0))
    const = lambda shape: pl.BlockSpec(shape, lambda b, i: (0,) * len(shape), pipeline_mode=pl.Buffered(1))
    kv = pl.BlockSpec((1, M, W), lambda b, i: (b, 0, 0))
    vec = lambda g: g.reshape(1, D)
    return pl.pallas_call(
        _out_cross_kernel,
        grid=(batch, n),
        in_specs=[row(mix_parts[0].shape[1]), row(mix_parts[1].shape[1]), row(D),
                  const(w_out_parts[0].shape), const(w_out_parts[1].shape), const((1, D)), const((1, D)),
                  const(w_mem_q.shape), kv, kv, const(w_mem_o.shape), const((1, D))],
        out_specs=row(D),
        out_shape=jax.ShapeDtypeStruct((T, D), F32),
        compiler_params=_params(("parallel", "parallel")),
        name="out_proj_cross_block",
    )(mix_parts[0], mix_parts[1], x2d, w_out_parts[0], w_out_parts[1], vec(g_mix_post), vec(g_mem_pre),
      w_mem_q, mk_bf, mv_bf, w_mem_o, vec(g_mem_post))


def kernel(x_prompt, x_sample, mem_prompt, state_ret, cache_diff_k, cache_diff_v, cache_mem_k, cache_mem_v, page_table, w_in, w_out, diff_lambda_q1, diff_lambda_k1, diff_lambda_q2, diff_lambda_k2, diff_subln_g, norm_mix_pre, norm_mix_post, norm_mem_pre, norm_mem_post, norm_mlp_pre, norm_mlp_post, mem_norm_g, w_mem_q, w_mem_k, w_mem_v, w_mem_o, w_mlp_up, w_mlp_down):
    depth = w_in.shape[0]
    B, L_p, D = x_prompt.shape
    B_s, L_s, _ = x_sample.shape
    n_pages, page = page_table.shape[1], cache_diff_k.shape[2]
    past_len = n_pages * page
    M = mem_prompt.shape[1]
    W_MEM = MEM_HEADS * MEM_DH
    TM = 512
    ret_chunk_p = math.gcd(L_p, RET_CHUNK)

    tabs_p, per_p = _position_tables(jnp.arange(L_p, dtype=F32), TM)
    TM_IN_S = 256
    tabs_s, per_s = _position_tables(past_len + jnp.arange(L_s, dtype=F32), TM_IN_S)

    yp = x_prompt.reshape(B * L_p, D)
    ys = x_sample.reshape(B_s * L_s, D)
    mem2d = mem_prompt.reshape(B * M, D)
    outs = {k: [] for k in ("rp", "kp", "vp", "mkp", "mvp", "rs", "ks", "vs")}

    for i in range(depth):
        lam_init = 0.8 - 0.6 * math.exp(-0.3 * i)
        lam_rows = jnp.stack([diff_lambda_q1[i], diff_lambda_k1[i], diff_lambda_q2[i], diff_lambda_k2[i]])
        w_in_bf = w_in[i].astype(BF16)
        half = w_out.shape[1] // 2
        tail_w = ([w_out[i, :half].astype(BF16), w_out[i, half:].astype(BF16)], norm_mix_post[i],
                  norm_mem_pre[i], w_mem_q[i].astype(BF16), w_mem_o[i].astype(BF16), norm_mem_post[i],
                  norm_mlp_pre[i], w_mlp_up[i].astype(BF16), w_mlp_down[i].astype(BF16), norm_mlp_post[i])

        mk_p = _norm_matmul(mem2d, mem_norm_g[i], w_mem_k[i].astype(BF16), B * M, F32)
        mv_p = _norm_matmul(mem2d, mem_norm_g[i], w_mem_v[i].astype(BF16), B * M, F32)
        rq, rk, rv, rg, dq_t, dk, dv, dv_t = _in_projection(
            yp, norm_mix_pre[i], w_in_bf, tabs_p, per_p, TM, BF16, DIFF_DK ** -0.5 * LOG2E, True)
        mix_ret, s_p = _retention_prompt(rq, rk, rv, rg, B, L_p, ret_chunk_p)
        mix_diff = _diff_attention_prompt(dq_t, dk, dv_t, lam_rows, diff_subln_g[i], B, L_p, lam_init)
        mk_bf = mk_p.astype(BF16).reshape(B, M, W_MEM)
        mv_bf = mv_p.astype(BF16).reshape(B, M, W_MEM)

        x2 = _out_proj_cross_block([mix_ret, mix_diff], yp, tail_w, mk_bf, mv_bf, B, L_p, TM)
        yp = _mlp(x2, norm_mlp_pre[i], tail_w[7], tail_w[8], norm_mlp_post[i], TM, MLP_FF_TILE)
        outs["rp"].append(s_p.astype(state_ret.dtype))
        outs["kp"].append(dk.reshape(B, L_p, DIFF_HEADS, 2, DIFF_DK))
        outs["vp"].append(dv.reshape(B, L_p, DIFF_HEADS, DIFF_DV))
        outs["mkp"].append(mk_p.reshape(B, M, MEM_HEADS, MEM_DH))
        outs["mvp"].append(mv_p.reshape(B, M, MEM_HEADS, MEM_DH))

        rq, rk, rv, rg, dq, dk, dv = _in_projection(ys, norm_mix_pre[i], w_in_bf, tabs_s, per_s, TM_IN_S, F32,
                                                    DIFF_DK ** -0.5, False)
        mix_ret, s_s = _retention_sample(rq, rk, rv, rg, state_ret, i, B_s, L_s, 4)
        mix_diff = _diff_attention_sample(dq, dk, dv, cache_diff_k, cache_diff_v, i, page_table,
                                          lam_rows, diff_subln_g[i], B_s, L_s, lam_init)
        def cross_s(x1, g, wq):
            q = _norm_matmul(x1, g, wq, TM, F32, scale=MEM_DH ** -0.5)
            return _cross_attention_sample(q, cache_mem_k, cache_mem_v, i, B_s, L_s, 8)

        ys = _layer_tail(ys, [mix_ret, mix_diff], tail_w, cross_s, TM)
        outs["rs"].append(s_s.astype(state_ret.dtype))
        outs["ks"].append(dk.reshape(B_s, L_s, DIFF_HEADS, 2, DIFF_DK))
        outs["vs"].append(dv.reshape(B_s, L_s, DIFF_HEADS, DIFF_DV))

    st = lambda k: outs[k][0][None] if depth == 1 else jnp.stack(outs[k])
    return (yp.reshape(B, L_p, D), ys.reshape(B_s, L_s, D), st("rp"), st("kp"), st("vp"),
            st("mkp"), st("mvp"), st("rs"), st("ks"), st("vs"))
```

```python
import functools
import math

import jax
import jax.numpy as jnp
from jax import lax
from jax.experimental import pallas as pl
from jax.experimental.pallas import tpu as pltpu

F32 = jnp.float32
BF16 = jnp.bfloat16

LANES = 128
MXU_COLS = 256
RET_HEADS = 8
RET_DK = 128
RET_DV = 128
RET_THETA = 10000.0
RET_CHUNK = 2 * 128
DIFF_HEADS = 4
DIFF_DK = 128
DIFF_DV = 256
ROPE_THETA = 500000.0
ROPE_DIM = DIFF_DK // 4
MEM_HEADS = 4
MEM_DH = 128
Q_BLOCK = 128
EPS = 1e-6
LOG2E = math.log2(math.e)
SEG = 1024
N_SEG = 7
MLP_FF_TILE = 1024
MLP_FF_TILE_FUSED = 512
PAGES_PER_STEP = 8

VMEM_LIMIT = 56 * 1024 * 1024

_NT = (((1,), (1,)), ((), ()))
_TN = (((0,), (0,)), ((), ()))


def _params(sem, vmem=VMEM_LIMIT):
    return pltpu.CompilerParams(dimension_semantics=sem, vmem_limit_bytes=vmem)


def _rms(x, g):
    return x * lax.rsqrt(jnp.mean(x * x, axis=-1, keepdims=True) + EPS) * g


def _unit_rms(o):
    return o * lax.rsqrt(jnp.mean(o * o, axis=-1, keepdims=True) + EPS)


def _rope_tables(pos, rot_dim, theta):
    half = rot_dim // 2
    inv = jnp.exp(-math.log(theta) * (2.0 * jnp.arange(half, dtype=F32) / rot_dim))
    ang = pos[:, None] * inv[None, :]
    cos, sin = jnp.cos(ang), jnp.sin(ang)
    n = pos.shape[0]
    rest = LANES - rot_dim
    c = jnp.concatenate([cos, cos, jnp.ones((n, rest), F32)], axis=-1)
    s_lo = jnp.concatenate([-sin, jnp.zeros((n, LANES - half), F32)], axis=-1)
    s_hi = jnp.concatenate([jnp.zeros((n, half), F32), sin, jnp.zeros((n, rest), F32)], axis=-1)
    return c, s_lo, s_hi


def _inproj_kernel(x_ref, g_ref, w_ref, cr_ref, sr_ref, cd_ref, sdl_ref, sdh_ref, *rest,
                   dq_scale, transposed):
    if transposed:
        rq_ref, rk_ref, rv_ref, rg_ref, dq_ref, dk_ref, dv_ref, dvt_ref, h_scr = rest
    else:
        rq_ref, rk_ref, rv_ref, rg_ref, dq_ref, dk_ref, dv_ref, h_scr = rest
    j = pl.program_id(1)
    heads = [slice(h * LANES, (h + 1) * LANES) for h in range(SEG // LANES)]

    @pl.when(j == 0)
    def _():
        h_scr[...] = _rms(x_ref[...], g_ref[...]).astype(BF16)

    def project():
        return jnp.dot(h_scr[...], w_ref[...], preferred_element_type=F32)

    def ret_rot(out_ref, scale):
        z_all = project()
        c, s = cr_ref[...], sr_ref[...]
        for sl in heads:
            z = z_all[:, sl]
            r = z * c + pltpu.roll(z, RET_DK // 2, 1) * s
            if scale != 1.0:
                r = r * scale
            out_ref[:, sl] = r.astype(out_ref.dtype)

    def diff_rot(z):
        half = ROPE_DIM // 2
        return (z * cd_ref[...] + pltpu.roll(z, LANES - half, 1) * sdl_ref[...]
                + pltpu.roll(z, half, 1) * sdh_ref[...])

    @pl.when(j == 0)
    def _():
        ret_rot(rq_ref, 1.0)

    @pl.when(j == 1)
    def _():
        ret_rot(rk_ref, RET_DK ** -0.5)

    @pl.when(j == 2)
    def _():
        rv_ref[...] = project().astype(rv_ref.dtype)

    @pl.when(j == 3)
    def _():
        rg_ref[...] = project()

    @pl.when(j == 4)
    def _():
        z_all = project()
        for sl in heads:
            r = diff_rot(z_all[:, sl]) * dq_scale
            if transposed:
                dq_ref[sl, :] = r.T.astype(dq_ref.dtype)
            else:
                dq_ref[:, sl] = r.astype(dq_ref.dtype)

    @pl.when(j == 5)
    def _():
        z_all = project()
        for sl in heads:
            dk_ref[:, sl] = diff_rot(z_all[:, sl])

    @pl.when(j == 6)
    def _():
        z_all = project()
        dv_ref[...] = z_all
        if transposed:
            for sl in heads:
                dvt_ref[sl, :] = z_all[:, sl].T.astype(dvt_ref.dtype)


def _in_projection(x2d, g, w_bf, tabs, period_blocks, tm, act_dtype, dq_scale, transposed):
    T, D = x2d.shape
    cr, sr, cd, sdl, sdh = tabs
    tab_spec = pl.BlockSpec((tm, LANES), lambda i, j: (i % period_blocks, 0))
    row_spec = pl.BlockSpec((tm, SEG), lambda i, j: (i, 0))
    col_spec = pl.BlockSpec((SEG, tm), lambda i, j: (0, i))
    sds = lambda dt: jax.ShapeDtypeStruct((T, SEG), dt)
    sds_t = lambda dt: jax.ShapeDtypeStruct((SEG, T), dt)
    out_specs = [row_spec] * 4 + [col_spec if transposed else row_spec, row_spec, row_spec]
    out_shape = [sds(act_dtype), sds(act_dtype), sds(act_dtype), sds(F32),
                 sds_t(act_dtype) if transposed else sds(act_dtype), sds(F32), sds(F32)]
    if transposed:
        out_specs.append(col_spec)
        out_shape.append(sds_t(act_dtype))
    return pl.pallas_call(
        functools.partial(_inproj_kernel, dq_scale=dq_scale, transposed=transposed),
        grid=(T // tm, N_SEG),
        in_specs=[
            pl.BlockSpec((tm, D), lambda i, j: (i, 0)),
            pl.BlockSpec((1, D), lambda i, j: (0, 0)),
            pl.BlockSpec((D, SEG), lambda i, j: (0, j)),
            tab_spec, tab_spec, tab_spec, tab_spec, tab_spec,
        ],
        out_specs=out_specs,
        out_shape=out_shape,
        scratch_shapes=[pltpu.VMEM((tm, D), BF16)],
        compiler_params=_params(("parallel", "arbitrary")),
        name="in_projection",
    )(x2d, g.reshape(1, D), w_bf, cr, sr, cd, sdl, sdh)


def _ret_decay_tables(chunk):
    lg = jnp.log1p(-jnp.exp2(-5.0 - jnp.arange(RET_HEADS, dtype=F32)))
    idx = jnp.arange(chunk, dtype=F32)
    rel = idx[:, None] - idx[None, :]
    dmat = jnp.where(rel[None] >= 0, jnp.exp(jnp.maximum(rel, 0.0)[None] * lg[:, None, None]), 0.0)
    q_decay = jnp.exp((idx + 1.0)[:, None] * lg[None, :])
    k_decay = jnp.exp((chunk - 1.0 - idx)[:, None] * lg[None, :])
    chunk_decay = jnp.exp(chunk * lg)
    widen = lambda t: jnp.repeat(t, RET_DK, axis=1)
    return dmat, widen(q_decay), widen(k_decay), chunk_decay


def _ret_head(q, k, v, s, dmat, qd, kd, cd, gate):
    qb, kb, vb = q.astype(BF16), k.astype(BF16), v.astype(BF16)
    att = lax.dot_general(qb, kb, _NT, preferred_element_type=F32) * dmat
    o = (jnp.dot(att.astype(BF16), vb, preferred_element_type=F32)
         + jnp.dot(qb, s.astype(BF16), preferred_element_type=F32) * qd)
    kdk = (k.astype(F32) * kd).astype(BF16)
    s_new = s * cd + lax.dot_general(kdk, vb, _TN, preferred_element_type=F32)
    y = _unit_rms(o) * (gate * jax.nn.sigmoid(gate))
    return y, s_new


def _ret_prompt_kernel(cd_ref, q_ref, k_ref, v_ref, g_ref, dmat_ref, qd_ref, kd_ref, mix_ref, s_ref):
    @pl.when(pl.program_id(1) == 0)
    def _():
        s_ref[...] = jnp.zeros_like(s_ref)

    for h in range(RET_HEADS):
        sl = slice(h * RET_DK, (h + 1) * RET_DK)
        y, s_new = _ret_head(q_ref[:, sl], k_ref[:, sl], v_ref[:, sl], s_ref[0, h], dmat_ref[h],
                             qd_ref[:, sl], kd_ref[:, sl], cd_ref[h], g_ref[:, sl])
        s_ref[0, h] = s_new
        mix_ref[:, sl] = y.astype(mix_ref.dtype)


def _retention_prompt(rq, rk, rv, rg, batch, seq, chunk):
    n = seq // chunk
    dmat, qd, kd, cd = _ret_decay_tables(chunk)
    row = pl.BlockSpec((chunk, SEG), lambda b, c: (b * n + c, 0))
    full2 = pl.BlockSpec((chunk, SEG), lambda b, c: (0, 0))
    return pl.pallas_call(
        _ret_prompt_kernel,
        grid=(batch, n),
        in_specs=[
            pl.BlockSpec(memory_space=pltpu.SMEM),
            row, row, row, row,
            pl.BlockSpec((RET_HEADS, chunk, chunk), lambda b, c: (0, 0, 0)),
            full2, full2,
        ],
        out_specs=[row, pl.BlockSpec((1, RET_HEADS, RET_DK, RET_DV), lambda b, c: (b, 0, 0, 0))],
        out_shape=[jax.ShapeDtypeStruct((batch * seq, SEG), BF16),
                   jax.ShapeDtypeStruct((batch, RET_HEADS, RET_DK, RET_DV), F32)],
        compiler_params=_params(("parallel", "arbitrary")),
        name="retention_prompt",
    )(cd, rq, rk, rv, rg, dmat, qd, kd)


def _ret_sample_kernel(cd_ref, q_ref, k_ref, v_ref, g_ref, s0_ref, dmat_ref, qd_ref, kd_ref,
                       mix_ref, s_ref, *, group, seq):
    rows = []
    for b in range(group):
        r = slice(b * seq, (b + 1) * seq)
        heads = []
        for h in range(RET_HEADS):
            sl = slice(h * RET_DK, (h + 1) * RET_DK)
            y, s_new = _ret_head(q_ref[r, sl], k_ref[r, sl], v_ref[r, sl], s0_ref[b, h], dmat_ref[h],
                                 qd_ref[:, sl], kd_ref[:, sl], cd_ref[h], g_ref[r, sl])
            s_ref[b, h] = s_new
            heads.append(y)
        rows.append(jnp.concatenate(heads, axis=1))
    mix_ref[...] = jnp.concatenate(rows, axis=0)


def _retention_sample(rq, rk, rv, rg, state, layer, batch, seq, group):
    dmat, qd, kd, cd = _ret_decay_tables(seq)
    row = pl.BlockSpec((group * seq, SEG), lambda i: (i, 0))
    tab = pl.BlockSpec((seq, SEG), lambda i: (0, 0))
    st = pl.BlockSpec((group, RET_HEADS, RET_DK, RET_DV), lambda i: (i, 0, 0, 0))
    st_in = pl.BlockSpec((None, group, RET_HEADS, RET_DK, RET_DV), lambda i: (layer, i, 0, 0, 0))
    return pl.pallas_call(
        functools.partial(_ret_sample_kernel, group=group, seq=seq),
        grid=(batch // group,),
        in_specs=[
            pl.BlockSpec(memory_space=pltpu.SMEM),
            row, row, row, row, st_in,
            pl.BlockSpec((RET_HEADS, seq, seq), lambda i: (0, 0, 0)),
            tab, tab,
        ],
        out_specs=[row, st],
        out_shape=[jax.ShapeDtypeStruct((batch * seq, SEG), F32),
                   jax.ShapeDtypeStruct((batch, RET_HEADS, RET_DK, RET_DV), F32)],
        compiler_params=_params(("parallel",)),
        name="retention_sample",
    )(cd, rq, rk, rv, rg, state, dmat, qd, kd)


def _lambda_value(lam_ref, lam_init):
    a = jnp.sum(lam_ref[0:1, :] * lam_ref[1:2, :], axis=-1, keepdims=True)
    b = jnp.sum(lam_ref[2:3, :] * lam_ref[3:4, :], axis=-1, keepdims=True)
    return jnp.exp(a) - jnp.exp(b) + lam_init


def _softmax_update(s, m_old, l_old):
    m_new = jnp.maximum(m_old, jnp.max(s, axis=-1, keepdims=True))
    alpha = jnp.exp(m_old - m_new)
    p = jnp.exp(s - m_new)
    l_new = alpha * l_old + jnp.sum(p, axis=-1, keepdims=True)
    return p, alpha, m_new, l_new


def _diff_prompt_kernel(qi_tab, ki_tab, lam_ref, qt_ref, k_ref, vt_ref, g_ref, o_ref, m_scr, l_scr, acc_scr,
                        *, tq, tk, lam_init):
    t = pl.program_id(1)
    qi, ki = qi_tab[t], ki_tab[t]

    @pl.when(ki == 0)
    def _():
        m_scr[...] = jnp.full_like(m_scr, -jnp.inf)
        l_scr[...] = jnp.zeros_like(l_scr)
        acc_scr[...] = jnp.zeros_like(acc_scr)

    def step(masked):
        if masked:
            kpos = ki * tk + lax.broadcasted_iota(jnp.int32, (tk, tq), 0)
            qpos = qi * tq + lax.broadcasted_iota(jnp.int32, (tk, tq), 1)
            keep = qpos >= kpos
        for h in range(DIFF_HEADS):
            vt = vt_ref[h * DIFF_DV:(h + 1) * DIFF_DV, :]
            for m in range(2):
                i = 2 * h + m
                sl = slice(i * DIFF_DK, (i + 1) * DIFF_DK)
                kb = k_ref[:, sl].astype(BF16)
                for c in range(tq // MXU_COLS):
                    cs = slice(c * MXU_COLS, (c + 1) * MXU_COLS)
                    st = jnp.dot(kb, qt_ref[sl, cs], preferred_element_type=F32)
                    if masked:
                        st = jnp.where(keep[:, cs], st, -jnp.inf)
                    m_old = m_scr[i, :, cs]
                    m_new = jnp.maximum(m_old, jnp.max(st, axis=0, keepdims=True))
                    alpha = jnp.exp2(m_old - m_new)
                    p = jnp.exp2(st - m_new)
                    l_scr[i, :, cs] = alpha * l_scr[i, :, cs] + jnp.sum(p, axis=0, keepdims=True)
                    m_scr[i, :, cs] = m_new
                    acc_scr[i, :, cs] = (alpha * acc_scr[i, :, cs]
                                         + jnp.dot(vt, p.astype(BF16), preferred_element_type=F32))

    @pl.when(ki < qi)
    def _():
        step(False)

    @pl.when(ki == qi)
    def _():
        step(True)
        lam = _lambda_value(lam_ref, lam_init)
        for h in range(DIFF_HEADS):
            a, b = 2 * h, 2 * h + 1
            ot = acc_scr[a] * (1.0 / l_scr[a]) - lam * (acc_scr[b] * (1.0 / l_scr[b]))
            yt = ot * lax.rsqrt(jnp.mean(ot * ot, axis=0, keepdims=True) + EPS)
            o_ref[:, h * DIFF_DV:(h + 1) * DIFF_DV] = (yt.T * g_ref[...] * (1.0 - lam_init)).astype(o_ref.dtype)


def _diff_attention_prompt(dq_t, dk, dv_t, lam_rows, subln_g, batch, seq, lam_init, tq=512, tk=512):
    assert tq == tk
    nq = seq // tq
    pairs = [(qi, ki) for qi in range(nq) for ki in range(qi + 1)]
    qi_tab = jnp.asarray([p[0] for p in pairs], jnp.int32)
    ki_tab = jnp.asarray([p[1] for p in pairs], jnp.int32)
    n_sub = 2 * DIFF_HEADS
    grid_spec = pltpu.PrefetchScalarGridSpec(
        num_scalar_prefetch=2,
        grid=(batch, len(pairs)),
        in_specs=[
            pl.BlockSpec((4, DIFF_DK), lambda b, t, qt, kt: (0, 0)),
            pl.BlockSpec((SEG, tq), lambda b, t, qt, kt: (0, b * nq + qt[t])),
            pl.BlockSpec((tk, SEG), lambda b, t, qt, kt: (b * nq + kt[t], 0)),
            pl.BlockSpec((SEG, tk), lambda b, t, qt, kt: (0, b * nq + kt[t])),
            pl.BlockSpec((1, DIFF_DV), lambda b, t, qt, kt: (0, 0)),
        ],
        out_specs=pl.BlockSpec((tq, SEG), lambda b, t, qt, kt: (b * nq + qt[t], 0)),
        scratch_shapes=[pltpu.VMEM((n_sub, 1, tq), F32), pltpu.VMEM((n_sub, 1, tq), F32),
                        pltpu.VMEM((n_sub, DIFF_DV, tq), F32)],
    )
    return pl.pallas_call(
        functools.partial(_diff_prompt_kernel, tq=tq, tk=tk, lam_init=lam_init),
        grid_spec=grid_spec,
        out_shape=jax.ShapeDtypeStruct((batch * seq, SEG), BF16),
        compiler_params=_params(("parallel", "arbitrary")),
        name="diff_attention_prompt",
    )(qi_tab, ki_tab, lam_rows, dq_t, dk, dv_t, subln_g.reshape(1, DIFF_DV))


def _head_match_bias(n_rows, seq, n_cols, heads):
    row_h = jnp.arange(n_rows, dtype=jnp.int32)[:, None] // seq
    col_h = jnp.arange(n_cols, dtype=jnp.int32)[None, :] % heads
    return jnp.where(row_h == col_h, 0.0, -jnp.inf).astype(F32)


def _stack_heads(x, width, offset, stride):
    return jnp.concatenate(
        [x[:, offset + h * stride: offset + h * stride + width] for h in range(DIFF_HEADS)], axis=0)


def _stacked_queries(q_ref):
    q = q_ref[...]
    return [_stack_heads(q, DIFF_DK, m * DIFF_DK, 2 * DIFF_DK).astype(BF16) for m in range(2)]


def _diff_sample_begin(m_scr, l_scr, acc_scr):
    m_scr[...] = jnp.full_like(m_scr, -jnp.inf)
    l_scr[...] = jnp.zeros_like(l_scr)
    acc_scr[...] = jnp.zeros_like(acc_scr)


def _diff_sample_pages(bias_ref, q_ref, k_refs, v_refs, m_scr, l_scr, acc_scr, *, seq):
    rows = DIFF_HEADS * seq
    kv_rows = k_refs[0].shape[0] // 2
    qs = _stacked_queries(q_ref)
    bias = bias_ref[...]
    ps, alphas = [], []
    for m in range(2):
        s = jnp.concatenate(
            [lax.dot_general(qs[m], kr[pl.ds(m, kv_rows, stride=2), :].astype(BF16), _NT,
                             preferred_element_type=F32) + bias for kr in k_refs], axis=1)
        p, alpha, m_new, l_new = _softmax_update(s, m_scr[m], l_scr[m])
        m_scr[m] = m_new
        l_scr[m] = l_new
        ps.append(p)
        alphas.append(alpha)
    p = jnp.concatenate(ps, axis=0).astype(BF16)
    for e in range(2):
        pv = None
        for n in range(len(v_refs)):
            v = v_refs[n][:, :, e * LANES:(e + 1) * LANES].reshape(kv_rows, LANES).astype(BF16)
            t = jnp.dot(p[:, n * kv_rows:(n + 1) * kv_rows], v, preferred_element_type=F32)
            pv = t if pv is None else pv + t
        for m in range(2):
            cols = slice(e * LANES, (e + 1) * LANES)
            acc_scr[m, :, cols] = alphas[m] * acc_scr[m, :, cols] + pv[m * rows:(m + 1) * rows]


def _diff_sample_finish(lam_ref, q_ref, kn_ref, vn_ref, g_ref, o_ref, m_scr, l_scr, acc_scr, *, seq, lam_init):
    qs = _stacked_queries(q_ref)
    kn, vn = kn_ref[...], vn_ref[...]
    causal = (lax.broadcasted_iota(jnp.int32, (seq, seq), 0)
              >= lax.broadcasted_iota(jnp.int32, (seq, seq), 1))
    accs, ls = [], []
    for m in range(2):
        s = jnp.concatenate(
            [jnp.where(causal,
                       lax.dot_general(qs[m][h * seq:(h + 1) * seq],
                                       kn[:, (2 * h + m) * DIFF_DK:(2 * h + m + 1) * DIFF_DK].astype(BF16),
                                       _NT, preferred_element_type=F32),
                       -jnp.inf) for h in range(DIFF_HEADS)], axis=0)
        p, alpha, _, l_new = _softmax_update(s, m_scr[m], l_scr[m])
        pn = p.astype(BF16)
        pv = jnp.concatenate(
            [jnp.dot(pn[h * seq:(h + 1) * seq], vn[:, h * DIFF_DV:(h + 1) * DIFF_DV].astype(BF16),
                     preferred_element_type=F32) for h in range(DIFF_HEADS)], axis=0)
        accs.append(alpha * acc_scr[m] + pv)
        ls.append(l_new)
    lam = _lambda_value(lam_ref, lam_init)
    o = accs[0] / ls[0] - lam * (accs[1] / ls[1])
    y = _unit_rms(o) * g_ref[...] * (1.0 - lam_init)
    o_ref[...] = jnp.concatenate([y[h * seq:(h + 1) * seq] for h in range(DIFF_HEADS)], axis=1)


def _mlp_and_sample_attention_kernel(pt_ref, x_ref, gpre_ref, wup_ref, wdn_ref, gpost_ref,
                                     lam_ref, bias_ref, q_ref, kn_ref, vn_ref, g_ref, *rest,
                                     n_pages_step, attn_steps, seq, lam_init):
    k_refs = rest[:n_pages_step]
    v_refs = rest[n_pages_step:2 * n_pages_step]
    y_ref, o_ref, h_scr, m_scr, l_scr, acc_scr = rest[2 * n_pages_step:]
    j, n_j = pl.program_id(1), pl.num_programs(1)
    step = lax.rem(pl.program_id(0) * n_j + j, attn_steps)
    pl.when(j == 0)(lambda: _mlp_begin(x_ref, gpre_ref, y_ref, h_scr))
    pl.when(step == 0)(lambda: _diff_sample_begin(m_scr, l_scr, acc_scr))
    _mlp_tile(wup_ref, wdn_ref, y_ref, h_scr)
    _diff_sample_pages(bias_ref, q_ref, k_refs, v_refs, m_scr, l_scr, acc_scr, seq=seq)
    pl.when(j == n_j - 1)(lambda: _mlp_finish(x_ref, gpost_ref, y_ref))
    pl.when(step == attn_steps - 1)(lambda: _diff_sample_finish(
        lam_ref, q_ref, kn_ref, vn_ref, g_ref, o_ref, m_scr, l_scr, acc_scr, seq=seq, lam_init=lam_init))


def _mlp_and_sample_attention(x2d, g_pre, w_up_bf, w_dn_bf, g_post, tm, tf,
                              dq, dk_new, dv_new, cache_k, cache_v, layer, page_table, lam_rows, subln_g,
                              batch, seq, lam_init, pages_per_step):
    T, D = x2d.shape
    FF = w_up_bf.shape[1]
    n_i, n_j = T // tm, FF // tf
    depth, n_pool, page = cache_k.shape[:3]
    n_pages = page_table.shape[1]
    attn_steps = n_pages // pages_per_step
    assert n_pages % pages_per_step == 0 and DIFF_DV == 2 * LANES
    assert n_i * n_j == batch * attn_steps, "the two jobs must have the same number of grid steps"
    k_rows = page * DIFF_HEADS * 2
    ck = cache_k.reshape(depth, n_pool, k_rows, DIFF_DK)
    rows = DIFF_HEADS * seq
    bias = _head_match_bias(rows, seq, page * DIFF_HEADS, DIFF_HEADS)

    def seq_of(i, j):
        return (i * n_j + j) // attn_steps

    def page_of(i, j, pt, n):
        g = i * n_j + j
        return pt[g // attn_steps, (g % attn_steps) * pages_per_step + n]

    vec = pl.BlockSpec((1, D), lambda i, j, pt: (0, 0))
    row = pl.BlockSpec((seq, SEG), lambda i, j, pt: (seq_of(i, j), 0))
    k_spec = lambda n: pl.BlockSpec((None, None, k_rows, DIFF_DK),
                                    lambda i, j, pt: (layer, page_of(i, j, pt, n), 0, 0))
    v_spec = lambda n: pl.BlockSpec((None, None, page, DIFF_HEADS, DIFF_DV),
                                    lambda i, j, pt: (layer, page_of(i, j, pt, n), 0, 0, 0))
    grid_spec = pltpu.PrefetchScalarGridSpec(
        num_scalar_prefetch=1,
        grid=(n_i, n_j),
        in_specs=[pl.BlockSpec((tm, D), lambda i, j, pt: (i, 0)), vec,
                  pl.BlockSpec((D, tf), lambda i, j, pt: (0, j)),
                  pl.BlockSpec((tf, D), lambda i, j, pt: (j, 0)), vec,
                  pl.BlockSpec((4, DIFF_DK), lambda i, j, pt: (0, 0)),
                  pl.BlockSpec(bias.shape, lambda i, j, pt: (0, 0)),
                  row, row, row,
                  pl.BlockSpec((1, DIFF_DV), lambda i, j, pt: (0, 0))]
                 + [k_spec(n) for n in range(pages_per_step)] + [v_spec(n) for n in range(pages_per_step)],
        out_specs=[pl.BlockSpec((tm, D), lambda i, j, pt: (i, 0)), row],
        scratch_shapes=[pltpu.VMEM((tm, D), BF16),
                        pltpu.VMEM((2, rows, 1), F32), pltpu.VMEM((2, rows, 1), F32),
                        pltpu.VMEM((2, rows, DIFF_DV), F32)],
    )
    return pl.pallas_call(
        functools.partial(_mlp_and_sample_attention_kernel, n_pages_step=pages_per_step,
                          attn_steps=attn_steps, seq=seq, lam_init=lam_init),
        grid_spec=grid_spec,
        out_shape=[jax.ShapeDtypeStruct((T, D), F32), jax.ShapeDtypeStruct((batch * seq, SEG), F32)],
        compiler_params=_params(("arbitrary", "arbitrary")),
        name="mlp_and_sample_attention",
    )(page_table, x2d, g_pre.reshape(1, D), w_up_bf, w_dn_bf, g_post.reshape(1, D),
      lam_rows, bias, dq, dk_new, dv_new, subln_g.reshape(1, DIFF_DV),
      *([ck] * pages_per_step), *([cache_v] * pages_per_step))


def _norm_matmul_kernel(x_ref, g_ref, w_ref, o_ref, *, scale):
    h = _rms(x_ref[...], g_ref[...]).astype(BF16)
    z = jnp.dot(h, w_ref[...], preferred_element_type=F32)
    if scale != 1.0:
        z = z * scale
    o_ref[...] = z.astype(o_ref.dtype)


def _norm_matmul(x2d, g, w_bf, tm, out_dtype, scale=1.0):
    T, D = x2d.shape
    N = w_bf.shape[1]
    return pl.pallas_call(
        functools.partial(_norm_matmul_kernel, scale=scale),
        grid=(T // tm,),
        in_specs=[pl.BlockSpec((tm, D), lambda i: (i, 0)),
                  pl.BlockSpec((1, D), lambda i: (0, 0)),
                  pl.BlockSpec((D, N), lambda i: (0, 0))],
        out_specs=pl.BlockSpec((tm, N), lambda i: (i, 0)),
        out_shape=jax.ShapeDtypeStruct((T, N), out_dtype),
        compiler_params=_params(("parallel",)),
        name="norm_matmul",
    )(x2d, g.reshape(1, D), w_bf)


def _matmul_norm_res_kernel(*refs, n_in):
    a_refs = refs[:n_in]
    w_refs = refs[n_in:2 * n_in]
    g_ref, res_ref, o_ref = refs[2 * n_in:]
    z = None
    for a_ref, w_ref in zip(a_refs, w_refs):
        t = jnp.dot(a_ref[...].astype(BF16), w_ref[...], preferred_element_type=F32)
        z = t if z is None else z + t
    o_ref[...] = res_ref[...] + _rms(z, g_ref[...])


def _matmul_norm_residual(acts, ws_bf, g, res, tm):
    T, D = res.shape
    n_in = len(acts)
    in_specs = ([pl.BlockSpec((tm, a.shape[1]), lambda i: (i, 0)) for a in acts]
                + [pl.BlockSpec(w.shape, lambda i: (0, 0)) for w in ws_bf]
                + [pl.BlockSpec((1, D), lambda i: (0, 0)), pl.BlockSpec((tm, D), lambda i: (i, 0))])
    return pl.pallas_call(
        functools.partial(_matmul_norm_res_kernel, n_in=n_in),
        grid=(T // tm,),
        in_specs=in_specs,
        out_specs=pl.BlockSpec((tm, D), lambda i: (i, 0)),
        out_shape=jax.ShapeDtypeStruct((T, D), F32),
        compiler_params=_params(("parallel",)),
        name="matmul_norm_residual",
    )(*acts, *ws_bf, g.reshape(1, D), res)


def _softmax_rows(s):
    e = jnp.exp(s - jnp.max(s, axis=-1, keepdims=True))
    return e / jnp.sum(e, axis=-1, keepdims=True)


def _cross_sample_kernel(bias_ref, q_ref, mk_ref, mv_ref, o_ref, *, group, seq):
    bias = bias_ref[...]
    outs = []
    for b in range(group):
        qb = q_ref[b * seq:(b + 1) * seq, :]
        qh = jnp.concatenate([qb[:, h * MEM_DH:(h + 1) * MEM_DH] for h in range(MEM_HEADS)], axis=0)
        s = lax.dot_general(qh.astype(BF16), mk_ref[b].astype(BF16), _NT, preferred_element_type=F32) + bias
        p = _softmax_rows(s).astype(BF16)
        o = jnp.dot(p, mv_ref[b].astype(BF16), preferred_element_type=F32)
        outs.append(jnp.concatenate([o[h * seq:(h + 1) * seq] for h in range(MEM_HEADS)], axis=1))
    o_ref[...] = jnp.concatenate(outs, axis=0)


def _cross_attention_sample(q, mem_k, mem_v, layer, batch, seq, group):
    depth, _, M = mem_k.shape[:3]
    W = MEM_HEADS * MEM_DH
    mk = mem_k.reshape(depth, batch, M * MEM_HEADS, MEM_DH)
    mv = mem_v.reshape(depth, batch, M * MEM_HEADS, MEM_DH)
    bias = _head_match_bias(MEM_HEADS * seq, seq, M * MEM_HEADS, MEM_HEADS)
    kv = pl.BlockSpec((None, group, M * MEM_HEADS, MEM_DH), lambda i: (layer, i, 0, 0))
    row = pl.BlockSpec((group * seq, W), lambda i: (i, 0))
    return pl.pallas_call(
        functools.partial(_cross_sample_kernel, group=group, seq=seq),
        grid=(batch // group,),
        in_specs=[pl.BlockSpec(bias.shape, lambda i: (0, 0)), row, kv, kv],
        out_specs=row,
        out_shape=jax.ShapeDtypeStruct((batch * seq, W), F32),
        compiler_params=_params(("parallel",)),
        name="cross_attention_sample",
    )(bias, q, mk, mv)


def _mlp_begin(x_ref, gpre_ref, o_ref, h_scr):
    h_scr[...] = _rms(x_ref[...], gpre_ref[...]).astype(BF16)
    o_ref[...] = jnp.zeros_like(o_ref)


def _mlp_tile(wup_ref, wdn_ref, o_ref, h_scr):
    u = jnp.maximum(jnp.dot(h_scr[...], wup_ref[...], preferred_element_type=F32), 0.0)
    o_ref[...] += jnp.dot((u * u).astype(BF16), wdn_ref[...], preferred_element_type=F32)


def _mlp_finish(x_ref, gpost_ref, o_ref):
    o_ref[...] = x_ref[...] + _rms(o_ref[...], gpost_ref[...])


def _mlp_kernel(x_ref, gpre_ref, wup_ref, wdn_ref, gpost_ref, o_ref, h_scr):
    j = pl.program_id(1)
    pl.when(j == 0)(lambda: _mlp_begin(x_ref, gpre_ref, o_ref, h_scr))
    _mlp_tile(wup_ref, wdn_ref, o_ref, h_scr)
    pl.when(j == pl.num_programs(1) - 1)(lambda: _mlp_finish(x_ref, gpost_ref, o_ref))


def _mlp(x2d, g_pre, w_up_bf, w_dn_bf, g_post, tm, tf):
    T, D = x2d.shape
    FF = w_up_bf.shape[1]
    vec = pl.BlockSpec((1, D), lambda i, j: (0, 0))
    return pl.pallas_call(
        _mlp_kernel,
        grid=(T // tm, FF // tf),
        in_specs=[pl.BlockSpec((tm, D), lambda i, j: (i, 0)), vec,
                  pl.BlockSpec((D, tf), lambda i, j: (0, j)),
                  pl.BlockSpec((tf, D), lambda i, j: (j, 0)), vec],
        out_specs=pl.BlockSpec((tm, D), lambda i, j: (i, 0)),
        out_shape=jax.ShapeDtypeStruct((T, D), F32),
        scratch_shapes=[pltpu.VMEM((tm, D), BF16)],
        compiler_params=_params(("parallel", "arbitrary")),
        name="mlp",
    )(x2d, g_pre.reshape(1, D), w_up_bf, w_dn_bf, g_post.reshape(1, D))


def _position_tables(pos, tm):
    reps = max(1, tm // pos.shape[0])
    pos = jnp.tile(pos, reps)
    cr, srl, srh = _rope_tables(pos, RET_DK, RET_THETA)
    cd, sdl, sdh = _rope_tables(pos, ROPE_DIM, ROPE_THETA)
    return (cr, srl + srh, cd, sdl, sdh), pos.shape[0] // tm


def _layer_tail(x2d, mix_parts, wts, cross_fn, tm):
    (w_out_parts, g_mix_post, g_mem_pre, w_mem_q, w_mem_o, g_mem_post,
     g_mlp_pre, w_up, w_down, g_mlp_post) = wts
    x1 = _matmul_norm_residual(mix_parts, w_out_parts, g_mix_post, x2d, tm)
    q = cross_fn(x1, g_mem_pre, w_mem_q)
    x2 = _matmul_norm_residual([q], [w_mem_o], g_mem_post, x1, tm)
    return _mlp(x2, g_mlp_pre, w_up, w_down, g_mlp_post, tm, MLP_FF_TILE)


def _out_cross_kernel(mr_ref, md_ref, x_ref, wo1_ref, wo2_ref, g1_ref, gpre_ref, wq_ref, mk_ref, mv_ref,
                      wmo_ref, g2_ref, o_ref):
    z = (jnp.dot(mr_ref[...], wo1_ref[...], preferred_element_type=F32)
         + jnp.dot(md_ref[...], wo2_ref[...], preferred_element_type=F32))
    x1 = x_ref[...] + _rms(z, g1_ref[...])
    h = _rms(x1, gpre_ref[...]).astype(BF16)
    q = (jnp.dot(h, wq_ref[...], preferred_element_type=F32) * (MEM_DH ** -0.5)).astype(BF16)
    outs = []
    for hd in range(MEM_HEADS):
        sl = slice(hd * MEM_DH, (hd + 1) * MEM_DH)
        s = lax.dot_general(q[:, sl], mk_ref[0, :, sl], _NT, preferred_element_type=F32)
        p = _softmax_rows(s).astype(BF16)
        outs.append(jnp.dot(p, mv_ref[0, :, sl], preferred_element_type=F32).astype(BF16))
    y = jnp.dot(jnp.concatenate(outs, axis=1), wmo_ref[...], preferred_element_type=F32)
    o_ref[...] = x1 + _rms(y, g2_ref[...])


def _out_proj_cross_block(mix_parts, x2d, wts, mk_bf, mv_bf, batch, seq, tm):
    (w_out_parts, g_mix_post, g_mem_pre, w_mem_q, w_mem_o, g_mem_post) = wts[:6]
    T, D = x2d.shape
    n = seq // tm
    M, W = mk_bf.shape[1], mk_bf.shape[2]
    row = lambda width: pl.BlockSpec((tm, width), lambda b, i: (b * n + i, 0))
    const = lambda shape: pl.BlockSpec(shape, lambda b, i: (0,) * len(shape), pipeline_mode=pl.Buffered(1))
    kv = pl.BlockSpec((1, M, W), lambda b, i: (b, 0, 0))
    vec = lambda g: g.reshape(1, D)
    return pl.pallas_call(
        _out_cross_kernel,
        grid=(batch, n),
        in_specs=[row(mix_parts[0].shape[1]), row(mix_parts[1].shape[1]), row(D),
                  const(w_out_parts[0].shape), const(w_out_parts[1].shape), const((1, D)), const((1, D)),
                  const(w_mem_q.shape), kv, kv, const(w_mem_o.shape), const((1, D))],
        out_specs=row(D),
        out_shape=jax.ShapeDtypeStruct((T, D), F32),
        compiler_params=_params(("parallel", "parallel")),
        name="out_proj_cross_block",
    )(mix_parts[0], mix_parts[1], x2d, w_out_parts[0], w_out_parts[1], vec(g_mix_post), vec(g_mem_pre),
      w_mem_q, mk_bf, mv_bf, w_mem_o, vec(g_mem_post))


def kernel(x_prompt, x_sample, mem_prompt, state_ret, cache_diff_k, cache_diff_v, cache_mem_k, cache_mem_v, page_table, w_in, w_out, diff_lambda_q1, diff_lambda_k1, diff_lambda_q2, diff_lambda_k2, diff_subln_g, norm_mix_pre, norm_mix_post, norm_mem_pre, norm_mem_post, norm_mlp_pre, norm_mlp_post, mem_norm_g, w_mem_q, w_mem_k, w_mem_v, w_mem_o, w_mlp_up, w_mlp_down):
    depth = w_in.shape[0]
    B, L_p, D = x_prompt.shape
    B_s, L_s, _ = x_sample.shape
    n_pages, page = page_table.shape[1], cache_diff_k.shape[2]
    past_len = n_pages * page
    M = mem_prompt.shape[1]
    W_MEM = MEM_HEADS * MEM_DH
    TM = 512
    ret_chunk_p = math.gcd(L_p, RET_CHUNK)

    tabs_p, per_p = _position_tables(jnp.arange(L_p, dtype=F32), TM)
    TM_IN_S = 256
    tabs_s, per_s = _position_tables(past_len + jnp.arange(L_s, dtype=F32), TM_IN_S)

    yp = x_prompt.reshape(B * L_p, D)
    ys = x_sample.reshape(B_s * L_s, D)
    mem2d = mem_prompt.reshape(B * M, D)
    outs = {k: [] for k in ("rp", "kp", "vp", "mkp", "mvp", "rs", "ks", "vs")}

    for i in range(depth):
        lam_init = 0.8 - 0.6 * math.exp(-0.3 * i)
        lam_rows = jnp.stack([diff_lambda_q1[i], diff_lambda_k1[i], diff_lambda_q2[i], diff_lambda_k2[i]])
        w_in_bf = w_in[i].astype(BF16)
        half = w_out.shape[1] // 2
        tail_w = ([w_out[i, :half].astype(BF16), w_out[i, half:].astype(BF16)], norm_mix_post[i],
                  norm_mem_pre[i], w_mem_q[i].astype(BF16), w_mem_o[i].astype(BF16), norm_mem_post[i],
                  norm_mlp_pre[i], w_mlp_up[i].astype(BF16), w_mlp_down[i].astype(BF16), norm_mlp_post[i])

        mk_p = _norm_matmul(mem2d, mem_norm_g[i], w_mem_k[i].astype(BF16), B * M, F32)
        mv_p = _norm_matmul(mem2d, mem_norm_g[i], w_mem_v[i].astype(BF16), B * M, F32)
        rq, rk, rv, rg, dq_t, dk, dv, dv_t = _in_projection(
            yp, norm_mix_pre[i], w_in_bf, tabs_p, per_p, TM, BF16, DIFF_DK ** -0.5 * LOG2E, True)
        mix_ret, s_p = _retention_prompt(rq, rk, rv, rg, B, L_p, ret_chunk_p)
        mix_diff = _diff_attention_prompt(dq_t, dk, dv_t, lam_rows, diff_subln_g[i], B, L_p, lam_init)
        mk_bf = mk_p.astype(BF16).reshape(B, M, W_MEM)
        mv_bf = mv_p.astype(BF16).reshape(B, M, W_MEM)

        x2 = _out_proj_cross_block([mix_ret, mix_diff], yp, tail_w, mk_bf, mv_bf, B, L_p, TM)
        outs["rp"].append(s_p.astype(state_ret.dtype))
        outs["kp"].append(dk.reshape(B, L_p, DIFF_HEADS, 2, DIFF_DK))
        outs["vp"].append(dv.reshape(B, L_p, DIFF_HEADS, DIFF_DV))
        outs["mkp"].append(mk_p.reshape(B, M, MEM_HEADS, MEM_DH))
        outs["mvp"].append(mv_p.reshape(B, M, MEM_HEADS, MEM_DH))

        rq, rk, rv, rg, dq, dk, dv = _in_projection(ys, norm_mix_pre[i], w_in_bf, tabs_s, per_s, TM_IN_S, F32,
                                                    DIFF_DK ** -0.5, False)
        mix_ret, s_s = _retention_sample(rq, rk, rv, rg, state_ret, i, B_s, L_s, 4)
        yp, mix_diff = _mlp_and_sample_attention(
            x2, norm_mlp_pre[i], tail_w[7], tail_w[8], norm_mlp_post[i], TM, MLP_FF_TILE_FUSED,
            dq, dk, dv, cache_diff_k, cache_diff_v, i, page_table, lam_rows, diff_subln_g[i],
            B_s, L_s, lam_init, PAGES_PER_STEP)
        def cross_s(x1, g, wq):
            q = _norm_matmul(x1, g, wq, TM, F32, scale=MEM_DH ** -0.5)
            return _cross_attention_sample(q, cache_mem_k, cache_mem_v, i, B_s, L_s, 8)

        ys = _layer_tail(ys, [mix_ret, mix_diff], tail_w, cross_s, TM)
        outs["rs"].append(s_s.astype(state_ret.dtype))
        outs["ks"].append(dk.reshape(B_s, L_s, DIFF_HEADS, 2, DIFF_DK))
        outs["vs"].append(dv.reshape(B_s, L_s, DIFF_HEADS, DIFF_DV))

    st = lambda k: outs[k][0][None] if depth == 1 else jnp.stack(outs[k])
    return (yp.reshape(B, L_p, D), ys.reshape(B_s, L_s, D), st("rp"), st("kp"), st("vp"),
            st("mkp"), st("mvp"), st("rs"), st("ks"), st("vs"))
```

```python
import functools
import math

import jax
import jax.numpy as jnp
from jax import lax
from jax.experimental import pallas as pl
from jax.experimental.pallas import tpu as pltpu

F32 = jnp.float32
BF16 = jnp.bfloat16

LANES = 128
MXU_COLS = 256
RET_HEADS = 8
RET_DK = 128
RET_DV = 128
RET_THETA = 10000.0
RET_CHUNK = 2 * 128
DIFF_HEADS = 4
DIFF_DK = 128
DIFF_DV = 256
ROPE_THETA = 500000.0
ROPE_DIM = DIFF_DK // 4
MEM_HEADS = 4
MEM_DH = 128
Q_BLOCK = 128
EPS = 1e-6
LOG2E = math.log2(math.e)
SEG = 1024
N_SEG = 7
MLP_FF_TILE = 1024
MLP_FF_TILE_FUSED = 512
PAGES_PER_STEP = 8
PAGE_DMA_PRIORITY = 1

VMEM_LIMIT = 56 * 1024 * 1024

_NT = (((1,), (1,)), ((), ()))
_TN = (((0,), (0,)), ((), ()))


def _params(sem, vmem=VMEM_LIMIT):
    return pltpu.CompilerParams(dimension_semantics=sem, vmem_limit_bytes=vmem)


def _rms(x, g):
    return x * lax.rsqrt(jnp.mean(x * x, axis=-1, keepdims=True) + EPS) * g


def _unit_rms(o):
    return o * lax.rsqrt(jnp.mean(o * o, axis=-1, keepdims=True) + EPS)


def _rope_tables(pos, rot_dim, theta):
    half = rot_dim // 2
    inv = jnp.exp(-math.log(theta) * (2.0 * jnp.arange(half, dtype=F32) / rot_dim))
    ang = pos[:, None] * inv[None, :]
    cos, sin = jnp.cos(ang), jnp.sin(ang)
    n = pos.shape[0]
    rest = LANES - rot_dim
    c = jnp.concatenate([cos, cos, jnp.ones((n, rest), F32)], axis=-1)
    s_lo = jnp.concatenate([-sin, jnp.zeros((n, LANES - half), F32)], axis=-1)
    s_hi = jnp.concatenate([jnp.zeros((n, half), F32), sin, jnp.zeros((n, rest), F32)], axis=-1)
    return c, s_lo, s_hi


def _inproj_kernel(x_ref, g_ref, w_ref, cr_ref, sr_ref, cd_ref, sdl_ref, sdh_ref, *rest,
                   dq_scale, transposed):
    if transposed:
        rq_ref, rk_ref, rv_ref, rg_ref, dq_ref, dk_ref, dv_ref, dvt_ref, h_scr = rest
    else:
        rq_ref, rk_ref, rv_ref, rg_ref, dq_ref, dk_ref, dv_ref, h_scr = rest
    j = pl.program_id(1)
    heads = [slice(h * LANES, (h + 1) * LANES) for h in range(SEG // LANES)]

    @pl.when(j == 0)
    def _():
        h_scr[...] = _rms(x_ref[...], g_ref[...]).astype(BF16)

    def project():
        return jnp.dot(h_scr[...], w_ref[...], preferred_element_type=F32)

    def ret_rot(out_ref, scale):
        z_all = project()
        c, s = cr_ref[...], sr_ref[...]
        for sl in heads:
            z = z_all[:, sl]
            r = z * c + pltpu.roll(z, RET_DK // 2, 1) * s
            if scale != 1.0:
                r = r * scale
            out_ref[:, sl] = r.astype(out_ref.dtype)

    def diff_rot(z):
        half = ROPE_DIM // 2
        return (z * cd_ref[...] + pltpu.roll(z, LANES - half, 1) * sdl_ref[...]
                + pltpu.roll(z, half, 1) * sdh_ref[...])

    @pl.when(j == 0)
    def _():
        ret_rot(rq_ref, 1.0)

    @pl.when(j == 1)
    def _():
        ret_rot(rk_ref, RET_DK ** -0.5)

    @pl.when(j == 2)
    def _():
        rv_ref[...] = project().astype(rv_ref.dtype)

    @pl.when(j == 3)
    def _():
        rg_ref[...] = project()

    @pl.when(j == 4)
    def _():
        z_all = project()
        for sl in heads:
            r = diff_rot(z_all[:, sl]) * dq_scale
            if transposed:
                dq_ref[sl, :] = r.T.astype(dq_ref.dtype)
            else:
                dq_ref[:, sl] = r.astype(dq_ref.dtype)

    @pl.when(j == 5)
    def _():
        z_all = project()
        for sl in heads:
            dk_ref[:, sl] = diff_rot(z_all[:, sl])

    @pl.when(j == 6)
    def _():
        z_all = project()
        dv_ref[...] = z_all
        if transposed:
            for sl in heads:
                dvt_ref[sl, :] = z_all[:, sl].T.astype(dvt_ref.dtype)


def _in_projection(x2d, g, w_bf, tabs, period_blocks, tm, act_dtype, dq_scale, transposed):
    T, D = x2d.shape
    cr, sr, cd, sdl, sdh = tabs
    tab_spec = pl.BlockSpec((tm, LANES), lambda i, j: (i % period_blocks, 0))
    row_spec = pl.BlockSpec((tm, SEG), lambda i, j: (i, 0))
    col_spec = pl.BlockSpec((SEG, tm), lambda i, j: (0, i))
    sds = lambda dt: jax.ShapeDtypeStruct((T, SEG), dt)
    sds_t = lambda dt: jax.ShapeDtypeStruct((SEG, T), dt)
    out_specs = [row_spec] * 4 + [col_spec if transposed else row_spec, row_spec, row_spec]
    out_shape = [sds(act_dtype), sds(act_dtype), sds(act_dtype), sds(F32),
                 sds_t(act_dtype) if transposed else sds(act_dtype), sds(F32), sds(F32)]
    if transposed:
        out_specs.append(col_spec)
        out_shape.append(sds_t(act_dtype))
    return pl.pallas_call(
        functools.partial(_inproj_kernel, dq_scale=dq_scale, transposed=transposed),
        grid=(T // tm, N_SEG),
        in_specs=[
            pl.BlockSpec((tm, D), lambda i, j: (i, 0)),
            pl.BlockSpec((1, D), lambda i, j: (0, 0)),
            pl.BlockSpec((D, SEG), lambda i, j: (0, j)),
            tab_spec, tab_spec, tab_spec, tab_spec, tab_spec,
        ],
        out_specs=out_specs,
        out_shape=out_shape,
        scratch_shapes=[pltpu.VMEM((tm, D), BF16)],
        compiler_params=_params(("parallel", "arbitrary")),
        name="in_projection",
    )(x2d, g.reshape(1, D), w_bf, cr, sr, cd, sdl, sdh)


def _ret_decay_tables(chunk):
    lg = jnp.log1p(-jnp.exp2(-5.0 - jnp.arange(RET_HEADS, dtype=F32)))
    idx = jnp.arange(chunk, dtype=F32)
    rel = idx[:, None] - idx[None, :]
    dmat = jnp.where(rel[None] >= 0, jnp.exp(jnp.maximum(rel, 0.0)[None] * lg[:, None, None]), 0.0)
    q_decay = jnp.exp((idx + 1.0)[:, None] * lg[None, :])
    k_decay = jnp.exp((chunk - 1.0 - idx)[:, None] * lg[None, :])
    chunk_decay = jnp.exp(chunk * lg)
    widen = lambda t: jnp.repeat(t, RET_DK, axis=1)
    return dmat, widen(q_decay), widen(k_decay), chunk_decay


def _ret_head(q, k, v, s, dmat, qd, kd, cd, gate):
    qb, kb, vb = q.astype(BF16), k.astype(BF16), v.astype(BF16)
    att = lax.dot_general(qb, kb, _NT, preferred_element_type=F32) * dmat
    o = (jnp.dot(att.astype(BF16), vb, preferred_element_type=F32)
         + jnp.dot(qb, s.astype(BF16), preferred_element_type=F32) * qd)
    kdk = (k.astype(F32) * kd).astype(BF16)
    s_new = s * cd + lax.dot_general(kdk, vb, _TN, preferred_element_type=F32)
    y = _unit_rms(o) * (gate * jax.nn.sigmoid(gate))
    return y, s_new


def _ret_prompt_kernel(cd_ref, q_ref, k_ref, v_ref, g_ref, dmat_ref, qd_ref, kd_ref, mix_ref, s_ref):
    @pl.when(pl.program_id(1) == 0)
    def _():
        s_ref[...] = jnp.zeros_like(s_ref)

    for h in range(RET_HEADS):
        sl = slice(h * RET_DK, (h + 1) * RET_DK)
        y, s_new = _ret_head(q_ref[:, sl], k_ref[:, sl], v_ref[:, sl], s_ref[0, h], dmat_ref[h],
                             qd_ref[:, sl], kd_ref[:, sl], cd_ref[h], g_ref[:, sl])
        s_ref[0, h] = s_new
        mix_ref[:, sl] = y.astype(mix_ref.dtype)


def _retention_prompt(rq, rk, rv, rg, batch, seq, chunk):
    n = seq // chunk
    dmat, qd, kd, cd = _ret_decay_tables(chunk)
    row = pl.BlockSpec((chunk, SEG), lambda b, c: (b * n + c, 0))
    full2 = pl.BlockSpec((chunk, SEG), lambda b, c: (0, 0))
    return pl.pallas_call(
        _ret_prompt_kernel,
        grid=(batch, n),
        in_specs=[
            pl.BlockSpec(memory_space=pltpu.SMEM),
            row, row, row, row,
            pl.BlockSpec((RET_HEADS, chunk, chunk), lambda b, c: (0, 0, 0)),
            full2, full2,
        ],
        out_specs=[row, pl.BlockSpec((1, RET_HEADS, RET_DK, RET_DV), lambda b, c: (b, 0, 0, 0))],
        out_shape=[jax.ShapeDtypeStruct((batch * seq, SEG), BF16),
                   jax.ShapeDtypeStruct((batch, RET_HEADS, RET_DK, RET_DV), F32)],
        compiler_params=_params(("parallel", "arbitrary")),
        name="retention_prompt",
    )(cd, rq, rk, rv, rg, dmat, qd, kd)


def _ret_sample_kernel(cd_ref, q_ref, k_ref, v_ref, g_ref, s0_ref, dmat_ref, qd_ref, kd_ref,
                       mix_ref, s_ref, *, group, seq):
    rows = []
    for b in range(group):
        r = slice(b * seq, (b + 1) * seq)
        heads = []
        for h in range(RET_HEADS):
            sl = slice(h * RET_DK, (h + 1) * RET_DK)
            y, s_new = _ret_head(q_ref[r, sl], k_ref[r, sl], v_ref[r, sl], s0_ref[b, h], dmat_ref[h],
                                 qd_ref[:, sl], kd_ref[:, sl], cd_ref[h], g_ref[r, sl])
            s_ref[b, h] = s_new
            heads.append(y)
        rows.append(jnp.concatenate(heads, axis=1))
    mix_ref[...] = jnp.concatenate(rows, axis=0)


def _retention_sample(rq, rk, rv, rg, state, layer, batch, seq, group):
    dmat, qd, kd, cd = _ret_decay_tables(seq)
    row = pl.BlockSpec((group * seq, SEG), lambda i: (i, 0))
    tab = pl.BlockSpec((seq, SEG), lambda i: (0, 0))
    st = pl.BlockSpec((group, RET_HEADS, RET_DK, RET_DV), lambda i: (i, 0, 0, 0))
    st_in = pl.BlockSpec((None, group, RET_HEADS, RET_DK, RET_DV), lambda i: (layer, i, 0, 0, 0))
    return pl.pallas_call(
        functools.partial(_ret_sample_kernel, group=group, seq=seq),
        grid=(batch // group,),
        in_specs=[
            pl.BlockSpec(memory_space=pltpu.SMEM),
            row, row, row, row, st_in,
            pl.BlockSpec((RET_HEADS, seq, seq), lambda i: (0, 0, 0)),
            tab, tab,
        ],
        out_specs=[row, st],
        out_shape=[jax.ShapeDtypeStruct((batch * seq, SEG), F32),
                   jax.ShapeDtypeStruct((batch, RET_HEADS, RET_DK, RET_DV), F32)],
        compiler_params=_params(("parallel",)),
        name="retention_sample",
    )(cd, rq, rk, rv, rg, state, dmat, qd, kd)


def _lambda_value(lam_ref, lam_init):
    a = jnp.sum(lam_ref[0:1, :] * lam_ref[1:2, :], axis=-1, keepdims=True)
    b = jnp.sum(lam_ref[2:3, :] * lam_ref[3:4, :], axis=-1, keepdims=True)
    return jnp.exp(a) - jnp.exp(b) + lam_init


def _softmax_update(s, m_old, l_old):
    m_new = jnp.maximum(m_old, jnp.max(s, axis=-1, keepdims=True))
    alpha = jnp.exp(m_old - m_new)
    p = jnp.exp(s - m_new)
    l_new = alpha * l_old + jnp.sum(p, axis=-1, keepdims=True)
    return p, alpha, m_new, l_new


def _diff_prompt_kernel(qi_tab, ki_tab, lam_ref, qt_ref, k_ref, vt_ref, g_ref, o_ref, m_scr, l_scr, acc_scr,
                        *, tq, tk, lam_init):
    t = pl.program_id(1)
    qi, ki = qi_tab[t], ki_tab[t]

    @pl.when(ki == 0)
    def _():
        m_scr[...] = jnp.full_like(m_scr, -jnp.inf)
        l_scr[...] = jnp.zeros_like(l_scr)
        acc_scr[...] = jnp.zeros_like(acc_scr)

    def step(masked):
        if masked:
            kpos = ki * tk + lax.broadcasted_iota(jnp.int32, (tk, tq), 0)
            qpos = qi * tq + lax.broadcasted_iota(jnp.int32, (tk, tq), 1)
            keep = qpos >= kpos
        for h in range(DIFF_HEADS):
            vt = vt_ref[h * DIFF_DV:(h + 1) * DIFF_DV, :]
            for m in range(2):
                i = 2 * h + m
                sl = slice(i * DIFF_DK, (i + 1) * DIFF_DK)
                kb = k_ref[:, sl].astype(BF16)
                for c in range(tq // MXU_COLS):
                    cs = slice(c * MXU_COLS, (c + 1) * MXU_COLS)
                    st = jnp.dot(kb, qt_ref[sl, cs], preferred_element_type=F32)
                    if masked:
                        st = jnp.where(keep[:, cs], st, -jnp.inf)
                    m_old = m_scr[i, :, cs]
                    m_new = jnp.maximum(m_old, jnp.max(st, axis=0, keepdims=True))
                    alpha = jnp.exp2(m_old - m_new)
                    p = jnp.exp2(st - m_new)
                    l_scr[i, :, cs] = alpha * l_scr[i, :, cs] + jnp.sum(p, axis=0, keepdims=True)
                    m_scr[i, :, cs] = m_new
                    acc_scr[i, :, cs] = (alpha * acc_scr[i, :, cs]
                                         + jnp.dot(vt, p.astype(BF16), preferred_element_type=F32))

    @pl.when(ki < qi)
    def _():
        step(False)

    @pl.when(ki == qi)
    def _():
        step(True)
        lam = _lambda_value(lam_ref, lam_init)
        for h in range(DIFF_HEADS):
            a, b = 2 * h, 2 * h + 1
            ot = acc_scr[a] * (1.0 / l_scr[a]) - lam * (acc_scr[b] * (1.0 / l_scr[b]))
            yt = ot * lax.rsqrt(jnp.mean(ot * ot, axis=0, keepdims=True) + EPS)
            o_ref[:, h * DIFF_DV:(h + 1) * DIFF_DV] = (yt.T * g_ref[...] * (1.0 - lam_init)).astype(o_ref.dtype)


def _diff_attention_prompt(dq_t, dk, dv_t, lam_rows, subln_g, batch, seq, lam_init, tq=512, tk=512):
    assert tq == tk
    nq = seq // tq
    pairs = [(qi, ki) for qi in range(nq) for ki in range(qi + 1)]
    qi_tab = jnp.asarray([p[0] for p in pairs], jnp.int32)
    ki_tab = jnp.asarray([p[1] for p in pairs], jnp.int32)
    n_sub = 2 * DIFF_HEADS
    grid_spec = pltpu.PrefetchScalarGridSpec(
        num_scalar_prefetch=2,
        grid=(batch, len(pairs)),
        in_specs=[
            pl.BlockSpec((4, DIFF_DK), lambda b, t, qt, kt: (0, 0)),
            pl.BlockSpec((SEG, tq), lambda b, t, qt, kt: (0, b * nq + qt[t])),
            pl.BlockSpec((tk, SEG), lambda b, t, qt, kt: (b * nq + kt[t], 0)),
            pl.BlockSpec((SEG, tk), lambda b, t, qt, kt: (0, b * nq + kt[t])),
            pl.BlockSpec((1, DIFF_DV), lambda b, t, qt, kt: (0, 0)),
        ],
        out_specs=pl.BlockSpec((tq, SEG), lambda b, t, qt, kt: (b * nq + qt[t], 0)),
        scratch_shapes=[pltpu.VMEM((n_sub, 1, tq), F32), pltpu.VMEM((n_sub, 1, tq), F32),
                        pltpu.VMEM((n_sub, DIFF_DV, tq), F32)],
    )
    return pl.pallas_call(
        functools.partial(_diff_prompt_kernel, tq=tq, tk=tk, lam_init=lam_init),
        grid_spec=grid_spec,
        out_shape=jax.ShapeDtypeStruct((batch * seq, SEG), BF16),
        compiler_params=_params(("parallel", "arbitrary")),
        name="diff_attention_prompt",
    )(qi_tab, ki_tab, lam_rows, dq_t, dk, dv_t, subln_g.reshape(1, DIFF_DV))


def _head_match_bias(n_rows, seq, n_cols, heads):
    row_h = jnp.arange(n_rows, dtype=jnp.int32)[:, None] // seq
    col_h = jnp.arange(n_cols, dtype=jnp.int32)[None, :] % heads
    return jnp.where(row_h == col_h, 0.0, -jnp.inf).astype(F32)


def _stack_heads(x, width, offset, stride):
    return jnp.concatenate(
        [x[:, offset + h * stride: offset + h * stride + width] for h in range(DIFF_HEADS)], axis=0)


def _stacked_queries(q_ref):
    q = q_ref[...]
    return [_stack_heads(q, DIFF_DK, m * DIFF_DK, 2 * DIFF_DK).astype(BF16) for m in range(2)]


def _diff_sample_begin(m_scr, l_scr, acc_scr):
    m_scr[...] = jnp.full_like(m_scr, -jnp.inf)
    l_scr[...] = jnp.zeros_like(l_scr)
    acc_scr[...] = jnp.zeros_like(acc_scr)


def _diff_sample_pages(bias_ref, q_ref, k_refs, v_refs, m_scr, l_scr, acc_scr, *, seq):
    rows = DIFF_HEADS * seq
    kv_rows = k_refs[0].shape[0] // 2
    qs = _stacked_queries(q_ref)
    bias = bias_ref[...]
    ps, alphas = [], []
    for m in range(2):
        s = jnp.concatenate(
            [lax.dot_general(qs[m], kr[pl.ds(m, kv_rows, stride=2), :].astype(BF16), _NT,
                             preferred_element_type=F32) + bias for kr in k_refs], axis=1)
        p, alpha, m_new, l_new = _softmax_update(s, m_scr[m], l_scr[m])
        m_scr[m] = m_new
        l_scr[m] = l_new
        ps.append(p)
        alphas.append(alpha)
    p = jnp.concatenate(ps, axis=0).astype(BF16)
    for e in range(2):
        pv = None
        for n in range(len(v_refs)):
            v = v_refs[n][:, :, e * LANES:(e + 1) * LANES].reshape(kv_rows, LANES).astype(BF16)
            t = jnp.dot(p[:, n * kv_rows:(n + 1) * kv_rows], v, preferred_element_type=F32)
            pv = t if pv is None else pv + t
        for m in range(2):
            cols = slice(e * LANES, (e + 1) * LANES)
            acc_scr[m, :, cols] = alphas[m] * acc_scr[m, :, cols] + pv[m * rows:(m + 1) * rows]


def _diff_sample_finish(lam_ref, q_ref, kn_ref, vn_ref, g_ref, o_ref, m_scr, l_scr, acc_scr, *, seq, lam_init):
    qs = _stacked_queries(q_ref)
    kn, vn = kn_ref[...], vn_ref[...]
    causal = (lax.broadcasted_iota(jnp.int32, (seq, seq), 0)
              >= lax.broadcasted_iota(jnp.int32, (seq, seq), 1))
    accs, ls = [], []
    for m in range(2):
        s = jnp.concatenate(
            [jnp.where(causal,
                       lax.dot_general(qs[m][h * seq:(h + 1) * seq],
                                       kn[:, (2 * h + m) * DIFF_DK:(2 * h + m + 1) * DIFF_DK].astype(BF16),
                                       _NT, preferred_element_type=F32),
                       -jnp.inf) for h in range(DIFF_HEADS)], axis=0)
        p, alpha, _, l_new = _softmax_update(s, m_scr[m], l_scr[m])
        pn = p.astype(BF16)
        pv = jnp.concatenate(
            [jnp.dot(pn[h * seq:(h + 1) * seq], vn[:, h * DIFF_DV:(h + 1) * DIFF_DV].astype(BF16),
                     preferred_element_type=F32) for h in range(DIFF_HEADS)], axis=0)
        accs.append(alpha * acc_scr[m] + pv)
        ls.append(l_new)
    lam = _lambda_value(lam_ref, lam_init)
    o = accs[0] / ls[0] - lam * (accs[1] / ls[1])
    y = _unit_rms(o) * g_ref[...] * (1.0 - lam_init)
    o_ref[...] = jnp.concatenate([y[h * seq:(h + 1) * seq] for h in range(DIFF_HEADS)], axis=1)


def _mlp_and_sample_attention_kernel(pt_ref, x_ref, gpre_ref, wup_ref, wdn_ref, gpost_ref,
                                     lam_ref, bias_ref, q_ref, kn_ref, vn_ref, g_ref, ck_hbm, cv_hbm,
                                     y_ref, o_ref, h_scr, m_scr, l_scr, acc_scr, k_buf, v_buf, sem,
                                     *, layer, n_pages_step, attn_steps, seq, lam_init):
    j, n_j = pl.program_id(1), pl.num_programs(1)
    g = pl.program_id(0) * n_j + j
    total = pl.num_programs(0) * n_j
    step = lax.rem(g, attn_steps)
    slot = lax.rem(g, 2)

    def page_copies(at_step, into):
        b, s = lax.div(at_step, attn_steps), lax.rem(at_step, attn_steps)
        copies = []
        for n in range(n_pages_step):
            page = pt_ref[b, s * n_pages_step + n]
            copies.append(pltpu.make_async_copy(ck_hbm.at[layer, page], k_buf.at[into, n], sem.at[into]))
            copies.append(pltpu.make_async_copy(cv_hbm.at[layer, page], v_buf.at[into, n], sem.at[into]))
        return copies

    def start_all(copies):
        for c in copies:
            c.start(priority=PAGE_DMA_PRIORITY)

    pl.when(g == 0)(lambda: start_all(page_copies(g, slot)))
    pl.when(g + 1 < total)(lambda: start_all(page_copies(g + 1, 1 - slot)))
    pl.when(j == 0)(lambda: _mlp_begin(x_ref, gpre_ref, y_ref, h_scr))
    pl.when(step == 0)(lambda: _diff_sample_begin(m_scr, l_scr, acc_scr))
    for c in page_copies(g, slot):
        c.wait()
    k_refs = [k_buf.at[slot, n] for n in range(n_pages_step)]
    v_refs = [v_buf.at[slot, n] for n in range(n_pages_step)]
    _mlp_tile(wup_ref, wdn_ref, y_ref, h_scr)
    _diff_sample_pages(bias_ref, q_ref, k_refs, v_refs, m_scr, l_scr, acc_scr, seq=seq)
    pl.when(j == n_j - 1)(lambda: _mlp_finish(x_ref, gpost_ref, y_ref))
    pl.when(step == attn_steps - 1)(lambda: _diff_sample_finish(
        lam_ref, q_ref, kn_ref, vn_ref, g_ref, o_ref, m_scr, l_scr, acc_scr, seq=seq, lam_init=lam_init))


def _mlp_and_sample_attention(x2d, g_pre, w_up_bf, w_dn_bf, g_post, tm, tf,
                              dq, dk_new, dv_new, cache_k, cache_v, layer, page_table, lam_rows, subln_g,
                              batch, seq, lam_init, pages_per_step):
    T, D = x2d.shape
    FF = w_up_bf.shape[1]
    n_i, n_j = T // tm, FF // tf
    depth, n_pool, page = cache_k.shape[:3]
    n_pages = page_table.shape[1]
    attn_steps = n_pages // pages_per_step
    assert n_pages % pages_per_step == 0 and DIFF_DV == 2 * LANES
    assert n_i * n_j == batch * attn_steps, "the two jobs must have the same number of grid steps"
    k_rows = page * DIFF_HEADS * 2
    ck = cache_k.reshape(depth, n_pool, k_rows, DIFF_DK)
    rows = DIFF_HEADS * seq
    bias = _head_match_bias(rows, seq, page * DIFF_HEADS, DIFF_HEADS)

    vec = pl.BlockSpec((1, D), lambda i, j, pt: (0, 0))
    row = pl.BlockSpec((seq, SEG), lambda i, j, pt: ((i * n_j + j) // attn_steps, 0))
    hbm = pl.BlockSpec(memory_space=pl.ANY)
    grid_spec = pltpu.PrefetchScalarGridSpec(
        num_scalar_prefetch=1,
        grid=(n_i, n_j),
        in_specs=[pl.BlockSpec((tm, D), lambda i, j, pt: (i, 0)), vec,
                  pl.BlockSpec((D, tf), lambda i, j, pt: (0, j)),
                  pl.BlockSpec((tf, D), lambda i, j, pt: (j, 0)), vec,
                  pl.BlockSpec((4, DIFF_DK), lambda i, j, pt: (0, 0)),
                  pl.BlockSpec(bias.shape, lambda i, j, pt: (0, 0)),
                  row, row, row,
                  pl.BlockSpec((1, DIFF_DV), lambda i, j, pt: (0, 0)),
                  hbm, hbm],
        out_specs=[pl.BlockSpec((tm, D), lambda i, j, pt: (i, 0)), row],
        scratch_shapes=[pltpu.VMEM((tm, D), BF16),
                        pltpu.VMEM((2, rows, 1), F32), pltpu.VMEM((2, rows, 1), F32),
                        pltpu.VMEM((2, rows, DIFF_DV), F32),
                        pltpu.VMEM((2, pages_per_step, k_rows, DIFF_DK), F32),
                        pltpu.VMEM((2, pages_per_step, page, DIFF_HEADS, DIFF_DV), F32),
                        pltpu.SemaphoreType.DMA((2,))],
    )
    return pl.pallas_call(
        functools.partial(_mlp_and_sample_attention_kernel, layer=layer, n_pages_step=pages_per_step,
                          attn_steps=attn_steps, seq=seq, lam_init=lam_init),
        grid_spec=grid_spec,
        out_shape=[jax.ShapeDtypeStruct((T, D), F32), jax.ShapeDtypeStruct((batch * seq, SEG), F32)],
        compiler_params=_params(("arbitrary", "arbitrary")),
        name="mlp_and_sample_attention",
    )(page_table, x2d, g_pre.reshape(1, D), w_up_bf, w_dn_bf, g_post.reshape(1, D),
      lam_rows, bias, dq, dk_new, dv_new, subln_g.reshape(1, DIFF_DV), ck, cache_v)


def _norm_matmul_kernel(x_ref, g_ref, w_ref, o_ref, *, scale):
    h = _rms(x_ref[...], g_ref[...]).astype(BF16)
    z = jnp.dot(h, w_ref[...], preferred_element_type=F32)
    if scale != 1.0:
        z = z * scale
    o_ref[...] = z.astype(o_ref.dtype)


def _norm_matmul(x2d, g, w_bf, tm, out_dtype, scale=1.0):
    T, D = x2d.shape
    N = w_bf.shape[1]
    return pl.pallas_call(
        functools.partial(_norm_matmul_kernel, scale=scale),
        grid=(T // tm,),
        in_specs=[pl.BlockSpec((tm, D), lambda i: (i, 0)),
                  pl.BlockSpec((1, D), lambda i: (0, 0)),
                  pl.BlockSpec((D, N), lambda i: (0, 0))],
        out_specs=pl.BlockSpec((tm, N), lambda i: (i, 0)),
        out_shape=jax.ShapeDtypeStruct((T, N), out_dtype),
        compiler_params=_params(("parallel",)),
        name="norm_matmul",
    )(x2d, g.reshape(1, D), w_bf)


def _matmul_norm_res_kernel(*refs, n_in):
    a_refs = refs[:n_in]
    w_refs = refs[n_in:2 * n_in]
    g_ref, res_ref, o_ref = refs[2 * n_in:]
    z = None
    for a_ref, w_ref in zip(a_refs, w_refs):
        t = jnp.dot(a_ref[...].astype(BF16), w_ref[...], preferred_element_type=F32)
        z = t if z is None else z + t
    o_ref[...] = res_ref[...] + _rms(z, g_ref[...])


def _matmul_norm_residual(acts, ws_bf, g, res, tm):
    T, D = res.shape
    n_in = len(acts)
    in_specs = ([pl.BlockSpec((tm, a.shape[1]), lambda i: (i, 0)) for a in acts]
                + [pl.BlockSpec(w.shape, lambda i: (0, 0)) for w in ws_bf]
                + [pl.BlockSpec((1, D), lambda i: (0, 0)), pl.BlockSpec((tm, D), lambda i: (i, 0))])
    return pl.pallas_call(
        functools.partial(_matmul_norm_res_kernel, n_in=n_in),
        grid=(T // tm,),
        in_specs=in_specs,
        out_specs=pl.BlockSpec((tm, D), lambda i: (i, 0)),
        out_shape=jax.ShapeDtypeStruct((T, D), F32),
        compiler_params=_params(("parallel",)),
        name="matmul_norm_residual",
    )(*acts, *ws_bf, g.reshape(1, D), res)


def _softmax_rows(s):
    e = jnp.exp(s - jnp.max(s, axis=-1, keepdims=True))
    return e / jnp.sum(e, axis=-1, keepdims=True)


def _cross_sample_kernel(bias_ref, q_ref, mk_ref, mv_ref, o_ref, *, group, seq):
    bias = bias_ref[...]
    outs = []
    for b in range(group):
        qb = q_ref[b * seq:(b + 1) * seq, :]
        qh = jnp.concatenate([qb[:, h * MEM_DH:(h + 1) * MEM_DH] for h in range(MEM_HEADS)], axis=0)
        s = lax.dot_general(qh.astype(BF16), mk_ref[b].astype(BF16), _NT, preferred_element_type=F32) + bias
        p = _softmax_rows(s).astype(BF16)
        o = jnp.dot(p, mv_ref[b].astype(BF16), preferred_element_type=F32)
        outs.append(jnp.concatenate([o[h * seq:(h + 1) * seq] for h in range(MEM_HEADS)], axis=1))
    o_ref[...] = jnp.concatenate(outs, axis=0)


def _cross_attention_sample(q, mem_k, mem_v, layer, batch, seq, group):
    depth, _, M = mem_k.shape[:3]
    W = MEM_HEADS * MEM_DH
    mk = mem_k.reshape(depth, batch, M * MEM_HEADS, MEM_DH)
    mv = mem_v.reshape(depth, batch, M * MEM_HEADS, MEM_DH)
    bias = _head_match_bias(MEM_HEADS * seq, seq, M * MEM_HEADS, MEM_HEADS)
    kv = pl.BlockSpec((None, group, M * MEM_HEADS, MEM_DH), lambda i: (layer, i, 0, 0))
    row = pl.BlockSpec((group * seq, W), lambda i: (i, 0))
    return pl.pallas_call(
        functools.partial(_cross_sample_kernel, group=group, seq=seq),
        grid=(batch // group,),
        in_specs=[pl.BlockSpec(bias.shape, lambda i: (0, 0)), row, kv, kv],
        out_specs=row,
        out_shape=jax.ShapeDtypeStruct((batch * seq, W), F32),
        compiler_params=_params(("parallel",)),
        name="cross_attention_sample",
    )(bias, q, mk, mv)


def _mlp_begin(x_ref, gpre_ref, o_ref, h_scr):
    h_scr[...] = _rms(x_ref[...], gpre_ref[...]).astype(BF16)
    o_ref[...] = jnp.zeros_like(o_ref)


def _mlp_tile(wup_ref, wdn_ref, o_ref, h_scr):
    u = jnp.maximum(jnp.dot(h_scr[...], wup_ref[...], preferred_element_type=F32), 0.0)
    o_ref[...] += jnp.dot((u * u).astype(BF16), wdn_ref[...], preferred_element_type=F32)


def _mlp_finish(x_ref, gpost_ref, o_ref):
    o_ref[...] = x_ref[...] + _rms(o_ref[...], gpost_ref[...])


def _mlp_kernel(x_ref, gpre_ref, wup_ref, wdn_ref, gpost_ref, o_ref, h_scr):
    j = pl.program_id(1)
    pl.when(j == 0)(lambda: _mlp_begin(x_ref, gpre_ref, o_ref, h_scr))
    _mlp_tile(wup_ref, wdn_ref, o_ref, h_scr)
    pl.when(j == pl.num_programs(1) - 1)(lambda: _mlp_finish(x_ref, gpost_ref, o_ref))


def _mlp(x2d, g_pre, w_up_bf, w_dn_bf, g_post, tm, tf):
    T, D = x2d.shape
    FF = w_up_bf.shape[1]
    vec = pl.BlockSpec((1, D), lambda i, j: (0, 0))
    return pl.pallas_call(
        _mlp_kernel,
        grid=(T // tm, FF // tf),
        in_specs=[pl.BlockSpec((tm, D), lambda i, j: (i, 0)), vec,
                  pl.BlockSpec((D, tf), lambda i, j: (0, j)),
                  pl.BlockSpec((tf, D), lambda i, j: (j, 0)), vec],
        out_specs=pl.BlockSpec((tm, D), lambda i, j: (i, 0)),
        out_shape=jax.ShapeDtypeStruct((T, D), F32),
        scratch_shapes=[pltpu.VMEM((tm, D), BF16)],
        compiler_params=_params(("parallel", "arbitrary")),
        name="mlp",
    )(x2d, g_pre.reshape(1, D), w_up_bf, w_dn_bf, g_post.reshape(1, D))


def _position_tables(pos, tm):
    reps = max(1, tm // pos.shape[0])
    pos = jnp.tile(pos, reps)
    cr, srl, srh = _rope_tables(pos, RET_DK, RET_THETA)
    cd, sdl, sdh = _rope_tables(pos, ROPE_DIM, ROPE_THETA)
    return (cr, srl + srh, cd, sdl, sdh), pos.shape[0] // tm


def _layer_tail(x2d, mix_parts, wts, cross_fn, tm):
    (w_out_parts, g_mix_post, g_mem_pre, w_mem_q, w_mem_o, g_mem_post,
     g_mlp_pre, w_up, w_down, g_mlp_post) = wts
    x1 = _matmul_norm_residual(mix_parts, w_out_parts, g_mix_post, x2d, tm)
    q = cross_fn(x1, g_mem_pre, w_mem_q)
    x2 = _matmul_norm_residual([q], [w_mem_o], g_mem_post, x1, tm)
    return _mlp(x2, g_mlp_pre, w_up, w_down, g_mlp_post, tm, MLP_FF_TILE)


def _out_cross_kernel(mr_ref, md_ref, x_ref, wo1_ref, wo2_ref, g1_ref, gpre_ref, wq_ref, mk_ref, mv_ref,
                      wmo_ref, g2_ref, o_ref):
    z = (jnp.dot(mr_ref[...], wo1_ref[...], preferred_element_type=F32)
         + jnp.dot(md_ref[...], wo2_ref[...], preferred_element_type=F32))
    x1 = x_ref[...] + _rms(z, g1_ref[...])
    h = _rms(x1, gpre_ref[...]).astype(BF16)
    q = (jnp.dot(h, wq_ref[...], preferred_element_type=F32) * (MEM_DH ** -0.5)).astype(BF16)
    outs = []
    for hd in range(MEM_HEADS):
        sl = slice(hd * MEM_DH, (hd + 1) * MEM_DH)
        s = lax.dot_general(q[:, sl], mk_ref[0, :, sl], _NT, preferred_element_type=F32)
        p = _softmax_rows(s).astype(BF16)
        outs.append(jnp.dot(p, mv_ref[0, :, sl], preferred_element_type=F32).astype(BF16))
    y = jnp.dot(jnp.concatenate(outs, axis=1), wmo_ref[...], preferred_element_type=F32)
    o_ref[...] = x1 + _rms(y, g2_ref[...])


def _out_proj_cross_block(mix_parts, x2d, wts, mk_bf, mv_bf, batch, seq, tm):
    (w_out_parts, g_mix_post, g_mem_pre, w_mem_q, w_mem_o, g_mem_post) = wts[:6]
    T, D = x2d.shape
    n = seq // tm
    M, W = mk_bf.shape[1], mk_bf.shape[2]
    row = lambda width: pl.BlockSpec((tm, width), lambda b, i: (b * n + i, 0))
    const = lambda shape: pl.BlockSpec(shape, lambda b, i: (0,) * len(shape), pipeline_mode=pl.Buffered(1))
    kv = pl.BlockSpec((1, M, W), lambda b, i: (b, 0, 0))
    vec = lambda g: g.reshape(1, D)
    return pl.pallas_call(
        _out_cross_kernel,
        grid=(batch, n),
        in_specs=[row(mix_parts[0].shape[1]), row(mix_parts[1].shape[1]), row(D),
                  const(w_out_parts[0].shape), const(w_out_parts[1].shape), const((1, D)), const((1, D)),
                  const(w_mem_q.shape), kv, kv, const(w_mem_o.shape), const((1, D))],
        out_specs=row(D),
        out_shape=jax.ShapeDtypeStruct((T, D), F32),
        compiler_params=_params(("parallel", "parallel")),
        name="out_proj_cross_block",
    )(mix_parts[0], mix_parts[1], x2d, w_out_parts[0], w_out_parts[1], vec(g_mix_post), vec(g_mem_pre),
      w_mem_q, mk_bf, mv_bf, w_mem_o, vec(g_mem_post))


def kernel(x_prompt, x_sample, mem_prompt, state_ret, cache_diff_k, cache_diff_v, cache_mem_k, cache_mem_v, page_table, w_in, w_out, diff_lambda_q1, diff_lambda_k1, diff_lambda_q2, diff_lambda_k2, diff_subln_g, norm_mix_pre, norm_mix_post, norm_mem_pre, norm_mem_post, norm_mlp_pre, norm_mlp_post, mem_norm_g, w_mem_q, w_mem_k, w_mem_v, w_mem_o, w_mlp_up, w_mlp_down):
    depth = w_in.shape[0]
    B, L_p, D = x_prompt.shape
    B_s, L_s, _ = x_sample.shape
    n_pages, page = page_table.shape[1], cache_diff_k.shape[2]
    past_len = n_pages * page
    M = mem_prompt.shape[1]
    W_MEM = MEM_HEADS * MEM_DH
    TM = 512
    ret_chunk_p = math.gcd(L_p, RET_CHUNK)

    tabs_p, per_p = _position_tables(jnp.arange(L_p, dtype=F32), TM)
    TM_IN_S = 256
    tabs_s, per_s = _position_tables(past_len + jnp.arange(L_s, dtype=F32), TM_IN_S)

    yp = x_prompt.reshape(B * L_p, D)
    ys = x_sample.reshape(B_s * L_s, D)
    mem2d = mem_prompt.reshape(B * M, D)
    outs = {k: [] for k in ("rp", "kp", "vp", "mkp", "mvp", "rs", "ks", "vs")}

    for i in range(depth):
        lam_init = 0.8 - 0.6 * math.exp(-0.3 * i)
        lam_rows = jnp.stack([diff_lambda_q1[i], diff_lambda_k1[i], diff_lambda_q2[i], diff_lambda_k2[i]])
        w_in_bf = w_in[i].astype(BF16)
        half = w_out.shape[1] // 2
        tail_w = ([w_out[i, :half].astype(BF16), w_out[i, half:].astype(BF16)], norm_mix_post[i],
                  norm_mem_pre[i], w_mem_q[i].astype(BF16), w_mem_o[i].astype(BF16), norm_mem_post[i],
                  norm_mlp_pre[i], w_mlp_up[i].astype(BF16), w_mlp_down[i].astype(BF16), norm_mlp_post[i])

        mk_p = _norm_matmul(mem2d, mem_norm_g[i], w_mem_k[i].astype(BF16), B * M, F32)
        mv_p = _norm_matmul(mem2d, mem_norm_g[i], w_mem_v[i].astype(BF16), B * M, F32)
        rq, rk, rv, rg, dq_t, dk, dv, dv_t = _in_projection(
            yp, norm_mix_pre[i], w_in_bf, tabs_p, per_p, TM, BF16, DIFF_DK ** -0.5 * LOG2E, True)
        mix_ret, s_p = _retention_prompt(rq, rk, rv, rg, B, L_p, ret_chunk_p)
        mix_diff = _diff_attention_prompt(dq_t, dk, dv_t, lam_rows, diff_subln_g[i], B, L_p, lam_init)
        mk_bf = mk_p.astype(BF16).reshape(B, M, W_MEM)
        mv_bf = mv_p.astype(BF16).reshape(B, M, W_MEM)

        x2 = _out_proj_cross_block([mix_ret, mix_diff], yp, tail_w, mk_bf, mv_bf, B, L_p, TM)
        outs["rp"].append(s_p.astype(state_ret.dtype))
        outs["kp"].append(dk.reshape(B, L_p, DIFF_HEADS, 2, DIFF_DK))
        outs["vp"].append(dv.reshape(B, L_p, DIFF_HEADS, DIFF_DV))
        outs["mkp"].append(mk_p.reshape(B, M, MEM_HEADS, MEM_DH))
        outs["mvp"].append(mv_p.reshape(B, M, MEM_HEADS, MEM_DH))

        rq, rk, rv, rg, dq, dk, dv = _in_projection(ys, norm_mix_pre[i], w_in_bf, tabs_s, per_s, TM_IN_S, F32,
                                                    DIFF_DK ** -0.5, False)
        mix_ret, s_s = _retention_sample(rq, rk, rv, rg, state_ret, i, B_s, L_s, 4)
        yp, mix_diff = _mlp_and_sample_attention(
            x2, norm_mlp_pre[i], tail_w[7], tail_w[8], norm_mlp_post[i], TM, MLP_FF_TILE_FUSED,
            dq, dk, dv, cache_diff_k, cache_diff_v, i, page_table, lam_rows, diff_subln_g[i],
            B_s, L_s, lam_init, PAGES_PER_STEP)
        def cross_s(x1, g, wq):
            q = _norm_matmul(x1, g, wq, TM, F32, scale=MEM_DH ** -0.5)
            return _cross_attention_sample(q, cache_mem_k, cache_mem_v, i, B_s, L_s, 8)

        ys = _layer_tail(ys, [mix_ret, mix_diff], tail_w, cross_s, TM)
        outs["rs"].append(s_s.astype(state_ret.dtype))
        outs["ks"].append(dk.reshape(B_s, L_s, DIFF_HEADS, 2, DIFF_DK))
        outs["vs"].append(dv.reshape(B_s, L_s, DIFF_HEADS, DIFF_DV))

    st = lambda k: outs[k][0][None] if depth == 1 else jnp.stack(outs[k])
    return (yp.reshape(B, L_p, D), ys.reshape(B_s, L_s, D), st("rp"), st("kp"), st("vp"),
            st("mkp"), st("mvp"), st("rs"), st("ks"), st("vs"))
```

```python
import functools
import math

import jax
import jax.numpy as jnp
from jax import lax
from jax.experimental import pallas as pl
from jax.experimental.pallas import tpu as pltpu

F32 = jnp.float32
BF16 = jnp.bfloat16

LANES = 128
MXU_COLS = 256
RET_HEADS = 8
RET_DK = 128
RET_DV = 128
RET_THETA = 10000.0
RET_CHUNK = 2 * 128
DIFF_HEADS = 4
DIFF_DK = 128
DIFF_DV = 256
ROPE_THETA = 500000.0
ROPE_DIM = DIFF_DK // 4
MEM_HEADS = 4
MEM_DH = 128
Q_BLOCK = 128
EPS = 1e-6
LOG2E = math.log2(math.e)
SEG = 1024
N_SEG = 7
MLP_FF_TILE = 1024
MLP_FF_TILE_FUSED = 512
PAGES_PER_STEP = 8
PAGE_DMA_PRIORITY = 1

VMEM_LIMIT = 56 * 1024 * 1024

_NT = (((1,), (1,)), ((), ()))
_TN = (((0,), (0,)), ((), ()))


def _params(sem, vmem=VMEM_LIMIT):
    return pltpu.CompilerParams(dimension_semantics=sem, vmem_limit_bytes=vmem)


def _rms(x, g):
    return x * lax.rsqrt(jnp.mean(x * x, axis=-1, keepdims=True) + EPS) * g


def _unit_rms(o):
    return o * lax.rsqrt(jnp.mean(o * o, axis=-1, keepdims=True) + EPS)


def _rope_tables(pos, rot_dim, theta):
    half = rot_dim // 2
    inv = jnp.exp(-math.log(theta) * (2.0 * jnp.arange(half, dtype=F32) / rot_dim))
    ang = pos[:, None] * inv[None, :]
    cos, sin = jnp.cos(ang), jnp.sin(ang)
    n = pos.shape[0]
    rest = LANES - rot_dim
    c = jnp.concatenate([cos, cos, jnp.ones((n, rest), F32)], axis=-1)
    s_lo = jnp.concatenate([-sin, jnp.zeros((n, LANES - half), F32)], axis=-1)
    s_hi = jnp.concatenate([jnp.zeros((n, half), F32), sin, jnp.zeros((n, rest), F32)], axis=-1)
    return c, s_lo, s_hi


def _inproj_kernel(x_ref, g_ref, w_ref, cr_ref, sr_ref, cd_ref, sdl_ref, sdh_ref, *rest,
                   dq_scale, transposed):
    if transposed:
        rq_ref, rk_ref, rv_ref, rg_ref, dq_ref, dk_ref, dv_ref, dvt_ref, h_scr = rest
    else:
        rq_ref, rk_ref, rv_ref, rg_ref, dq_ref, dk_ref, dv_ref, h_scr = rest
    j = pl.program_id(1)
    heads = [slice(h * LANES, (h + 1) * LANES) for h in range(SEG // LANES)]

    @pl.when(j == 0)
    def _():
        h_scr[...] = _rms(x_ref[...], g_ref[...]).astype(BF16)

    def project():
        return jnp.dot(h_scr[...], w_ref[...], preferred_element_type=F32)

    def ret_rot(out_ref, scale):
        z_all = project()
        c, s = cr_ref[...], sr_ref[...]
        for sl in heads:
            z = z_all[:, sl]
            r = z * c + pltpu.roll(z, RET_DK // 2, 1) * s
            if scale != 1.0:
                r = r * scale
            out_ref[:, sl] = r.astype(out_ref.dtype)

    def diff_rot(z):
        half = ROPE_DIM // 2
        return (z * cd_ref[...] + pltpu.roll(z, LANES - half, 1) * sdl_ref[...]
                + pltpu.roll(z, half, 1) * sdh_ref[...])

    @pl.when(j == 0)
    def _():
        ret_rot(rq_ref, 1.0)

    @pl.when(j == 1)
    def _():
        ret_rot(rk_ref, RET_DK ** -0.5)

    @pl.when(j == 2)
    def _():
        rv_ref[...] = project().astype(rv_ref.dtype)

    @pl.when(j == 3)
    def _():
        rg_ref[...] = project()

    @pl.when(j == 4)
    def _():
        z_all = project()
        for sl in heads:
            r = diff_rot(z_all[:, sl]) * dq_scale
            if transposed:
                dq_ref[sl, :] = r.T.astype(dq_ref.dtype)
            else:
                dq_ref[:, sl] = r.astype(dq_ref.dtype)

    tm = x_ref.shape[0]
    n_heads = len(heads)

    @pl.when(j == 5)
    def _():
        z_all = project()
        for i, sl in enumerate(heads):
            r = diff_rot(z_all[:, sl])
            if transposed:
                dk_ref[pl.ds(i, tm, stride=n_heads), :] = r
            else:
                dk_ref[:, sl] = r

    @pl.when(j == 6)
    def _():
        z_all = project()
        if transposed:
            for i, sl in enumerate(heads):
                h, e = divmod(i, DIFF_DV // LANES)
                dv_ref[pl.ds(e * DIFF_HEADS + h, tm, stride=n_heads), :] = z_all[:, sl]
                dvt_ref[sl, :] = z_all[:, sl].T.astype(dvt_ref.dtype)
        else:
            dv_ref[...] = z_all


def _in_projection(x2d, g, w_bf, tabs, period_blocks, tm, act_dtype, dq_scale, transposed):
    T, D = x2d.shape
    cr, sr, cd, sdl, sdh = tabs
    tab_spec = pl.BlockSpec((tm, LANES), lambda i, j: (i % period_blocks, 0))
    row_spec = pl.BlockSpec((tm, SEG), lambda i, j: (i, 0))
    col_spec = pl.BlockSpec((SEG, tm), lambda i, j: (0, i))
    sds = lambda dt: jax.ShapeDtypeStruct((T, SEG), dt)
    sds_t = lambda dt: jax.ShapeDtypeStruct((SEG, T), dt)
    n_heads = SEG // LANES
    cache_spec = pl.BlockSpec((tm * n_heads, LANES), lambda i, j: (i, 0))
    sds_c = jax.ShapeDtypeStruct((T * n_heads, LANES), F32)
    if transposed:
        out_specs = [row_spec] * 4 + [col_spec, cache_spec, cache_spec, col_spec]
        out_shape = [sds(act_dtype), sds(act_dtype), sds(act_dtype), sds(F32),
                     sds_t(act_dtype), sds_c, sds_c, sds_t(act_dtype)]
    else:
        out_specs = [row_spec] * N_SEG
        out_shape = [sds(act_dtype), sds(act_dtype), sds(act_dtype), sds(F32), sds(act_dtype), sds(F32), sds(F32)]
    return pl.pallas_call(
        functools.partial(_inproj_kernel, dq_scale=dq_scale, transposed=transposed),
        grid=(T // tm, N_SEG),
        in_specs=[
            pl.BlockSpec((tm, D), lambda i, j: (i, 0)),
            pl.BlockSpec((1, D), lambda i, j: (0, 0)),
            pl.BlockSpec((D, SEG), lambda i, j: (0, j)),
            tab_spec, tab_spec, tab_spec, tab_spec, tab_spec,
        ],
        out_specs=out_specs,
        out_shape=out_shape,
        scratch_shapes=[pltpu.VMEM((tm, D), BF16)],
        compiler_params=_params(("parallel", "arbitrary")),
        name="in_projection",
    )(x2d, g.reshape(1, D), w_bf, cr, sr, cd, sdl, sdh)


def _ret_decay_tables(chunk):
    lg = jnp.log1p(-jnp.exp2(-5.0 - jnp.arange(RET_HEADS, dtype=F32)))
    idx = jnp.arange(chunk, dtype=F32)
    rel = idx[:, None] - idx[None, :]
    dmat = jnp.where(rel[None] >= 0, jnp.exp(jnp.maximum(rel, 0.0)[None] * lg[:, None, None]), 0.0)
    q_decay = jnp.exp((idx + 1.0)[:, None] * lg[None, :])
    k_decay = jnp.exp((chunk - 1.0 - idx)[:, None] * lg[None, :])
    chunk_decay = jnp.exp(chunk * lg)
    widen = lambda t: jnp.repeat(t, RET_DK, axis=1)
    return dmat, widen(q_decay), widen(k_decay), chunk_decay


def _ret_head(q, k, v, s, dmat, qd, kd, cd, gate):
    qb, kb, vb = q.astype(BF16), k.astype(BF16), v.astype(BF16)
    att = lax.dot_general(qb, kb, _NT, preferred_element_type=F32) * dmat
    o = (jnp.dot(att.astype(BF16), vb, preferred_element_type=F32)
         + jnp.dot(qb, s.astype(BF16), preferred_element_type=F32) * qd)
    kdk = (k.astype(F32) * kd).astype(BF16)
    s_new = s * cd + lax.dot_general(kdk, vb, _TN, preferred_element_type=F32)
    y = _unit_rms(o) * (gate * jax.nn.sigmoid(gate))
    return y, s_new


def _ret_prompt_kernel(cd_ref, q_ref, k_ref, v_ref, g_ref, dmat_ref, qd_ref, kd_ref, mix_ref, s_ref):
    @pl.when(pl.program_id(1) == 0)
    def _():
        s_ref[...] = jnp.zeros_like(s_ref)

    for h in range(RET_HEADS):
        sl = slice(h * RET_DK, (h + 1) * RET_DK)
        y, s_new = _ret_head(q_ref[:, sl], k_ref[:, sl], v_ref[:, sl], s_ref[0, h], dmat_ref[h],
                             qd_ref[:, sl], kd_ref[:, sl], cd_ref[h], g_ref[:, sl])
        s_ref[0, h] = s_new
        mix_ref[:, sl] = y.astype(mix_ref.dtype)


def _retention_prompt(rq, rk, rv, rg, batch, seq, chunk):
    n = seq // chunk
    dmat, qd, kd, cd = _ret_decay_tables(chunk)
    row = pl.BlockSpec((chunk, SEG), lambda b, c: (b * n + c, 0))
    full2 = pl.BlockSpec((chunk, SEG), lambda b, c: (0, 0))
    return pl.pallas_call(
        _ret_prompt_kernel,
        grid=(batch, n),
        in_specs=[
            pl.BlockSpec(memory_space=pltpu.SMEM),
            row, row, row, row,
            pl.BlockSpec((RET_HEADS, chunk, chunk), lambda b, c: (0, 0, 0)),
            full2, full2,
        ],
        out_specs=[row, pl.BlockSpec((1, RET_HEADS, RET_DK, RET_DV), lambda b, c: (b, 0, 0, 0))],
        out_shape=[jax.ShapeDtypeStruct((batch * seq, SEG), BF16),
                   jax.ShapeDtypeStruct((batch, RET_HEADS, RET_DK, RET_DV), F32)],
        compiler_params=_params(("parallel", "arbitrary")),
        name="retention_prompt",
    )(cd, rq, rk, rv, rg, dmat, qd, kd)


def _ret_sample_kernel(cd_ref, q_ref, k_ref, v_ref, g_ref, s0_ref, dmat_ref, qd_ref, kd_ref,
                       mix_ref, s_ref, *, group, seq):
    rows = []
    for b in range(group):
        r = slice(b * seq, (b + 1) * seq)
        heads = []
        for h in range(RET_HEADS):
            sl = slice(h * RET_DK, (h + 1) * RET_DK)
            y, s_new = _ret_head(q_ref[r, sl], k_ref[r, sl], v_ref[r, sl], s0_ref[b, h], dmat_ref[h],
                                 qd_ref[:, sl], kd_ref[:, sl], cd_ref[h], g_ref[r, sl])
            s_ref[b, h] = s_new
            heads.append(y)
        rows.append(jnp.concatenate(heads, axis=1))
    mix_ref[...] = jnp.concatenate(rows, axis=0)


def _retention_sample(rq, rk, rv, rg, state, layer, batch, seq, group):
    dmat, qd, kd, cd = _ret_decay_tables(seq)
    row = pl.BlockSpec((group * seq, SEG), lambda i: (i, 0))
    tab = pl.BlockSpec((seq, SEG), lambda i: (0, 0))
    st = pl.BlockSpec((group, RET_HEADS, RET_DK, RET_DV), lambda i: (i, 0, 0, 0))
    st_in = pl.BlockSpec((None, group, RET_HEADS, RET_DK, RET_DV), lambda i: (layer, i, 0, 0, 0))
    return pl.pallas_call(
        functools.partial(_ret_sample_kernel, group=group, seq=seq),
        grid=(batch // group,),
        in_specs=[
            pl.BlockSpec(memory_space=pltpu.SMEM),
            row, row, row, row, st_in,
            pl.BlockSpec((RET_HEADS, seq, seq), lambda i: (0, 0, 0)),
            tab, tab,
        ],
        out_specs=[row, st],
        out_shape=[jax.ShapeDtypeStruct((batch * seq, SEG), F32),
                   jax.ShapeDtypeStruct((batch, RET_HEADS, RET_DK, RET_DV), F32)],
        compiler_params=_params(("parallel",)),
        name="retention_sample",
    )(cd, rq, rk, rv, rg, state, dmat, qd, kd)


def _lambda_value(lam_ref, lam_init):
    a = jnp.sum(lam_ref[0:1, :] * lam_ref[1:2, :], axis=-1, keepdims=True)
    b = jnp.sum(lam_ref[2:3, :] * lam_ref[3:4, :], axis=-1, keepdims=True)
    return jnp.exp(a) - jnp.exp(b) + lam_init


def _softmax_update(s, m_old, l_old):
    m_new = jnp.maximum(m_old, jnp.max(s, axis=-1, keepdims=True))
    alpha = jnp.exp(m_old - m_new)
    p = jnp.exp(s - m_new)
    l_new = alpha * l_old + jnp.sum(p, axis=-1, keepdims=True)
    return p, alpha, m_new, l_new


def _diff_prompt_kernel(qi_tab, ki_tab, lam_ref, qt_ref, k_ref, vt_ref, g_ref, o_ref, m_scr, l_scr, acc_scr,
                        *, tq, tk, lam_init):
    t = pl.program_id(1)
    qi, ki = qi_tab[t], ki_tab[t]
    n_sub = 2 * DIFF_HEADS

    @pl.when(ki == 0)
    def _():
        m_scr[...] = jnp.full_like(m_scr, -jnp.inf)
        l_scr[...] = jnp.zeros_like(l_scr)
        acc_scr[...] = jnp.zeros_like(acc_scr)

    def step(masked):
        if masked:
            kpos = ki * tk + lax.broadcasted_iota(jnp.int32, (tk, tq), 0)
            qpos = qi * tq + lax.broadcasted_iota(jnp.int32, (tk, tq), 1)
            keep = qpos >= kpos
        for h in range(DIFF_HEADS):
            vt = vt_ref[h * DIFF_DV:(h + 1) * DIFF_DV, :]
            for m in range(2):
                i = 2 * h + m
                sl = slice(i * DIFF_DK, (i + 1) * DIFF_DK)
                kb = k_ref[pl.ds(i, tk, stride=n_sub), :].astype(BF16)
                for c in range(tq // MXU_COLS):
                    cs = slice(c * MXU_COLS, (c + 1) * MXU_COLS)
                    st = jnp.dot(kb, qt_ref[sl, cs], preferred_element_type=F32)
                    if masked:
                        st = jnp.where(keep[:, cs], st, -jnp.inf)
                    m_old = m_scr[i, :, cs]
                    m_new = jnp.maximum(m_old, jnp.max(st, axis=0, keepdims=True))
                    alpha = jnp.exp2(m_old - m_new)
                    p = jnp.exp2(st - m_new)
                    l_scr[i, :, cs] = alpha * l_scr[i, :, cs] + jnp.sum(p, axis=0, keepdims=True)
                    m_scr[i, :, cs] = m_new
                    acc_scr[i, :, cs] = (alpha * acc_scr[i, :, cs]
                                         + jnp.dot(vt, p.astype(BF16), preferred_element_type=F32))

    @pl.when(ki < qi)
    def _():
        step(False)

    @pl.when(ki == qi)
    def _():
        step(True)
        lam = _lambda_value(lam_ref, lam_init)
        for h in range(DIFF_HEADS):
            a, b = 2 * h, 2 * h + 1
            ot = acc_scr[a] * (1.0 / l_scr[a]) - lam * (acc_scr[b] * (1.0 / l_scr[b]))
            yt = ot * lax.rsqrt(jnp.mean(ot * ot, axis=0, keepdims=True) + EPS)
            o_ref[:, h * DIFF_DV:(h + 1) * DIFF_DV] = (yt.T * g_ref[...] * (1.0 - lam_init)).astype(o_ref.dtype)


def _diff_attention_prompt(dq_t, dk, dv_t, lam_rows, subln_g, batch, seq, lam_init, tq=512, tk=512):
    assert tq == tk
    nq = seq // tq
    pairs = [(qi, ki) for qi in range(nq) for ki in range(qi + 1)]
    qi_tab = jnp.asarray([p[0] for p in pairs], jnp.int32)
    ki_tab = jnp.asarray([p[1] for p in pairs], jnp.int32)
    n_sub = 2 * DIFF_HEADS
    grid_spec = pltpu.PrefetchScalarGridSpec(
        num_scalar_prefetch=2,
        grid=(batch, len(pairs)),
        in_specs=[
            pl.BlockSpec((4, DIFF_DK), lambda b, t, qt, kt: (0, 0)),
            pl.BlockSpec((SEG, tq), lambda b, t, qt, kt: (0, b * nq + qt[t])),
            pl.BlockSpec((tk * n_sub, DIFF_DK), lambda b, t, qt, kt: (b * nq + kt[t], 0)),
            pl.BlockSpec((SEG, tk), lambda b, t, qt, kt: (0, b * nq + kt[t])),
            pl.BlockSpec((1, DIFF_DV), lambda b, t, qt, kt: (0, 0)),
        ],
        out_specs=pl.BlockSpec((tq, SEG), lambda b, t, qt, kt: (b * nq + qt[t], 0)),
        scratch_shapes=[pltpu.VMEM((n_sub, 1, tq), F32), pltpu.VMEM((n_sub, 1, tq), F32),
                        pltpu.VMEM((n_sub, DIFF_DV, tq), F32)],
    )
    return pl.pallas_call(
        functools.partial(_diff_prompt_kernel, tq=tq, tk=tk, lam_init=lam_init),
        grid_spec=grid_spec,
        out_shape=jax.ShapeDtypeStruct((batch * seq, SEG), BF16),
        compiler_params=_params(("parallel", "arbitrary")),
        name="diff_attention_prompt",
    )(qi_tab, ki_tab, lam_rows, dq_t, dk, dv_t, subln_g.reshape(1, DIFF_DV))


def _head_match_bias(n_rows, seq, n_cols, heads):
    row_h = jnp.arange(n_rows, dtype=jnp.int32)[:, None] // seq
    col_h = jnp.arange(n_cols, dtype=jnp.int32)[None, :] % heads
    return jnp.where(row_h == col_h, 0.0, -jnp.inf).astype(F32)


def _stack_heads(x, width, offset, stride):
    return jnp.concatenate(
        [x[:, offset + h * stride: offset + h * stride + width] for h in range(DIFF_HEADS)], axis=0)


def _stacked_queries(q_ref):
    q = q_ref[...]
    return [_stack_heads(q, DIFF_DK, m * DIFF_DK, 2 * DIFF_DK).astype(BF16) for m in range(2)]


def _diff_sample_begin(m_scr, l_scr, acc_scr):
    m_scr[...] = jnp.full_like(m_scr, -jnp.inf)
    l_scr[...] = jnp.zeros_like(l_scr)
    acc_scr[...] = jnp.zeros_like(acc_scr)


def _diff_sample_pages(bias_ref, q_ref, k_refs, v_refs, m_scr, l_scr, acc_scr, *, seq):
    rows = DIFF_HEADS * seq
    kv_rows = k_refs[0].shape[0] // 2
    qs = _stacked_queries(q_ref)
    bias = bias_ref[...]
    ps, alphas = [], []
    for m in range(2):
        s = jnp.concatenate(
            [lax.dot_general(qs[m], kr[pl.ds(m, kv_rows, stride=2), :].astype(BF16), _NT,
                             preferred_element_type=F32) + bias for kr in k_refs], axis=1)
        p, alpha, m_new, l_new = _softmax_update(s, m_scr[m], l_scr[m])
        m_scr[m] = m_new
        l_scr[m] = l_new
        ps.append(p)
        alphas.append(alpha)
    p = jnp.concatenate(ps, axis=0).astype(BF16)
    for e in range(2):
        pv = None
        for n in range(len(v_refs)):
            v = v_refs[n][:, :, e * LANES:(e + 1) * LANES].reshape(kv_rows, LANES).astype(BF16)
            t = jnp.dot(p[:, n * kv_rows:(n + 1) * kv_rows], v, preferred_element_type=F32)
            pv = t if pv is None else pv + t
        for m in range(2):
            cols = slice(e * LANES, (e + 1) * LANES)
            acc_scr[m, :, cols] = alphas[m] * acc_scr[m, :, cols] + pv[m * rows:(m + 1) * rows]


def _diff_sample_finish(lam_ref, q_ref, kn_ref, vn_ref, g_ref, o_ref, m_scr, l_scr, acc_scr, *, seq, lam_init):
    qs = _stacked_queries(q_ref)
    kn, vn = kn_ref[...], vn_ref[...]
    causal = (lax.broadcasted_iota(jnp.int32, (seq, seq), 0)
              >= lax.broadcasted_iota(jnp.int32, (seq, seq), 1))
    accs, ls = [], []
    for m in range(2):
        s = jnp.concatenate(
            [jnp.where(causal,
                       lax.dot_general(qs[m][h * seq:(h + 1) * seq],
                                       kn[:, (2 * h + m) * DIFF_DK:(2 * h + m + 1) * DIFF_DK].astype(BF16),
                                       _NT, preferred_element_type=F32),
                       -jnp.inf) for h in range(DIFF_HEADS)], axis=0)
        p, alpha, _, l_new = _softmax_update(s, m_scr[m], l_scr[m])
        pn = p.astype(BF16)
        pv = jnp.concatenate(
            [jnp.dot(pn[h * seq:(h + 1) * seq], vn[:, h * DIFF_DV:(h + 1) * DIFF_DV].astype(BF16),
                     preferred_element_type=F32) for h in range(DIFF_HEADS)], axis=0)
        accs.append(alpha * acc_scr[m] + pv)
        ls.append(l_new)
    lam = _lambda_value(lam_ref, lam_init)
    o = accs[0] / ls[0] - lam * (accs[1] / ls[1])
    y = _unit_rms(o) * g_ref[...] * (1.0 - lam_init)
    o_ref[...] = jnp.concatenate([y[h * seq:(h + 1) * seq] for h in range(DIFF_HEADS)], axis=1)


def _mlp_and_sample_attention_kernel(pt_ref, x_ref, gpre_ref, wup_ref, wdn_ref, gpost_ref,
                                     lam_ref, bias_ref, q_ref, kn_ref, vn_ref, g_ref, ck_hbm, cv_hbm,
                                     y_ref, o_ref, h_scr, m_scr, l_scr, acc_scr, k_buf, v_buf, sem,
                                     *, layer, n_pages_step, attn_steps, seq, lam_init):
    j, n_j = pl.program_id(1), pl.num_programs(1)
    g = pl.program_id(0) * n_j + j
    total = pl.num_programs(0) * n_j
    step = lax.rem(g, attn_steps)
    slot = lax.rem(g, 2)

    def page_copies(at_step, into):
        b, s = lax.div(at_step, attn_steps), lax.rem(at_step, attn_steps)
        copies = []
        for n in range(n_pages_step):
            page = pt_ref[b, s * n_pages_step + n]
            copies.append(pltpu.make_async_copy(ck_hbm.at[layer, page], k_buf.at[into, n], sem.at[into]))
            copies.append(pltpu.make_async_copy(cv_hbm.at[layer, page], v_buf.at[into, n], sem.at[into]))
        return copies

    def start_all(copies):
        for c in copies:
            c.start(priority=PAGE_DMA_PRIORITY)

    pl.when(g == 0)(lambda: start_all(page_copies(g, slot)))
    pl.when(g + 1 < total)(lambda: start_all(page_copies(g + 1, 1 - slot)))
    pl.when(j == 0)(lambda: _mlp_begin(x_ref, gpre_ref, y_ref, h_scr))
    pl.when(step == 0)(lambda: _diff_sample_begin(m_scr, l_scr, acc_scr))
    for c in page_copies(g, slot):
        c.wait()
    k_refs = [k_buf.at[slot, n] for n in range(n_pages_step)]
    v_refs = [v_buf.at[slot, n] for n in range(n_pages_step)]
    _mlp_tile(wup_ref, wdn_ref, y_ref, h_scr)
    _diff_sample_pages(bias_ref, q_ref, k_refs, v_refs, m_scr, l_scr, acc_scr, seq=seq)
    pl.when(j == n_j - 1)(lambda: _mlp_finish(x_ref, gpost_ref, y_ref))
    pl.when(step == attn_steps - 1)(lambda: _diff_sample_finish(
        lam_ref, q_ref, kn_ref, vn_ref, g_ref, o_ref, m_scr, l_scr, acc_scr, seq=seq, lam_init=lam_init))


def _mlp_and_sample_attention(x2d, g_pre, w_up_bf, w_dn_bf, g_post, tm, tf,
                              dq, dk_new, dv_new, cache_k, cache_v, layer, page_table, lam_rows, subln_g,
                              batch, seq, lam_init, pages_per_step):
    T, D = x2d.shape
    FF = w_up_bf.shape[1]
    n_i, n_j = T // tm, FF // tf
    depth, n_pool, page = cache_k.shape[:3]
    n_pages = page_table.shape[1]
    attn_steps = n_pages // pages_per_step
    assert n_pages % pages_per_step == 0 and DIFF_DV == 2 * LANES
    assert n_i * n_j == batch * attn_steps, "the two jobs must have the same number of grid steps"
    k_rows = page * DIFF_HEADS * 2
    ck = cache_k.reshape(depth, n_pool, k_rows, DIFF_DK)
    rows = DIFF_HEADS * seq
    bias = _head_match_bias(rows, seq, page * DIFF_HEADS, DIFF_HEADS)

    vec = pl.BlockSpec((1, D), lambda i, j, pt: (0, 0))
    row = pl.BlockSpec((seq, SEG), lambda i, j, pt: ((i * n_j + j) // attn_steps, 0))
    hbm = pl.BlockSpec(memory_space=pl.ANY)
    grid_spec = pltpu.PrefetchScalarGridSpec(
        num_scalar_prefetch=1,
        grid=(n_i, n_j),
        in_specs=[pl.BlockSpec((tm, D), lambda i, j, pt: (i, 0)), vec,
                  pl.BlockSpec((D, tf), lambda i, j, pt: (0, j)),
                  pl.BlockSpec((tf, D), lambda i, j, pt: (j, 0)), vec,
                  pl.BlockSpec((4, DIFF_DK), lambda i, j, pt: (0, 0)),
                  pl.BlockSpec(bias.shape, lambda i, j, pt: (0, 0)),
                  row, row, row,
                  pl.BlockSpec((1, DIFF_DV), lambda i, j, pt: (0, 0)),
                  hbm, hbm],
        out_specs=[pl.BlockSpec((tm, D), lambda i, j, pt: (i, 0)), row],
        scratch_shapes=[pltpu.VMEM((tm, D), BF16),
                        pltpu.VMEM((2, rows, 1), F32), pltpu.VMEM((2, rows, 1), F32),
                        pltpu.VMEM((2, rows, DIFF_DV), F32),
                        pltpu.VMEM((2, pages_per_step, k_rows, DIFF_DK), F32),
                        pltpu.VMEM((2, pages_per_step, page, DIFF_HEADS, DIFF_DV), F32),
                        pltpu.SemaphoreType.DMA((2,))],
    )
    return pl.pallas_call(
        functools.partial(_mlp_and_sample_attention_kernel, layer=layer, n_pages_step=pages_per_step,
                          attn_steps=attn_steps, seq=seq, lam_init=lam_init),
        grid_spec=grid_spec,
        out_shape=[jax.ShapeDtypeStruct((T, D), F32), jax.ShapeDtypeStruct((batch * seq, SEG), F32)],
        compiler_params=_params(("arbitrary", "arbitrary")),
        name="mlp_and_sample_attention",
    )(page_table, x2d, g_pre.reshape(1, D), w_up_bf, w_dn_bf, g_post.reshape(1, D),
      lam_rows, bias, dq, dk_new, dv_new, subln_g.reshape(1, DIFF_DV), ck, cache_v)


def _norm_matmul_kernel(x_ref, g_ref, w_ref, o_ref, *, scale):
    h = _rms(x_ref[...], g_ref[...]).astype(BF16)
    z = jnp.dot(h, w_ref[...], preferred_element_type=F32)
    if scale != 1.0:
        z = z * scale
    o_ref[...] = z.astype(o_ref.dtype)


def _norm_matmul(x2d, g, w_bf, tm, out_dtype, scale=1.0):
    T, D = x2d.shape
    N = w_bf.shape[1]
    return pl.pallas_call(
        functools.partial(_norm_matmul_kernel, scale=scale),
        grid=(T // tm,),
        in_specs=[pl.BlockSpec((tm, D), lambda i: (i, 0)),
                  pl.BlockSpec((1, D), lambda i: (0, 0)),
                  pl.BlockSpec((D, N), lambda i: (0, 0))],
        out_specs=pl.BlockSpec((tm, N), lambda i: (i, 0)),
        out_shape=jax.ShapeDtypeStruct((T, N), out_dtype),
        compiler_params=_params(("parallel",)),
        name="norm_matmul",
    )(x2d, g.reshape(1, D), w_bf)


def _matmul_norm_res_kernel(*refs, n_in):
    a_refs = refs[:n_in]
    w_refs = refs[n_in:2 * n_in]
    g_ref, res_ref, o_ref = refs[2 * n_in:]
    z = None
    for a_ref, w_ref in zip(a_refs, w_refs):
        t = jnp.dot(a_ref[...].astype(BF16), w_ref[...], preferred_element_type=F32)
        z = t if z is None else z + t
    o_ref[...] = res_ref[...] + _rms(z, g_ref[...])


def _matmul_norm_residual(acts, ws_bf, g, res, tm):
    T, D = res.shape
    n_in = len(acts)
    in_specs = ([pl.BlockSpec((tm, a.shape[1]), lambda i: (i, 0)) for a in acts]
                + [pl.BlockSpec(w.shape, lambda i: (0, 0)) for w in ws_bf]
                + [pl.BlockSpec((1, D), lambda i: (0, 0)), pl.BlockSpec((tm, D), lambda i: (i, 0))])
    return pl.pallas_call(
        functools.partial(_matmul_norm_res_kernel, n_in=n_in),
        grid=(T // tm,),
        in_specs=in_specs,
        out_specs=pl.BlockSpec((tm, D), lambda i: (i, 0)),
        out_shape=jax.ShapeDtypeStruct((T, D), F32),
        compiler_params=_params(("parallel",)),
        name="matmul_norm_residual",
    )(*acts, *ws_bf, g.reshape(1, D), res)


def _softmax_rows(s):
    e = jnp.exp(s - jnp.max(s, axis=-1, keepdims=True))
    return e / jnp.sum(e, axis=-1, keepdims=True)


def _cross_sample_kernel(bias_ref, q_ref, mk_ref, mv_ref, o_ref, *, group, seq):
    bias = bias_ref[...]
    outs = []
    for b in range(group):
        qb = q_ref[b * seq:(b + 1) * seq, :]
        qh = jnp.concatenate([qb[:, h * MEM_DH:(h + 1) * MEM_DH] for h in range(MEM_HEADS)], axis=0)
        s = lax.dot_general(qh.astype(BF16), mk_ref[b].astype(BF16), _NT, preferred_element_type=F32) + bias
        p = _softmax_rows(s).astype(BF16)
        o = jnp.dot(p, mv_ref[b].astype(BF16), preferred_element_type=F32)
        outs.append(jnp.concatenate([o[h * seq:(h + 1) * seq] for h in range(MEM_HEADS)], axis=1))
    o_ref[...] = jnp.concatenate(outs, axis=0)


def _cross_attention_sample(q, mem_k, mem_v, layer, batch, seq, group):
    depth, _, M = mem_k.shape[:3]
    W = MEM_HEADS * MEM_DH
    mk = mem_k.reshape(depth, batch, M * MEM_HEADS, MEM_DH)
    mv = mem_v.reshape(depth, batch, M * MEM_HEADS, MEM_DH)
    bias = _head_match_bias(MEM_HEADS * seq, seq, M * MEM_HEADS, MEM_HEADS)
    kv = pl.BlockSpec((None, group, M * MEM_HEADS, MEM_DH), lambda i: (layer, i, 0, 0))
    row = pl.BlockSpec((group * seq, W), lambda i: (i, 0))
    return pl.pallas_call(
        functools.partial(_cross_sample_kernel, group=group, seq=seq),
        grid=(batch // group,),
        in_specs=[pl.BlockSpec(bias.shape, lambda i: (0, 0)), row, kv, kv],
        out_specs=row,
        out_shape=jax.ShapeDtypeStruct((batch * seq, W), F32),
        compiler_params=_params(("parallel",)),
        name="cross_attention_sample",
    )(bias, q, mk, mv)


def _mlp_begin(x_ref, gpre_ref, o_ref, h_scr):
    h_scr[...] = _rms(x_ref[...], gpre_ref[...]).astype(BF16)
    o_ref[...] = jnp.zeros_like(o_ref)


def _mlp_tile(wup_ref, wdn_ref, o_ref, h_scr):
    u = jnp.maximum(jnp.dot(h_scr[...], wup_ref[...], preferred_element_type=F32), 0.0)
    o_ref[...] += jnp.dot((u * u).astype(BF16), wdn_ref[...], preferred_element_type=F32)


def _mlp_finish(x_ref, gpost_ref, o_ref):
    o_ref[...] = x_ref[...] + _rms(o_ref[...], gpost_ref[...])


def _mlp_kernel(x_ref, gpre_ref, wup_ref, wdn_ref, gpost_ref, o_ref, h_scr):
    j = pl.program_id(1)
    pl.when(j == 0)(lambda: _mlp_begin(x_ref, gpre_ref, o_ref, h_scr))
    _mlp_tile(wup_ref, wdn_ref, o_ref, h_scr)
    pl.when(j == pl.num_programs(1) - 1)(lambda: _mlp_finish(x_ref, gpost_ref, o_ref))


def _mlp(x2d, g_pre, w_up_bf, w_dn_bf, g_post, tm, tf):
    T, D = x2d.shape
    FF = w_up_bf.shape[1]
    vec = pl.BlockSpec((1, D), lambda i, j: (0, 0))
    return pl.pallas_call(
        _mlp_kernel,
        grid=(T // tm, FF // tf),
        in_specs=[pl.BlockSpec((tm, D), lambda i, j: (i, 0)), vec,
                  pl.BlockSpec((D, tf), lambda i, j: (0, j)),
                  pl.BlockSpec((tf, D), lambda i, j: (j, 0)), vec],
        out_specs=pl.BlockSpec((tm, D), lambda i, j: (i, 0)),
        out_shape=jax.ShapeDtypeStruct((T, D), F32),
        scratch_shapes=[pltpu.VMEM((tm, D), BF16)],
        compiler_params=_params(("parallel", "arbitrary")),
        name="mlp",
    )(x2d, g_pre.reshape(1, D), w_up_bf, w_dn_bf, g_post.reshape(1, D))


def _position_tables(pos, tm):
    reps = max(1, tm // pos.shape[0])
    pos = jnp.tile(pos, reps)
    cr, srl, srh = _rope_tables(pos, RET_DK, RET_THETA)
    cd, sdl, sdh = _rope_tables(pos, ROPE_DIM, ROPE_THETA)
    return (cr, srl + srh, cd, sdl, sdh), pos.shape[0] // tm


def _layer_tail(x2d, mix_parts, wts, cross_fn, tm):
    (w_out_parts, g_mix_post, g_mem_pre, w_mem_q, w_mem_o, g_mem_post,
     g_mlp_pre, w_up, w_down, g_mlp_post) = wts
    x1 = _matmul_norm_residual(mix_parts, w_out_parts, g_mix_post, x2d, tm)
    q = cross_fn(x1, g_mem_pre, w_mem_q)
    x2 = _matmul_norm_residual([q], [w_mem_o], g_mem_post, x1, tm)
    return _mlp(x2, g_mlp_pre, w_up, w_down, g_mlp_post, tm, MLP_FF_TILE)


def _out_cross_kernel(mr_ref, md_ref, x_ref, wo1_ref, wo2_ref, g1_ref, gpre_ref, wq_ref, mk_ref, mv_ref,
                      wmo_ref, g2_ref, o_ref):
    z = (jnp.dot(mr_ref[...], wo1_ref[...], preferred_element_type=F32)
         + jnp.dot(md_ref[...], wo2_ref[...], preferred_element_type=F32))
    x1 = x_ref[...] + _rms(z, g1_ref[...])
    h = _rms(x1, gpre_ref[...]).astype(BF16)
    q = (jnp.dot(h, wq_ref[...], preferred_element_type=F32) * (MEM_DH ** -0.5)).astype(BF16)
    outs = []
    for hd in range(MEM_HEADS):
        sl = slice(hd * MEM_DH, (hd + 1) * MEM_DH)
        s = lax.dot_general(q[:, sl], mk_ref[0, :, sl], _NT, preferred_element_type=F32)
        p = _softmax_rows(s).astype(BF16)
        outs.append(jnp.dot(p, mv_ref[0, :, sl], preferred_element_type=F32).astype(BF16))
    y = jnp.dot(jnp.concatenate(outs, axis=1), wmo_ref[...], preferred_element_type=F32)
    o_ref[...] = x1 + _rms(y, g2_ref[...])


def _out_proj_cross_block(mix_parts, x2d, wts, mk_bf, mv_bf, batch, seq, tm):
    (w_out_parts, g_mix_post, g_mem_pre, w_mem_q, w_mem_o, g_mem_post) = wts[:6]
    T, D = x2d.shape
    n = seq // tm
    M, W = mk_bf.shape[1], mk_bf.shape[2]
    row = lambda width: pl.BlockSpec((tm, width), lambda b, i: (b * n + i, 0))
    const = lambda shape: pl.BlockSpec(shape, lambda b, i: (0,) * len(shape), pipeline_mode=pl.Buffered(1))
    kv = pl.BlockSpec((1, M, W), lambda b, i: (b, 0, 0))
    vec = lambda g: g.reshape(1, D)
    return pl.pallas_call(
        _out_cross_kernel,
        grid=(batch, n),
        in_specs=[row(mix_parts[0].shape[1]), row(mix_parts[1].shape[1]), row(D),
                  const(w_out_parts[0].shape), const(w_out_parts[1].shape), const((1, D)), const((1, D)),
                  const(w_mem_q.shape), kv, kv, const(w_mem_o.shape), const((1, D))],
        out_specs=row(D),
        out_shape=jax.ShapeDtypeStruct((T, D), F32),
        compiler_params=_params(("parallel", "parallel")),
        name="out_proj_cross_block",
    )(mix_parts[0], mix_parts[1], x2d, w_out_parts[0], w_out_parts[1], vec(g_mix_post), vec(g_mem_pre),
      w_mem_q, mk_bf, mv_bf, w_mem_o, vec(g_mem_post))


def kernel(x_prompt, x_sample, mem_prompt, state_ret, cache_diff_k, cache_diff_v, cache_mem_k, cache_mem_v, page_table, w_in, w_out, diff_lambda_q1, diff_lambda_k1, diff_lambda_q2, diff_lambda_k2, diff_subln_g, norm_mix_pre, norm_mix_post, norm_mem_pre, norm_mem_post, norm_mlp_pre, norm_mlp_post, mem_norm_g, w_mem_q, w_mem_k, w_mem_v, w_mem_o, w_mlp_up, w_mlp_down):
    depth = w_in.shape[0]
    B, L_p, D = x_prompt.shape
    B_s, L_s, _ = x_sample.shape
    n_pages, page = page_table.shape[1], cache_diff_k.shape[2]
    past_len = n_pages * page
    M = mem_prompt.shape[1]
    W_MEM = MEM_HEADS * MEM_DH
    TM = 512
    ret_chunk_p = math.gcd(L_p, RET_CHUNK)

    tabs_p, per_p = _position_tables(jnp.arange(L_p, dtype=F32), TM)
    TM_IN_S = 256
    tabs_s, per_s = _position_tables(past_len + jnp.arange(L_s, dtype=F32), TM_IN_S)

    yp = x_prompt.reshape(B * L_p, D)
    ys = x_sample.reshape(B_s * L_s, D)
    mem2d = mem_prompt.reshape(B * M, D)
    outs = {k: [] for k in ("rp", "kp", "vp", "mkp", "mvp", "rs", "ks", "vs")}

    for i in range(depth):
        lam_init = 0.8 - 0.6 * math.exp(-0.3 * i)
        lam_rows = jnp.stack([diff_lambda_q1[i], diff_lambda_k1[i], diff_lambda_q2[i], diff_lambda_k2[i]])
        w_in_bf = w_in[i].astype(BF16)
        half = w_out.shape[1] // 2
        tail_w = ([w_out[i, :half].astype(BF16), w_out[i, half:].astype(BF16)], norm_mix_post[i],
                  norm_mem_pre[i], w_mem_q[i].astype(BF16), w_mem_o[i].astype(BF16), norm_mem_post[i],
                  norm_mlp_pre[i], w_mlp_up[i].astype(BF16), w_mlp_down[i].astype(BF16), norm_mlp_post[i])

        mk_p = _norm_matmul(mem2d, mem_norm_g[i], w_mem_k[i].astype(BF16), B * M, F32)
        mv_p = _norm_matmul(mem2d, mem_norm_g[i], w_mem_v[i].astype(BF16), B * M, F32)
        rq, rk, rv, rg, dq_t, dk, dv, dv_t = _in_projection(
            yp, norm_mix_pre[i], w_in_bf, tabs_p, per_p, TM, BF16, DIFF_DK ** -0.5 * LOG2E, True)
        mix_ret, s_p = _retention_prompt(rq, rk, rv, rg, B, L_p, ret_chunk_p)
        mix_diff = _diff_attention_prompt(dq_t, dk, dv_t, lam_rows, diff_subln_g[i], B, L_p, lam_init)
        mk_bf = mk_p.astype(BF16).reshape(B, M, W_MEM)
        mv_bf = mv_p.astype(BF16).reshape(B, M, W_MEM)

        x2 = _out_proj_cross_block([mix_ret, mix_diff], yp, tail_w, mk_bf, mv_bf, B, L_p, TM)
        outs["rp"].append(s_p.astype(state_ret.dtype))
        halves = DIFF_DV // LANES
        outs["kp"].append(dk.reshape(B, L_p, DIFF_HEADS, 2, DIFF_DK))
        outs["vp"].append(dv.reshape(B, L_p, halves, DIFF_HEADS, LANES).transpose(0, 1, 3, 2, 4)
                          .reshape(B, L_p, DIFF_HEADS, DIFF_DV))
        outs["mkp"].append(mk_p.reshape(B, M, MEM_HEADS, MEM_DH))
        outs["mvp"].append(mv_p.reshape(B, M, MEM_HEADS, MEM_DH))

        rq, rk, rv, rg, dq, dk, dv = _in_projection(ys, norm_mix_pre[i], w_in_bf, tabs_s, per_s, TM_IN_S, F32,
                                                    DIFF_DK ** -0.5, False)
        mix_ret, s_s = _retention_sample(rq, rk, rv, rg, state_ret, i, B_s, L_s, 4)
        yp, mix_diff = _mlp_and_sample_attention(
            x2, norm_mlp_pre[i], tail_w[7], tail_w[8], norm_mlp_post[i], TM, MLP_FF_TILE_FUSED,
            dq, dk, dv, cache_diff_k, cache_diff_v, i, page_table, lam_rows, diff_subln_g[i],
            B_s, L_s, lam_init, PAGES_PER_STEP)
        def cross_s(x1, g, wq):
            q = _norm_matmul(x1, g, wq, TM, F32, scale=MEM_DH ** -0.5)
            return _cross_attention_sample(q, cache_mem_k, cache_mem_v, i, B_s, L_s, 8)

        ys = _layer_tail(ys, [mix_ret, mix_diff], tail_w, cross_s, TM)
        outs["rs"].append(s_s.astype(state_ret.dtype))
        outs["ks"].append(dk.reshape(B_s, L_s, DIFF_HEADS, 2, DIFF_DK))
        outs["vs"].append(dv.reshape(B_s, L_s, DIFF_HEADS, DIFF_DV))

    st = lambda k: outs[k][0][None] if depth == 1 else jnp.stack(outs[k])
    return (yp.reshape(B, L_p, D), ys.reshape(B_s, L_s, D), st("rp"), st("kp"), st("vp"),
            st("mkp"), st("mvp"), st("rs"), st("ks"), st("vs"))
```

```python
import functools
import math

import jax
import jax.numpy as jnp
from jax import lax
from jax.experimental import pallas as pl
from jax.experimental.pallas import tpu as pltpu

F32 = jnp.float32
BF16 = jnp.bfloat16

LANES = 128
MXU_COLS = 256
RET_HEADS = 8
RET_DK = 128
RET_DV = 128
RET_THETA = 10000.0
RET_CHUNK = 2 * 128
DIFF_HEADS = 4
DIFF_DK = 128
DIFF_DV = 256
ROPE_THETA = 500000.0
ROPE_DIM = DIFF_DK // 4
MEM_HEADS = 4
MEM_DH = 128
Q_BLOCK = 128
EPS = 1e-6
LOG2E = math.log2(math.e)
SEG = 1024
N_SEG = 7
MLP_FF_TILE = 1024
MLP_FF_TILE_FUSED = 512
PAGES_PER_STEP = 8
PAGE_DMA_PRIORITY = 1

VMEM_LIMIT = 56 * 1024 * 1024

_NT = (((1,), (1,)), ((), ()))
_TN = (((0,), (0,)), ((), ()))


def _params(sem, vmem=VMEM_LIMIT):
    return pltpu.CompilerParams(dimension_semantics=sem, vmem_limit_bytes=vmem)


def _rms(x, g):
    return x * lax.rsqrt(jnp.mean(x * x, axis=-1, keepdims=True) + EPS) * g


def _unit_rms(o):
    return o * lax.rsqrt(jnp.mean(o * o, axis=-1, keepdims=True) + EPS)


def _rope_tables(pos, rot_dim, theta):
    half = rot_dim // 2
    inv = jnp.exp(-math.log(theta) * (2.0 * jnp.arange(half, dtype=F32) / rot_dim))
    ang = pos[:, None] * inv[None, :]
    cos, sin = jnp.cos(ang), jnp.sin(ang)
    n = pos.shape[0]
    rest = LANES - rot_dim
    c = jnp.concatenate([cos, cos, jnp.ones((n, rest), F32)], axis=-1)
    s_lo = jnp.concatenate([-sin, jnp.zeros((n, LANES - half), F32)], axis=-1)
    s_hi = jnp.concatenate([jnp.zeros((n, half), F32), sin, jnp.zeros((n, rest), F32)], axis=-1)
    return c, s_lo, s_hi


def _inproj_kernel(x_ref, g_ref, w_ref, cr_ref, sr_ref, cd_ref, sdl_ref, sdh_ref, *rest,
                   dq_scale, transposed):
    if transposed:
        rq_ref, rk_ref, rv_ref, rg_ref, dq_ref, dk_ref, dv_ref, dvt_ref, h_scr = rest
    else:
        rq_ref, rk_ref, rv_ref, rg_ref, dq_ref, dk_ref, dv_ref, h_scr = rest
    j = pl.program_id(1)
    heads = [slice(h * LANES, (h + 1) * LANES) for h in range(SEG // LANES)]

    @pl.when(j == 0)
    def _():
        h_scr[...] = _rms(x_ref[...], g_ref[...]).astype(BF16)

    def project():
        return jnp.dot(h_scr[...], w_ref[...], preferred_element_type=F32)

    def ret_rot(out_ref, scale):
        z_all = project()
        c, s = cr_ref[...], sr_ref[...]
        for sl in heads:
            z = z_all[:, sl]
            r = z * c + pltpu.roll(z, RET_DK // 2, 1) * s
            if scale != 1.0:
                r = r * scale
            out_ref[:, sl] = r.astype(out_ref.dtype)

    def diff_rot(z):
        half = ROPE_DIM // 2
        return (z * cd_ref[...] + pltpu.roll(z, LANES - half, 1) * sdl_ref[...]
                + pltpu.roll(z, half, 1) * sdh_ref[...])

    @pl.when(j == 0)
    def _():
        ret_rot(rq_ref, 1.0)

    @pl.when(j == 1)
    def _():
        ret_rot(rk_ref, RET_DK ** -0.5)

    @pl.when(j == 2)
    def _():
        rv_ref[...] = project().astype(rv_ref.dtype)

    @pl.when(j == 3)
    def _():
        rg_ref[...] = project()

    @pl.when(j == 4)
    def _():
        z_all = project()
        for sl in heads:
            r = diff_rot(z_all[:, sl]) * dq_scale
            if transposed:
                dq_ref[sl, :] = r.T.astype(dq_ref.dtype)
            else:
                dq_ref[:, sl] = r.astype(dq_ref.dtype)

    tm = x_ref.shape[0]
    n_heads = len(heads)

    @pl.when(j == 5)
    def _():
        z_all = project()
        for i, sl in enumerate(heads):
            r = diff_rot(z_all[:, sl])
            if transposed:
                dk_ref[pl.ds(i, tm, stride=n_heads), :] = r
            else:
                dk_ref[:, sl] = r

    @pl.when(j == 6)
    def _():
        z_all = project()
        if transposed:
            for i, sl in enumerate(heads):
                h, e = divmod(i, DIFF_DV // LANES)
                dv_ref[pl.ds(e * DIFF_HEADS + h, tm, stride=n_heads), :] = z_all[:, sl]
                dvt_ref[sl, :] = z_all[:, sl].T.astype(dvt_ref.dtype)
        else:
            dv_ref[...] = z_all


def _in_projection(x2d, g, w_bf, tabs, period_blocks, tm, act_dtype, dq_scale, transposed):
    T, D = x2d.shape
    cr, sr, cd, sdl, sdh = tabs
    tab_spec = pl.BlockSpec((tm, LANES), lambda i, j: (i % period_blocks, 0))
    row_spec = pl.BlockSpec((tm, SEG), lambda i, j: (i, 0))
    col_spec = pl.BlockSpec((SEG, tm), lambda i, j: (0, i))
    sds = lambda dt: jax.ShapeDtypeStruct((T, SEG), dt)
    sds_t = lambda dt: jax.ShapeDtypeStruct((SEG, T), dt)
    n_heads = SEG // LANES
    cache_spec = pl.BlockSpec((tm * n_heads, LANES), lambda i, j: (i, 0))
    sds_c = jax.ShapeDtypeStruct((T * n_heads, LANES), F32)
    if transposed:
        out_specs = [row_spec] * 4 + [col_spec, cache_spec, cache_spec, col_spec]
        out_shape = [sds(act_dtype), sds(act_dtype), sds(act_dtype), sds(F32),
                     sds_t(act_dtype), sds_c, sds_c, sds_t(act_dtype)]
    else:
        out_specs = [pl.BlockSpec((tm, SEG), lambda i, j: (i, 0), pipeline_mode=pl.Buffered(1))] * N_SEG
        out_shape = [sds(act_dtype), sds(act_dtype), sds(act_dtype), sds(F32), sds(act_dtype), sds(F32), sds(F32)]
    return pl.pallas_call(
        functools.partial(_inproj_kernel, dq_scale=dq_scale, transposed=transposed),
        grid=(T // tm, N_SEG),
        in_specs=[
            pl.BlockSpec((tm, D), lambda i, j: (i, 0)),
            pl.BlockSpec((1, D), lambda i, j: (0, 0)),
            pl.BlockSpec((D, SEG), lambda i, j: (0, j)),
            tab_spec, tab_spec, tab_spec, tab_spec, tab_spec,
        ],
        out_specs=out_specs,
        out_shape=out_shape,
        scratch_shapes=[pltpu.VMEM((tm, D), BF16)],
        compiler_params=_params(("parallel", "arbitrary")),
        name="in_projection",
    )(x2d, g.reshape(1, D), w_bf, cr, sr, cd, sdl, sdh)


def _ret_decay_tables(chunk):
    lg = jnp.log1p(-jnp.exp2(-5.0 - jnp.arange(RET_HEADS, dtype=F32)))
    idx = jnp.arange(chunk, dtype=F32)
    rel = idx[:, None] - idx[None, :]
    dmat = jnp.where(rel[None] >= 0, jnp.exp(jnp.maximum(rel, 0.0)[None] * lg[:, None, None]), 0.0)
    q_decay = jnp.exp((idx + 1.0)[:, None] * lg[None, :])
    k_decay = jnp.exp((chunk - 1.0 - idx)[:, None] * lg[None, :])
    chunk_decay = jnp.exp(chunk * lg)
    widen = lambda t: jnp.repeat(t, RET_DK, axis=1)
    return dmat, widen(q_decay), widen(k_decay), chunk_decay


def _ret_head(q, k, v, s, dmat, qd, kd, cd, gate):
    qb, kb, vb = q.astype(BF16), k.astype(BF16), v.astype(BF16)
    att = lax.dot_general(qb, kb, _NT, preferred_element_type=F32) * dmat
    o = (jnp.dot(att.astype(BF16), vb, preferred_element_type=F32)
         + jnp.dot(qb, s.astype(BF16), preferred_element_type=F32) * qd)
    kdk = (k.astype(F32) * kd).astype(BF16)
    s_new = s * cd + lax.dot_general(kdk, vb, _TN, preferred_element_type=F32)
    y = _unit_rms(o) * (gate * jax.nn.sigmoid(gate))
    return y, s_new


def _ret_prompt_kernel(cd_ref, q_ref, k_ref, v_ref, g_ref, dmat_ref, qd_ref, kd_ref, mix_ref, s_ref):
    @pl.when(pl.program_id(1) == 0)
    def _():
        s_ref[...] = jnp.zeros_like(s_ref)

    for h in range(RET_HEADS):
        sl = slice(h * RET_DK, (h + 1) * RET_DK)
        y, s_new = _ret_head(q_ref[:, sl], k_ref[:, sl], v_ref[:, sl], s_ref[0, h], dmat_ref[h],
                             qd_ref[:, sl], kd_ref[:, sl], cd_ref[h], g_ref[:, sl])
        s_ref[0, h] = s_new
        mix_ref[:, sl] = y.astype(mix_ref.dtype)


def _retention_prompt(rq, rk, rv, rg, batch, seq, chunk):
    n = seq // chunk
    dmat, qd, kd, cd = _ret_decay_tables(chunk)
    row = pl.BlockSpec((chunk, SEG), lambda b, c: (b * n + c, 0))
    full2 = pl.BlockSpec((chunk, SEG), lambda b, c: (0, 0))
    return pl.pallas_call(
        _ret_prompt_kernel,
        grid=(batch, n),
        in_specs=[
            pl.BlockSpec(memory_space=pltpu.SMEM),
            row, row, row, row,
            pl.BlockSpec((RET_HEADS, chunk, chunk), lambda b, c: (0, 0, 0)),
            full2, full2,
        ],
        out_specs=[row, pl.BlockSpec((1, RET_HEADS, RET_DK, RET_DV), lambda b, c: (b, 0, 0, 0))],
        out_shape=[jax.ShapeDtypeStruct((batch * seq, SEG), BF16),
                   jax.ShapeDtypeStruct((batch, RET_HEADS, RET_DK, RET_DV), F32)],
        compiler_params=_params(("parallel", "arbitrary")),
        name="retention_prompt",
    )(cd, rq, rk, rv, rg, dmat, qd, kd)


def _ret_sample_kernel(cd_ref, q_ref, k_ref, v_ref, g_ref, s0_ref, dmat_ref, qd_ref, kd_ref,
                       mix_ref, s_ref, *, group, seq):
    rows = []
    for b in range(group):
        r = slice(b * seq, (b + 1) * seq)
        heads = []
        for h in range(RET_HEADS):
            sl = slice(h * RET_DK, (h + 1) * RET_DK)
            y, s_new = _ret_head(q_ref[r, sl], k_ref[r, sl], v_ref[r, sl], s0_ref[b, h], dmat_ref[h],
                                 qd_ref[:, sl], kd_ref[:, sl], cd_ref[h], g_ref[r, sl])
            s_ref[b, h] = s_new
            heads.append(y)
        rows.append(jnp.concatenate(heads, axis=1))
    mix_ref[...] = jnp.concatenate(rows, axis=0)


def _retention_sample(rq, rk, rv, rg, state, layer, batch, seq, group):
    dmat, qd, kd, cd = _ret_decay_tables(seq)
    row = pl.BlockSpec((group * seq, SEG), lambda i: (i, 0))
    tab = pl.BlockSpec((seq, SEG), lambda i: (0, 0))
    st = pl.BlockSpec((group, RET_HEADS, RET_DK, RET_DV), lambda i: (i, 0, 0, 0))
    st_in = pl.BlockSpec((None, group, RET_HEADS, RET_DK, RET_DV), lambda i: (layer, i, 0, 0, 0))
    return pl.pallas_call(
        functools.partial(_ret_sample_kernel, group=group, seq=seq),
        grid=(batch // group,),
        in_specs=[
            pl.BlockSpec(memory_space=pltpu.SMEM),
            row, row, row, row, st_in,
            pl.BlockSpec((RET_HEADS, seq, seq), lambda i: (0, 0, 0)),
            tab, tab,
        ],
        out_specs=[row, st],
        out_shape=[jax.ShapeDtypeStruct((batch * seq, SEG), F32),
                   jax.ShapeDtypeStruct((batch, RET_HEADS, RET_DK, RET_DV), F32)],
        compiler_params=_params(("parallel",)),
        name="retention_sample",
    )(cd, rq, rk, rv, rg, state, dmat, qd, kd)


def _lambda_value(lam_ref, lam_init):
    a = jnp.sum(lam_ref[0:1, :] * lam_ref[1:2, :], axis=-1, keepdims=True)
    b = jnp.sum(lam_ref[2:3, :] * lam_ref[3:4, :], axis=-1, keepdims=True)
    return jnp.exp(a) - jnp.exp(b) + lam_init


def _softmax_update(s, m_old, l_old):
    m_new = jnp.maximum(m_old, jnp.max(s, axis=-1, keepdims=True))
    alpha = jnp.exp(m_old - m_new)
    p = jnp.exp(s - m_new)
    l_new = alpha * l_old + jnp.sum(p, axis=-1, keepdims=True)
    return p, alpha, m_new, l_new


def _diff_prompt_kernel(qi_tab, ki_tab, lam_ref, qt_ref, k_ref, vt_ref, g_ref, o_ref, m_scr, l_scr, acc_scr,
                        *, tq, tk, lam_init):
    t = pl.program_id(1)
    qi, ki = qi_tab[t], ki_tab[t]
    n_sub = 2 * DIFF_HEADS

    @pl.when(ki == 0)
    def _():
        m_scr[...] = jnp.full_like(m_scr, -jnp.inf)
        l_scr[...] = jnp.zeros_like(l_scr)
        acc_scr[...] = jnp.zeros_like(acc_scr)

    def step(masked):
        if masked:
            kpos = ki * tk + lax.broadcasted_iota(jnp.int32, (tk, tq), 0)
            qpos = qi * tq + lax.broadcasted_iota(jnp.int32, (tk, tq), 1)
            keep = qpos >= kpos
        for h in range(DIFF_HEADS):
            vt = vt_ref[h * DIFF_DV:(h + 1) * DIFF_DV, :]
            for m in range(2):
                i = 2 * h + m
                sl = slice(i * DIFF_DK, (i + 1) * DIFF_DK)
                kb = k_ref[pl.ds(i, tk, stride=n_sub), :].astype(BF16)
                for c in range(tq // MXU_COLS):
                    cs = slice(c * MXU_COLS, (c + 1) * MXU_COLS)
                    st = jnp.dot(kb, qt_ref[sl, cs], preferred_element_type=F32)
                    if masked:
                        st = jnp.where(keep[:, cs], st, -jnp.inf)
                    m_old = m_scr[i, :, cs]
                    m_new = jnp.maximum(m_old, jnp.max(st, axis=0, keepdims=True))
                    alpha = jnp.exp2(m_old - m_new)
                    p = jnp.exp2(st - m_new)
                    l_scr[i, :, cs] = alpha * l_scr[i, :, cs] + jnp.sum(p, axis=0, keepdims=True)
                    m_scr[i, :, cs] = m_new
                    acc_scr[i, :, cs] = (alpha * acc_scr[i, :, cs]
                                         + jnp.dot(vt, p.astype(BF16), preferred_element_type=F32))

    @pl.when(ki < qi)
    def _():
        step(False)

    @pl.when(ki == qi)
    def _():
        step(True)
        lam = _lambda_value(lam_ref, lam_init)
        for h in range(DIFF_HEADS):
            a, b = 2 * h, 2 * h + 1
            ot = acc_scr[a] * (1.0 / l_scr[a]) - lam * (acc_scr[b] * (1.0 / l_scr[b]))
            yt = ot * lax.rsqrt(jnp.mean(ot * ot, axis=0, keepdims=True) + EPS)
            o_ref[:, h * DIFF_DV:(h + 1) * DIFF_DV] = (yt.T * g_ref[...] * (1.0 - lam_init)).astype(o_ref.dtype)


def _diff_attention_prompt(dq_t, dk, dv_t, lam_rows, subln_g, batch, seq, lam_init, tq=512, tk=512):
    assert tq == tk
    nq = seq // tq
    pairs = [(qi, ki) for qi in range(nq) for ki in range(qi + 1)]
    qi_tab = jnp.asarray([p[0] for p in pairs], jnp.int32)
    ki_tab = jnp.asarray([p[1] for p in pairs], jnp.int32)
    n_sub = 2 * DIFF_HEADS
    grid_spec = pltpu.PrefetchScalarGridSpec(
        num_scalar_prefetch=2,
        grid=(batch, len(pairs)),
        in_specs=[
            pl.BlockSpec((4, DIFF_DK), lambda b, t, qt, kt: (0, 0)),
            pl.BlockSpec((SEG, tq), lambda b, t, qt, kt: (0, b * nq + qt[t])),
            pl.BlockSpec((tk * n_sub, DIFF_DK), lambda b, t, qt, kt: (b * nq + kt[t], 0)),
            pl.BlockSpec((SEG, tk), lambda b, t, qt, kt: (0, b * nq + kt[t])),
            pl.BlockSpec((1, DIFF_DV), lambda b, t, qt, kt: (0, 0)),
        ],
        out_specs=pl.BlockSpec((tq, SEG), lambda b, t, qt, kt: (b * nq + qt[t], 0)),
        scratch_shapes=[pltpu.VMEM((n_sub, 1, tq), F32), pltpu.VMEM((n_sub, 1, tq), F32),
                        pltpu.VMEM((n_sub, DIFF_DV, tq), F32)],
    )
    return pl.pallas_call(
        functools.partial(_diff_prompt_kernel, tq=tq, tk=tk, lam_init=lam_init),
        grid_spec=grid_spec,
        out_shape=jax.ShapeDtypeStruct((batch * seq, SEG), BF16),
        compiler_params=_params(("parallel", "arbitrary")),
        name="diff_attention_prompt",
    )(qi_tab, ki_tab, lam_rows, dq_t, dk, dv_t, subln_g.reshape(1, DIFF_DV))


def _head_match_bias(n_rows, seq, n_cols, heads):
    row_h = jnp.arange(n_rows, dtype=jnp.int32)[:, None] // seq
    col_h = jnp.arange(n_cols, dtype=jnp.int32)[None, :] % heads
    return jnp.where(row_h == col_h, 0.0, -jnp.inf).astype(F32)


def _stack_heads(x, width, offset, stride):
    return jnp.concatenate(
        [x[:, offset + h * stride: offset + h * stride + width] for h in range(DIFF_HEADS)], axis=0)


def _stacked_queries(q_ref):
    q = q_ref[...]
    return [_stack_heads(q, DIFF_DK, m * DIFF_DK, 2 * DIFF_DK).astype(BF16) for m in range(2)]


def _diff_sample_begin(m_scr, l_scr, acc_scr):
    m_scr[...] = jnp.full_like(m_scr, -jnp.inf)
    l_scr[...] = jnp.zeros_like(l_scr)
    acc_scr[...] = jnp.zeros_like(acc_scr)


def _diff_sample_pages(bias_ref, q_ref, k_refs, v_refs, m_scr, l_scr, acc_scr, *, seq):
    rows = DIFF_HEADS * seq
    kv_rows = k_refs[0].shape[0] // 2
    qs = _stacked_queries(q_ref)
    bias = bias_ref[...]
    ps, alphas = [], []
    for m in range(2):
        s = jnp.concatenate(
            [lax.dot_general(qs[m], kr[pl.ds(m, kv_rows, stride=2), :].astype(BF16), _NT,
                             preferred_element_type=F32) + bias for kr in k_refs], axis=1)
        p, alpha, m_new, l_new = _softmax_update(s, m_scr[m], l_scr[m])
        m_scr[m] = m_new
        l_scr[m] = l_new
        ps.append(p)
        alphas.append(alpha)
    p = jnp.concatenate(ps, axis=0).astype(BF16)
    for e in range(2):
        pv = None
        for n in range(len(v_refs)):
            v = v_refs[n][:, :, e * LANES:(e + 1) * LANES].reshape(kv_rows, LANES).astype(BF16)
            t = jnp.dot(p[:, n * kv_rows:(n + 1) * kv_rows], v, preferred_element_type=F32)
            pv = t if pv is None else pv + t
        for m in range(2):
            cols = slice(e * LANES, (e + 1) * LANES)
            acc_scr[m, :, cols] = alphas[m] * acc_scr[m, :, cols] + pv[m * rows:(m + 1) * rows]


def _diff_sample_finish(lam_ref, q_ref, kn_ref, vn_ref, g_ref, o_ref, m_scr, l_scr, acc_scr, *, seq, lam_init):
    qs = _stacked_queries(q_ref)
    kn, vn = kn_ref[...], vn_ref[...]
    causal = (lax.broadcasted_iota(jnp.int32, (seq, seq), 0)
              >= lax.broadcasted_iota(jnp.int32, (seq, seq), 1))
    accs, ls = [], []
    for m in range(2):
        s = jnp.concatenate(
            [jnp.where(causal,
                       lax.dot_general(qs[m][h * seq:(h + 1) * seq],
                                       kn[:, (2 * h + m) * DIFF_DK:(2 * h + m + 1) * DIFF_DK].astype(BF16),
                                       _NT, preferred_element_type=F32),
                       -jnp.inf) for h in range(DIFF_HEADS)], axis=0)
        p, alpha, _, l_new = _softmax_update(s, m_scr[m], l_scr[m])
        pn = p.astype(BF16)
        pv = jnp.concatenate(
            [jnp.dot(pn[h * seq:(h + 1) * seq], vn[:, h * DIFF_DV:(h + 1) * DIFF_DV].astype(BF16),
                     preferred_element_type=F32) for h in range(DIFF_HEADS)], axis=0)
        accs.append(alpha * acc_scr[m] + pv)
        ls.append(l_new)
    lam = _lambda_value(lam_ref, lam_init)
    o = accs[0] / ls[0] - lam * (accs[1] / ls[1])
    y = _unit_rms(o) * g_ref[...] * (1.0 - lam_init)
    o_ref[...] = jnp.concatenate([y[h * seq:(h + 1) * seq] for h in range(DIFF_HEADS)], axis=1)


def _mlp_and_sample_attention_kernel(pt_ref, x_ref, gpre_ref, wup_ref, wdn_ref, gpost_ref,
                                     lam_ref, bias_ref, q_ref, kn_ref, vn_ref, g_ref, ck_hbm, cv_hbm,
                                     y_ref, o_ref, h_scr, m_scr, l_scr, acc_scr, k_buf, v_buf, sem,
                                     *, layer, n_pages_step, attn_steps, seq, lam_init):
    j, n_j = pl.program_id(1), pl.num_programs(1)
    g = pl.program_id(0) * n_j + j
    total = pl.num_programs(0) * n_j
    step = lax.rem(g, attn_steps)
    slot = lax.rem(g, 2)

    def page_copies(at_step, into):
        b, s = lax.div(at_step, attn_steps), lax.rem(at_step, attn_steps)
        copies = []
        for n in range(n_pages_step):
            page = pt_ref[b, s * n_pages_step + n]
            copies.append(pltpu.make_async_copy(ck_hbm.at[layer, page], k_buf.at[into, n], sem.at[into]))
            copies.append(pltpu.make_async_copy(cv_hbm.at[layer, page], v_buf.at[into, n], sem.at[into]))
        return copies

    def start_all(copies):
        for c in copies:
            c.start(priority=PAGE_DMA_PRIORITY)

    pl.when(g == 0)(lambda: start_all(page_copies(g, slot)))
    pl.when(g + 1 < total)(lambda: start_all(page_copies(g + 1, 1 - slot)))
    pl.when(j == 0)(lambda: _mlp_begin(x_ref, gpre_ref, y_ref, h_scr))
    pl.when(step == 0)(lambda: _diff_sample_begin(m_scr, l_scr, acc_scr))
    for c in page_copies(g, slot):
        c.wait()
    k_refs = [k_buf.at[slot, n] for n in range(n_pages_step)]
    v_refs = [v_buf.at[slot, n] for n in range(n_pages_step)]
    _mlp_tile(wup_ref, wdn_ref, y_ref, h_scr)
    _diff_sample_pages(bias_ref, q_ref, k_refs, v_refs, m_scr, l_scr, acc_scr, seq=seq)
    pl.when(j == n_j - 1)(lambda: _mlp_finish(x_ref, gpost_ref, y_ref))
    pl.when(step == attn_steps - 1)(lambda: _diff_sample_finish(
        lam_ref, q_ref, kn_ref, vn_ref, g_ref, o_ref, m_scr, l_scr, acc_scr, seq=seq, lam_init=lam_init))


def _mlp_and_sample_attention(x2d, g_pre, w_up_bf, w_dn_bf, g_post, tm, tf,
                              dq, dk_new, dv_new, cache_k, cache_v, layer, page_table, lam_rows, subln_g,
                              batch, seq, lam_init, pages_per_step):
    T, D = x2d.shape
    FF = w_up_bf.shape[1]
    n_i, n_j = T // tm, FF // tf
    depth, n_pool, page = cache_k.shape[:3]
    n_pages = page_table.shape[1]
    attn_steps = n_pages // pages_per_step
    assert n_pages % pages_per_step == 0 and DIFF_DV == 2 * LANES
    assert n_i * n_j == batch * attn_steps, "the two jobs must have the same number of grid steps"
    k_rows = page * DIFF_HEADS * 2
    ck = cache_k.reshape(depth, n_pool, k_rows, DIFF_DK)
    rows = DIFF_HEADS * seq
    bias = _head_match_bias(rows, seq, page * DIFF_HEADS, DIFF_HEADS)

    vec = pl.BlockSpec((1, D), lambda i, j, pt: (0, 0))
    row = pl.BlockSpec((seq, SEG), lambda i, j, pt: ((i * n_j + j) // attn_steps, 0))
    hbm = pl.BlockSpec(memory_space=pl.ANY)
    grid_spec = pltpu.PrefetchScalarGridSpec(
        num_scalar_prefetch=1,
        grid=(n_i, n_j),
        in_specs=[pl.BlockSpec((tm, D), lambda i, j, pt: (i, 0)), vec,
                  pl.BlockSpec((D, tf), lambda i, j, pt: (0, j)),
                  pl.BlockSpec((tf, D), lambda i, j, pt: (j, 0)), vec,
                  pl.BlockSpec((4, DIFF_DK), lambda i, j, pt: (0, 0)),
                  pl.BlockSpec(bias.shape, lambda i, j, pt: (0, 0)),
                  row, row, row,
                  pl.BlockSpec((1, DIFF_DV), lambda i, j, pt: (0, 0)),
                  hbm, hbm],
        out_specs=[pl.BlockSpec((tm, D), lambda i, j, pt: (i, 0)), row],
        scratch_shapes=[pltpu.VMEM((tm, D), BF16),
                        pltpu.VMEM((2, rows, 1), F32), pltpu.VMEM((2, rows, 1), F32),
                        pltpu.VMEM((2, rows, DIFF_DV), F32),
                        pltpu.VMEM((2, pages_per_step, k_rows, DIFF_DK), F32),
                        pltpu.VMEM((2, pages_per_step, page, DIFF_HEADS, DIFF_DV), F32),
                        pltpu.SemaphoreType.DMA((2,))],
    )
    return pl.pallas_call(
        functools.partial(_mlp_and_sample_attention_kernel, layer=layer, n_pages_step=pages_per_step,
                          attn_steps=attn_steps, seq=seq, lam_init=lam_init),
        grid_spec=grid_spec,
        out_shape=[jax.ShapeDtypeStruct((T, D), F32), jax.ShapeDtypeStruct((batch * seq, SEG), F32)],
        compiler_params=_params(("arbitrary", "arbitrary")),
        name="mlp_and_sample_attention",
    )(page_table, x2d, g_pre.reshape(1, D), w_up_bf, w_dn_bf, g_post.reshape(1, D),
      lam_rows, bias, dq, dk_new, dv_new, subln_g.reshape(1, DIFF_DV), ck, cache_v)


def _norm_matmul_kernel(x_ref, g_ref, w_ref, o_ref, *, scale):
    h = _rms(x_ref[...], g_ref[...]).astype(BF16)
    z = jnp.dot(h, w_ref[...], preferred_element_type=F32)
    if scale != 1.0:
        z = z * scale
    o_ref[...] = z.astype(o_ref.dtype)


def _norm_matmul(x2d, g, w_bf, tm, out_dtype, scale=1.0):
    T, D = x2d.shape
    N = w_bf.shape[1]
    return pl.pallas_call(
        functools.partial(_norm_matmul_kernel, scale=scale),
        grid=(T // tm,),
        in_specs=[pl.BlockSpec((tm, D), lambda i: (i, 0)),
                  pl.BlockSpec((1, D), lambda i: (0, 0)),
                  pl.BlockSpec((D, N), lambda i: (0, 0))],
        out_specs=pl.BlockSpec((tm, N), lambda i: (i, 0)),
        out_shape=jax.ShapeDtypeStruct((T, N), out_dtype),
        compiler_params=_params(("parallel",)),
        name="norm_matmul",
    )(x2d, g.reshape(1, D), w_bf)


def _matmul_norm_res_kernel(*refs, n_in):
    a_refs = refs[:n_in]
    w_refs = refs[n_in:2 * n_in]
    g_ref, res_ref, o_ref = refs[2 * n_in:]
    z = None
    for a_ref, w_ref in zip(a_refs, w_refs):
        t = jnp.dot(a_ref[...].astype(BF16), w_ref[...], preferred_element_type=F32)
        z = t if z is None else z + t
    o_ref[...] = res_ref[...] + _rms(z, g_ref[...])


def _matmul_norm_residual(acts, ws_bf, g, res, tm):
    T, D = res.shape
    n_in = len(acts)
    in_specs = ([pl.BlockSpec((tm, a.shape[1]), lambda i: (i, 0)) for a in acts]
                + [pl.BlockSpec(w.shape, lambda i: (0, 0)) for w in ws_bf]
                + [pl.BlockSpec((1, D), lambda i: (0, 0)), pl.BlockSpec((tm, D), lambda i: (i, 0))])
    return pl.pallas_call(
        functools.partial(_matmul_norm_res_kernel, n_in=n_in),
        grid=(T // tm,),
        in_specs=in_specs,
        out_specs=pl.BlockSpec((tm, D), lambda i: (i, 0)),
        out_shape=jax.ShapeDtypeStruct((T, D), F32),
        compiler_params=_params(("parallel",)),
        name="matmul_norm_residual",
    )(*acts, *ws_bf, g.reshape(1, D), res)


def _softmax_rows(s):
    e = jnp.exp(s - jnp.max(s, axis=-1, keepdims=True))
    return e / jnp.sum(e, axis=-1, keepdims=True)


def _cross_sample_kernel(bias_ref, q_ref, mk_ref, mv_ref, o_ref, *, group, seq):
    bias = bias_ref[...]
    outs = []
    for b in range(group):
        qb = q_ref[b * seq:(b + 1) * seq, :]
        qh = jnp.concatenate([qb[:, h * MEM_DH:(h + 1) * MEM_DH] for h in range(MEM_HEADS)], axis=0)
        s = lax.dot_general(qh.astype(BF16), mk_ref[b].astype(BF16), _NT, preferred_element_type=F32) + bias
        p = _softmax_rows(s).astype(BF16)
        o = jnp.dot(p, mv_ref[b].astype(BF16), preferred_element_type=F32)
        outs.append(jnp.concatenate([o[h * seq:(h + 1) * seq] for h in range(MEM_HEADS)], axis=1))
    o_ref[...] = jnp.concatenate(outs, axis=0)


def _cross_attention_sample(q, mem_k, mem_v, layer, batch, seq, group):
    depth, _, M = mem_k.shape[:3]
    W = MEM_HEADS * MEM_DH
    mk = mem_k.reshape(depth, batch, M * MEM_HEADS, MEM_DH)
    mv = mem_v.reshape(depth, batch, M * MEM_HEADS, MEM_DH)
    bias = _head_match_bias(MEM_HEADS * seq, seq, M * MEM_HEADS, MEM_HEADS)
    kv = pl.BlockSpec((None, group, M * MEM_HEADS, MEM_DH), lambda i: (layer, i, 0, 0))
    row = pl.BlockSpec((group * seq, W), lambda i: (i, 0))
    return pl.pallas_call(
        functools.partial(_cross_sample_kernel, group=group, seq=seq),
        grid=(batch // group,),
        in_specs=[pl.BlockSpec(bias.shape, lambda i: (0, 0)), row, kv, kv],
        out_specs=row,
        out_shape=jax.ShapeDtypeStruct((batch * seq, W), F32),
        compiler_params=_params(("parallel",)),
        name="cross_attention_sample",
    )(bias, q, mk, mv)


def _mlp_begin(x_ref, gpre_ref, o_ref, h_scr):
    h_scr[...] = _rms(x_ref[...], gpre_ref[...]).astype(BF16)
    o_ref[...] = jnp.zeros_like(o_ref)


def _mlp_tile(wup_ref, wdn_ref, o_ref, h_scr):
    u = jnp.maximum(jnp.dot(h_scr[...], wup_ref[...], preferred_element_type=F32), 0.0)
    o_ref[...] += jnp.dot((u * u).astype(BF16), wdn_ref[...], preferred_element_type=F32)


def _mlp_finish(x_ref, gpost_ref, o_ref):
    o_ref[...] = x_ref[...] + _rms(o_ref[...], gpost_ref[...])


def _mlp_kernel(x_ref, gpre_ref, wup_ref, wdn_ref, gpost_ref, o_ref, h_scr):
    j = pl.program_id(1)
    pl.when(j == 0)(lambda: _mlp_begin(x_ref, gpre_ref, o_ref, h_scr))
    _mlp_tile(wup_ref, wdn_ref, o_ref, h_scr)
    pl.when(j == pl.num_programs(1) - 1)(lambda: _mlp_finish(x_ref, gpost_ref, o_ref))


def _mlp(x2d, g_pre, w_up_bf, w_dn_bf, g_post, tm, tf):
    T, D = x2d.shape
    FF = w_up_bf.shape[1]
    vec = pl.BlockSpec((1, D), lambda i, j: (0, 0))
    return pl.pallas_call(
        _mlp_kernel,
        grid=(T // tm, FF // tf),
        in_specs=[pl.BlockSpec((tm, D), lambda i, j: (i, 0)), vec,
                  pl.BlockSpec((D, tf), lambda i, j: (0, j)),
                  pl.BlockSpec((tf, D), lambda i, j: (j, 0)), vec],
        out_specs=pl.BlockSpec((tm, D), lambda i, j: (i, 0)),
        out_shape=jax.ShapeDtypeStruct((T, D), F32),
        scratch_shapes=[pltpu.VMEM((tm, D), BF16)],
        compiler_params=_params(("parallel", "arbitrary")),
        name="mlp",
    )(x2d, g_pre.reshape(1, D), w_up_bf, w_dn_bf, g_post.reshape(1, D))


def _position_tables(pos, tm):
    reps = max(1, tm // pos.shape[0])
    pos = jnp.tile(pos, reps)
    cr, srl, srh = _rope_tables(pos, RET_DK, RET_THETA)
    cd, sdl, sdh = _rope_tables(pos, ROPE_DIM, ROPE_THETA)
    return (cr, srl + srh, cd, sdl, sdh), pos.shape[0] // tm


def _layer_tail(x2d, mix_parts, wts, cross_fn, tm):
    (w_out_parts, g_mix_post, g_mem_pre, w_mem_q, w_mem_o, g_mem_post,
     g_mlp_pre, w_up, w_down, g_mlp_post) = wts
    x1 = _matmul_norm_residual(mix_parts, w_out_parts, g_mix_post, x2d, tm)
    q = cross_fn(x1, g_mem_pre, w_mem_q)
    x2 = _matmul_norm_residual([q], [w_mem_o], g_mem_post, x1, tm)
    return _mlp(x2, g_mlp_pre, w_up, w_down, g_mlp_post, tm, MLP_FF_TILE)


def _out_cross_kernel(mr_ref, md_ref, x_ref, wo1_ref, wo2_ref, g1_ref, gpre_ref, wq_ref, mk_ref, mv_ref,
                      wmo_ref, g2_ref, o_ref):
    z = (jnp.dot(mr_ref[...], wo1_ref[...], preferred_element_type=F32)
         + jnp.dot(md_ref[...], wo2_ref[...], preferred_element_type=F32))
    x1 = x_ref[...] + _rms(z, g1_ref[...])
    h = _rms(x1, gpre_ref[...]).astype(BF16)
    q = (jnp.dot(h, wq_ref[...], preferred_element_type=F32) * (MEM_DH ** -0.5)).astype(BF16)
    outs = []
    for hd in range(MEM_HEADS):
        sl = slice(hd * MEM_DH, (hd + 1) * MEM_DH)
        s = lax.dot_general(q[:, sl], mk_ref[0, :, sl], _NT, preferred_element_type=F32)
        p = _softmax_rows(s).astype(BF16)
        outs.append(jnp.dot(p, mv_ref[0, :, sl], preferred_element_type=F32).astype(BF16))
    y = jnp.dot(jnp.concatenate(outs, axis=1), wmo_ref[...], preferred_element_type=F32)
    o_ref[...] = x1 + _rms(y, g2_ref[...])


def _out_proj_cross_block(mix_parts, x2d, wts, mk_bf, mv_bf, batch, seq, tm):
    (w_out_parts, g_mix_post, g_mem_pre, w_mem_q, w_mem_o, g_mem_post) = wts[:6]
    T, D = x2d.shape
    n = seq // tm
    M, W = mk_bf.shape[1], mk_bf.shape[2]
    row = lambda width: pl.BlockSpec((tm, width), lambda b, i: (b * n + i, 0))
    const = lambda shape: pl.BlockSpec(shape, lambda b, i: (0,) * len(shape), pipeline_mode=pl.Buffered(1))
    kv = pl.BlockSpec((1, M, W), lambda b, i: (b, 0, 0))
    vec = lambda g: g.reshape(1, D)
    return pl.pallas_call(
        _out_cross_kernel,
        grid=(batch, n),
        in_specs=[row(mix_parts[0].shape[1]), row(mix_parts[1].shape[1]), row(D),
                  const(w_out_parts[0].shape), const(w_out_parts[1].shape), const((1, D)), const((1, D)),
                  const(w_mem_q.shape), kv, kv, const(w_mem_o.shape), const((1, D))],
        out_specs=row(D),
        out_shape=jax.ShapeDtypeStruct((T, D), F32),
        compiler_params=_params(("parallel", "parallel")),
        name="out_proj_cross_block",
    )(mix_parts[0], mix_parts[1], x2d, w_out_parts[0], w_out_parts[1], vec(g_mix_post), vec(g_mem_pre),
      w_mem_q, mk_bf, mv_bf, w_mem_o, vec(g_mem_post))


def kernel(x_prompt, x_sample, mem_prompt, state_ret, cache_diff_k, cache_diff_v, cache_mem_k, cache_mem_v, page_table, w_in, w_out, diff_lambda_q1, diff_lambda_k1, diff_lambda_q2, diff_lambda_k2, diff_subln_g, norm_mix_pre, norm_mix_post, norm_mem_pre, norm_mem_post, norm_mlp_pre, norm_mlp_post, mem_norm_g, w_mem_q, w_mem_k, w_mem_v, w_mem_o, w_mlp_up, w_mlp_down):
    depth = w_in.shape[0]
    B, L_p, D = x_prompt.shape
    B_s, L_s, _ = x_sample.shape
    n_pages, page = page_table.shape[1], cache_diff_k.shape[2]
    past_len = n_pages * page
    M = mem_prompt.shape[1]
    W_MEM = MEM_HEADS * MEM_DH
    TM = 512
    ret_chunk_p = math.gcd(L_p, RET_CHUNK)

    tabs_p, per_p = _position_tables(jnp.arange(L_p, dtype=F32), TM)
    tabs_s, per_s = _position_tables(past_len + jnp.arange(L_s, dtype=F32), TM)

    yp = x_prompt.reshape(B * L_p, D)
    ys = x_sample.reshape(B_s * L_s, D)
    mem2d = mem_prompt.reshape(B * M, D)
    outs = {k: [] for k in ("rp", "kp", "vp", "mkp", "mvp", "rs", "ks", "vs")}

    for i in range(depth):
        lam_init = 0.8 - 0.6 * math.exp(-0.3 * i)
        lam_rows = jnp.stack([diff_lambda_q1[i], diff_lambda_k1[i], diff_lambda_q2[i], diff_lambda_k2[i]])
        w_in_bf = w_in[i].astype(BF16)
        half = w_out.shape[1] // 2
        tail_w = ([w_out[i, :half].astype(BF16), w_out[i, half:].astype(BF16)], norm_mix_post[i],
                  norm_mem_pre[i], w_mem_q[i].astype(BF16), w_mem_o[i].astype(BF16), norm_mem_post[i],
                  norm_mlp_pre[i], w_mlp_up[i].astype(BF16), w_mlp_down[i].astype(BF16), norm_mlp_post[i])

        mk_p = _norm_matmul(mem2d, mem_norm_g[i], w_mem_k[i].astype(BF16), B * M, F32)
        mv_p = _norm_matmul(mem2d, mem_norm_g[i], w_mem_v[i].astype(BF16), B * M, F32)
        rq, rk, rv, rg, dq_t, dk, dv, dv_t = _in_projection(
            yp, norm_mix_pre[i], w_in_bf, tabs_p, per_p, TM, BF16, DIFF_DK ** -0.5 * LOG2E, True)
        mix_ret, s_p = _retention_prompt(rq, rk, rv, rg, B, L_p, ret_chunk_p)
        mix_diff = _diff_attention_prompt(dq_t, dk, dv_t, lam_rows, diff_subln_g[i], B, L_p, lam_init)
        mk_bf = mk_p.astype(BF16).reshape(B, M, W_MEM)
        mv_bf = mv_p.astype(BF16).reshape(B, M, W_MEM)

        x2 = _out_proj_cross_block([mix_ret, mix_diff], yp, tail_w, mk_bf, mv_bf, B, L_p, TM)
        outs["rp"].append(s_p.astype(state_ret.dtype))
        halves = DIFF_DV // LANES
        outs["kp"].append(dk.reshape(B, L_p, DIFF_HEADS, 2, DIFF_DK))
        outs["vp"].append(dv.reshape(B, L_p, halves, DIFF_HEADS, LANES).transpose(0, 1, 3, 2, 4)
                          .reshape(B, L_p, DIFF_HEADS, DIFF_DV))
        outs["mkp"].append(mk_p.reshape(B, M, MEM_HEADS, MEM_DH))
        outs["mvp"].append(mv_p.reshape(B, M, MEM_HEADS, MEM_DH))

        rq, rk, rv, rg, dq, dk, dv = _in_projection(ys, norm_mix_pre[i], w_in_bf, tabs_s, per_s, TM, F32,
                                                    DIFF_DK ** -0.5, False)
        mix_ret, s_s = _retention_sample(rq, rk, rv, rg, state_ret, i, B_s, L_s, 4)
        yp, mix_diff = _mlp_and_sample_attention(
            x2, norm_mlp_pre[i], tail_w[7], tail_w[8], norm_mlp_post[i], TM, MLP_FF_TILE_FUSED,
            dq, dk, dv, cache_diff_k, cache_diff_v, i, page_table, lam_rows, diff_subln_g[i],
            B_s, L_s, lam_init, PAGES_PER_STEP)
        def cross_s(x1, g, wq):
            q = _norm_matmul(x1, g, wq, TM, F32, scale=MEM_DH ** -0.5)
            return _cross_attention_sample(q, cache_mem_k, cache_mem_v, i, B_s, L_s, 8)

        ys = _layer_tail(ys, [mix_ret, mix_diff], tail_w, cross_s, TM)
        outs["rs"].append(s_s.astype(state_ret.dtype))
        outs["ks"].append(dk.reshape(B_s, L_s, DIFF_HEADS, 2, DIFF_DK))
        outs["vs"].append(dv.reshape(B_s, L_s, DIFF_HEADS, DIFF_DV))

    st = lambda k: outs[k][0][None] if depth == 1 else jnp.stack(outs[k])
    return (yp.reshape(B, L_p, D), ys.reshape(B_s, L_s, D), st("rp"), st("kp"), st("vp"),
            st("mkp"), st("mvp"), st("rs"), st("ks"), st("vs"))
```

```python
import functools
import math

import jax
import jax.numpy as jnp
from jax import lax
from jax.experimental import pallas as pl
from jax.experimental.pallas import tpu as pltpu

F32 = jnp.float32
BF16 = jnp.bfloat16

LANES = 128
MXU_COLS = 256
RET_HEADS = 8
RET_DK = 128
RET_DV = 128
RET_THETA = 10000.0
RET_CHUNK = 2 * 128
DIFF_HEADS = 4
DIFF_DK = 128
DIFF_DV = 256
ROPE_THETA = 500000.0
ROPE_DIM = DIFF_DK // 4
MEM_HEADS = 4
MEM_DH = 128
Q_BLOCK = 128
EPS = 1e-6
LOG2E = math.log2(math.e)
SEG = 1024
N_SEG = 7
MLP_FF_TILE = 1024
MLP_FF_TILE_FUSED = 512
PAGES_PER_STEP = 8
PAGE_DMA_PRIORITY = 1

VMEM_LIMIT = 56 * 1024 * 1024

_NT = (((1,), (1,)), ((), ()))
_TN = (((0,), (0,)), ((), ()))


def _params(sem, vmem=VMEM_LIMIT):
    return pltpu.CompilerParams(dimension_semantics=sem, vmem_limit_bytes=vmem)


def _rms(x, g):
    return x * lax.rsqrt(jnp.mean(x * x, axis=-1, keepdims=True) + EPS) * g


def _unit_rms(o):
    return o * lax.rsqrt(jnp.mean(o * o, axis=-1, keepdims=True) + EPS)


def _rope_tables(pos, rot_dim, theta):
    half = rot_dim // 2
    inv = jnp.exp(-math.log(theta) * (2.0 * jnp.arange(half, dtype=F32) / rot_dim))
    ang = pos[:, None] * inv[None, :]
    cos, sin = jnp.cos(ang), jnp.sin(ang)
    n = pos.shape[0]
    rest = LANES - rot_dim
    c = jnp.concatenate([cos, cos, jnp.ones((n, rest), F32)], axis=-1)
    s_lo = jnp.concatenate([-sin, jnp.zeros((n, LANES - half), F32)], axis=-1)
    s_hi = jnp.concatenate([jnp.zeros((n, half), F32), sin, jnp.zeros((n, rest), F32)], axis=-1)
    return c, s_lo, s_hi


def _inproj_kernel(x_ref, g_ref, w_ref, cr_ref, sr_ref, cd_ref, sdl_ref, sdh_ref, *rest,
                   dq_scale, transposed):
    if transposed:
        rq_ref, rk_ref, rv_ref, rg_ref, dq_ref, dk_ref, dv_ref, dvt_ref, h_scr = rest
    else:
        rq_ref, rk_ref, rv_ref, rg_ref, dq_ref, dk_ref, dv_ref, h_scr = rest
    j = pl.program_id(1)
    heads = [slice(h * LANES, (h + 1) * LANES) for h in range(SEG // LANES)]

    @pl.when(j == 0)
    def _():
        h_scr[...] = _rms(x_ref[...], g_ref[...]).astype(BF16)

    def project():
        return jnp.dot(h_scr[...], w_ref[...], preferred_element_type=F32)

    def ret_rot(out_ref, scale):
        z_all = project()
        c, s = cr_ref[...], sr_ref[...]
        for sl in heads:
            z = z_all[:, sl]
            r = z * c + pltpu.roll(z, RET_DK // 2, 1) * s
            if scale != 1.0:
                r = r * scale
            out_ref[:, sl] = r.astype(out_ref.dtype)

    def diff_rot(z):
        half = ROPE_DIM // 2
        return (z * cd_ref[...] + pltpu.roll(z, LANES - half, 1) * sdl_ref[...]
                + pltpu.roll(z, half, 1) * sdh_ref[...])

    @pl.when(j == 0)
    def _():
        ret_rot(rq_ref, 1.0)

    @pl.when(j == 1)
    def _():
        ret_rot(rk_ref, RET_DK ** -0.5)

    @pl.when(j == 2)
    def _():
        rv_ref[...] = project().astype(rv_ref.dtype)

    @pl.when(j == 3)
    def _():
        rg_ref[...] = project()

    @pl.when(j == 4)
    def _():
        z_all = project()
        for sl in heads:
            r = diff_rot(z_all[:, sl]) * dq_scale
            if transposed:
                dq_ref[sl, :] = r.T.astype(dq_ref.dtype)
            else:
                dq_ref[:, sl] = r.astype(dq_ref.dtype)

    tm = x_ref.shape[0]
    n_heads = len(heads)

    @pl.when(j == 5)
    def _():
        z_all = project()
        for i, sl in enumerate(heads):
            r = diff_rot(z_all[:, sl])
            if transposed:
                dk_ref[pl.ds(i, tm, stride=n_heads), :] = r
            else:
                dk_ref[:, sl] = r

    @pl.when(j == 6)
    def _():
        z_all = project()
        if transposed:
            for i, sl in enumerate(heads):
                h, e = divmod(i, DIFF_DV // LANES)
                dv_ref[pl.ds(e * DIFF_HEADS + h, tm, stride=n_heads), :] = z_all[:, sl]
                dvt_ref[sl, :] = z_all[:, sl].T.astype(dvt_ref.dtype)
        else:
            dv_ref[...] = z_all


def _in_projection(x2d, g, w_bf, tabs, period_blocks, tm, act_dtype, dq_scale, transposed):
    T, D = x2d.shape
    cr, sr, cd, sdl, sdh = tabs
    tab_spec = pl.BlockSpec((tm, LANES), lambda i, j: (i % period_blocks, 0))
    row_spec = pl.BlockSpec((tm, SEG), lambda i, j: (i, 0))
    col_spec = pl.BlockSpec((SEG, tm), lambda i, j: (0, i))
    sds = lambda dt: jax.ShapeDtypeStruct((T, SEG), dt)
    sds_t = lambda dt: jax.ShapeDtypeStruct((SEG, T), dt)
    n_heads = SEG // LANES
    cache_spec = pl.BlockSpec((tm * n_heads, LANES), lambda i, j: (i, 0))
    sds_c = jax.ShapeDtypeStruct((T * n_heads, LANES), F32)
    if transposed:
        n_i = T // tm
        nxt = lambda seg: (lambda i, j: jnp.minimum(i + (j > seg).astype(jnp.int32), n_i - 1))
        rspec = lambda seg: pl.BlockSpec((tm, SEG), lambda i, j: (nxt(seg)(i, j), 0))
        cspec = lambda seg: pl.BlockSpec((SEG, tm), lambda i, j: (0, nxt(seg)(i, j)))
        kspec = lambda seg: pl.BlockSpec((tm * n_heads, LANES), lambda i, j: (nxt(seg)(i, j), 0))
        out_specs = [rspec(0), rspec(1), rspec(2), rspec(3), cspec(4), kspec(5), kspec(6), cspec(6)]
        out_shape = [sds(act_dtype), sds(act_dtype), sds(act_dtype), sds(F32),
                     sds_t(act_dtype), sds_c, sds_c, sds_t(act_dtype)]
    else:
        out_specs = [pl.BlockSpec((tm, SEG), lambda i, j: (i, 0), pipeline_mode=pl.Buffered(1))] * N_SEG
        out_shape = [sds(act_dtype), sds(act_dtype), sds(act_dtype), sds(F32), sds(act_dtype), sds(F32), sds(F32)]
    return pl.pallas_call(
        functools.partial(_inproj_kernel, dq_scale=dq_scale, transposed=transposed),
        grid=(T // tm, N_SEG),
        in_specs=[
            pl.BlockSpec((tm, D), lambda i, j: (i, 0)),
            pl.BlockSpec((1, D), lambda i, j: (0, 0)),
            pl.BlockSpec((D, SEG), lambda i, j: (0, j)),
            tab_spec, tab_spec, tab_spec, tab_spec, tab_spec,
        ],
        out_specs=out_specs,
        out_shape=out_shape,
        scratch_shapes=[pltpu.VMEM((tm, D), BF16)],
        compiler_params=_params(("arbitrary", "arbitrary")),
        name="in_projection",
    )(x2d, g.reshape(1, D), w_bf, cr, sr, cd, sdl, sdh)


def _ret_decay_tables(chunk):
    lg = jnp.log1p(-jnp.exp2(-5.0 - jnp.arange(RET_HEADS, dtype=F32)))
    idx = jnp.arange(chunk, dtype=F32)
    rel = idx[:, None] - idx[None, :]
    dmat = jnp.where(rel[None] >= 0, jnp.exp(jnp.maximum(rel, 0.0)[None] * lg[:, None, None]), 0.0)
    q_decay = jnp.exp((idx + 1.0)[:, None] * lg[None, :])
    k_decay = jnp.exp((chunk - 1.0 - idx)[:, None] * lg[None, :])
    chunk_decay = jnp.exp(chunk * lg)
    widen = lambda t: jnp.repeat(t, RET_DK, axis=1)
    return dmat, widen(q_decay), widen(k_decay), chunk_decay


def _ret_head(q, k, v, s, dmat, qd, kd, cd, gate):
    qb, kb, vb = q.astype(BF16), k.astype(BF16), v.astype(BF16)
    att = lax.dot_general(qb, kb, _NT, preferred_element_type=F32) * dmat
    o = (jnp.dot(att.astype(BF16), vb, preferred_element_type=F32)
         + jnp.dot(qb, s.astype(BF16), preferred_element_type=F32) * qd)
    kdk = (k.astype(F32) * kd).astype(BF16)
    s_new = s * cd + lax.dot_general(kdk, vb, _TN, preferred_element_type=F32)
    y = _unit_rms(o) * (gate * jax.nn.sigmoid(gate))
    return y, s_new


def _ret_prompt_kernel(cd_ref, q_ref, k_ref, v_ref, g_ref, dmat_ref, qd_ref, kd_ref, mix_ref, s_ref):
    @pl.when(pl.program_id(1) == 0)
    def _():
        s_ref[...] = jnp.zeros_like(s_ref)

    for h in range(RET_HEADS):
        sl = slice(h * RET_DK, (h + 1) * RET_DK)
        y, s_new = _ret_head(q_ref[:, sl], k_ref[:, sl], v_ref[:, sl], s_ref[0, h], dmat_ref[h],
                             qd_ref[:, sl], kd_ref[:, sl], cd_ref[h], g_ref[:, sl])
        s_ref[0, h] = s_new
        mix_ref[:, sl] = y.astype(mix_ref.dtype)


def _retention_prompt(rq, rk, rv, rg, batch, seq, chunk):
    n = seq // chunk
    dmat, qd, kd, cd = _ret_decay_tables(chunk)
    row = pl.BlockSpec((chunk, SEG), lambda b, c: (b * n + c, 0))
    full2 = pl.BlockSpec((chunk, SEG), lambda b, c: (0, 0))
    return pl.pallas_call(
        _ret_prompt_kernel,
        grid=(batch, n),
        in_specs=[
            pl.BlockSpec(memory_space=pltpu.SMEM),
            row, row, row, row,
            pl.BlockSpec((RET_HEADS, chunk, chunk), lambda b, c: (0, 0, 0)),
            full2, full2,
        ],
        out_specs=[row, pl.BlockSpec((1, RET_HEADS, RET_DK, RET_DV), lambda b, c: (b, 0, 0, 0))],
        out_shape=[jax.ShapeDtypeStruct((batch * seq, SEG), BF16),
                   jax.ShapeDtypeStruct((batch, RET_HEADS, RET_DK, RET_DV), F32)],
        compiler_params=_params(("parallel", "arbitrary")),
        name="retention_prompt",
    )(cd, rq, rk, rv, rg, dmat, qd, kd)


def _ret_sample_kernel(cd_ref, q_ref, k_ref, v_ref, g_ref, s0_ref, dmat_ref, qd_ref, kd_ref,
                       mix_ref, s_ref, *, group, seq):
    rows = []
    for b in range(group):
        r = slice(b * seq, (b + 1) * seq)
        heads = []
        for h in range(RET_HEADS):
            sl = slice(h * RET_DK, (h + 1) * RET_DK)
            y, s_new = _ret_head(q_ref[r, sl], k_ref[r, sl], v_ref[r, sl], s0_ref[b, h], dmat_ref[h],
                                 qd_ref[:, sl], kd_ref[:, sl], cd_ref[h], g_ref[r, sl])
            s_ref[b, h] = s_new
            heads.append(y)
        rows.append(jnp.concatenate(heads, axis=1))
    mix_ref[...] = jnp.concatenate(rows, axis=0)


def _retention_sample(rq, rk, rv, rg, state, layer, batch, seq, group):
    dmat, qd, kd, cd = _ret_decay_tables(seq)
    row = pl.BlockSpec((group * seq, SEG), lambda i: (i, 0))
    tab = pl.BlockSpec((seq, SEG), lambda i: (0, 0))
    st = pl.BlockSpec((group, RET_HEADS, RET_DK, RET_DV), lambda i: (i, 0, 0, 0))
    st_in = pl.BlockSpec((None, group, RET_HEADS, RET_DK, RET_DV), lambda i: (layer, i, 0, 0, 0))
    return pl.pallas_call(
        functools.partial(_ret_sample_kernel, group=group, seq=seq),
        grid=(batch // group,),
        in_specs=[
            pl.BlockSpec(memory_space=pltpu.SMEM),
            row, row, row, row, st_in,
            pl.BlockSpec((RET_HEADS, seq, seq), lambda i: (0, 0, 0)),
            tab, tab,
        ],
        out_specs=[row, st],
        out_shape=[jax.ShapeDtypeStruct((batch * seq, SEG), F32),
                   jax.ShapeDtypeStruct((batch, RET_HEADS, RET_DK, RET_DV), F32)],
        compiler_params=_params(("parallel",)),
        name="retention_sample",
    )(cd, rq, rk, rv, rg, state, dmat, qd, kd)


def _lambda_value(lam_ref, lam_init):
    a = jnp.sum(lam_ref[0:1, :] * lam_ref[1:2, :], axis=-1, keepdims=True)
    b = jnp.sum(lam_ref[2:3, :] * lam_ref[3:4, :], axis=-1, keepdims=True)
    return jnp.exp(a) - jnp.exp(b) + lam_init


def _softmax_update(s, m_old, l_old):
    m_new = jnp.maximum(m_old, jnp.max(s, axis=-1, keepdims=True))
    alpha = jnp.exp(m_old - m_new)
    p = jnp.exp(s - m_new)
    l_new = alpha * l_old + jnp.sum(p, axis=-1, keepdims=True)
    return p, alpha, m_new, l_new


def _diff_prompt_kernel(qi_tab, ki_tab, lam_ref, qt_ref, k_ref, vt_ref, g_ref, o_ref, m_scr, l_scr, acc_scr,
                        *, tq, tk, lam_init):
    t = pl.program_id(1)
    qi, ki = qi_tab[t], ki_tab[t]
    n_sub = 2 * DIFF_HEADS

    @pl.when(ki == 0)
    def _():
        m_scr[...] = jnp.full_like(m_scr, -jnp.inf)
        l_scr[...] = jnp.zeros_like(l_scr)
        acc_scr[...] = jnp.zeros_like(acc_scr)

    def step(masked):
        if masked:
            kpos = ki * tk + lax.broadcasted_iota(jnp.int32, (tk, tq), 0)
            qpos = qi * tq + lax.broadcasted_iota(jnp.int32, (tk, tq), 1)
            keep = qpos >= kpos
        for h in range(DIFF_HEADS):
            vt = vt_ref[h * DIFF_DV:(h + 1) * DIFF_DV, :]
            for m in range(2):
                i = 2 * h + m
                sl = slice(i * DIFF_DK, (i + 1) * DIFF_DK)
                kb = k_ref[pl.ds(i, tk, stride=n_sub), :].astype(BF16)
                for c in range(tq // MXU_COLS):
                    cs = slice(c * MXU_COLS, (c + 1) * MXU_COLS)
                    st = jnp.dot(kb, qt_ref[sl, cs], preferred_element_type=F32)
                    if masked:
                        st = jnp.where(keep[:, cs], st, -jnp.inf)
                    m_old = m_scr[i, :, cs]
                    m_new = jnp.maximum(m_old, jnp.max(st, axis=0, keepdims=True))
                    alpha = jnp.exp2(m_old - m_new)
                    p = jnp.exp2(st - m_new)
                    l_scr[i, :, cs] = alpha * l_scr[i, :, cs] + jnp.sum(p, axis=0, keepdims=True)
                    m_scr[i, :, cs] = m_new
                    acc_scr[i, :, cs] = (alpha * acc_scr[i, :, cs]
                                         + jnp.dot(vt, p.astype(BF16), preferred_element_type=F32))

    @pl.when(ki < qi)
    def _():
        step(False)

    @pl.when(ki == qi)
    def _():
        step(True)
        lam = _lambda_value(lam_ref, lam_init)
        for h in range(DIFF_HEADS):
            a, b = 2 * h, 2 * h + 1
            ot = acc_scr[a] * (1.0 / l_scr[a]) - lam * (acc_scr[b] * (1.0 / l_scr[b]))
            yt = ot * lax.rsqrt(jnp.mean(ot * ot, axis=0, keepdims=True) + EPS)
            o_ref[:, h * DIFF_DV:(h + 1) * DIFF_DV] = (yt.T * g_ref[...] * (1.0 - lam_init)).astype(o_ref.dtype)


def _diff_attention_prompt(dq_t, dk, dv_t, lam_rows, subln_g, batch, seq, lam_init, tq=512, tk=512):
    assert tq == tk
    nq = seq // tq
    pairs = [(qi, ki) for qi in range(nq) for ki in range(qi + 1)]
    qi_tab = jnp.asarray([p[0] for p in pairs], jnp.int32)
    ki_tab = jnp.asarray([p[1] for p in pairs], jnp.int32)
    n_sub = 2 * DIFF_HEADS
    grid_spec = pltpu.PrefetchScalarGridSpec(
        num_scalar_prefetch=2,
        grid=(batch, len(pairs)),
        in_specs=[
            pl.BlockSpec((4, DIFF_DK), lambda b, t, qt, kt: (0, 0)),
            pl.BlockSpec((SEG, tq), lambda b, t, qt, kt: (0, b * nq + qt[t])),
            pl.BlockSpec((tk * n_sub, DIFF_DK), lambda b, t, qt, kt: (b * nq + kt[t], 0)),
            pl.BlockSpec((SEG, tk), lambda b, t, qt, kt: (0, b * nq + kt[t])),
            pl.BlockSpec((1, DIFF_DV), lambda b, t, qt, kt: (0, 0)),
        ],
        out_specs=pl.BlockSpec((tq, SEG), lambda b, t, qt, kt: (b * nq + qt[t], 0)),
        scratch_shapes=[pltpu.VMEM((n_sub, 1, tq), F32), pltpu.VMEM((n_sub, 1, tq), F32),
                        pltpu.VMEM((n_sub, DIFF_DV, tq), F32)],
    )
    return pl.pallas_call(
        functools.partial(_diff_prompt_kernel, tq=tq, tk=tk, lam_init=lam_init),
        grid_spec=grid_spec,
        out_shape=jax.ShapeDtypeStruct((batch * seq, SEG), BF16),
        compiler_params=_params(("parallel", "arbitrary")),
        name="diff_attention_prompt",
    )(qi_tab, ki_tab, lam_rows, dq_t, dk, dv_t, subln_g.reshape(1, DIFF_DV))


def _head_match_bias(n_rows, seq, n_cols, heads):
    row_h = jnp.arange(n_rows, dtype=jnp.int32)[:, None] // seq
    col_h = jnp.arange(n_cols, dtype=jnp.int32)[None, :] % heads
    return jnp.where(row_h == col_h, 0.0, -jnp.inf).astype(F32)


def _stack_heads(x, width, offset, stride):
    return jnp.concatenate(
        [x[:, offset + h * stride: offset + h * stride + width] for h in range(DIFF_HEADS)], axis=0)


def _stacked_queries(q_ref):
    q = q_ref[...]
    return [_stack_heads(q, DIFF_DK, m * DIFF_DK, 2 * DIFF_DK).astype(BF16) for m in range(2)]


def _diff_sample_begin(m_scr, l_scr, acc_scr):
    m_scr[...] = jnp.full_like(m_scr, -jnp.inf)
    l_scr[...] = jnp.zeros_like(l_scr)
    acc_scr[...] = jnp.zeros_like(acc_scr)


def _diff_sample_pages(bias_ref, q_ref, k_refs, v_refs, m_scr, l_scr, acc_scr, *, seq):
    rows = DIFF_HEADS * seq
    kv_rows = k_refs[0].shape[0] // 2
    qs = _stacked_queries(q_ref)
    bias = bias_ref[...]
    ps, alphas = [], []
    for m in range(2):
        s = jnp.concatenate(
            [lax.dot_general(qs[m], kr[pl.ds(m, kv_rows, stride=2), :].astype(BF16), _NT,
                             preferred_element_type=F32) + bias for kr in k_refs], axis=1)
        p, alpha, m_new, l_new = _softmax_update(s, m_scr[m], l_scr[m])
        m_scr[m] = m_new
        l_scr[m] = l_new
        ps.append(p)
        alphas.append(alpha)
    p = jnp.concatenate(ps, axis=0).astype(BF16)
    for e in range(2):
        pv = None
        for n in range(len(v_refs)):
            v = v_refs[n][:, :, e * LANES:(e + 1) * LANES].reshape(kv_rows, LANES).astype(BF16)
            t = jnp.dot(p[:, n * kv_rows:(n + 1) * kv_rows], v, preferred_element_type=F32)
            pv = t if pv is None else pv + t
        for m in range(2):
            cols = slice(e * LANES, (e + 1) * LANES)
            acc_scr[m, :, cols] = alphas[m] * acc_scr[m, :, cols] + pv[m * rows:(m + 1) * rows]


def _diff_sample_finish(lam_ref, q_ref, kn_ref, vn_ref, g_ref, o_ref, m_scr, l_scr, acc_scr, *, seq, lam_init):
    qs = _stacked_queries(q_ref)
    kn, vn = kn_ref[...], vn_ref[...]
    causal = (lax.broadcasted_iota(jnp.int32, (seq, seq), 0)
              >= lax.broadcasted_iota(jnp.int32, (seq, seq), 1))
    accs, ls = [], []
    for m in range(2):
        s = jnp.concatenate(
            [jnp.where(causal,
                       lax.dot_general(qs[m][h * seq:(h + 1) * seq],
                                       kn[:, (2 * h + m) * DIFF_DK:(2 * h + m + 1) * DIFF_DK].astype(BF16),
                                       _NT, preferred_element_type=F32),
                       -jnp.inf) for h in range(DIFF_HEADS)], axis=0)
        p, alpha, _, l_new = _softmax_update(s, m_scr[m], l_scr[m])
        pn = p.astype(BF16)
        pv = jnp.concatenate(
            [jnp.dot(pn[h * seq:(h + 1) * seq], vn[:, h * DIFF_DV:(h + 1) * DIFF_DV].astype(BF16),
                     preferred_element_type=F32) for h in range(DIFF_HEADS)], axis=0)
        accs.append(alpha * acc_scr[m] + pv)
        ls.append(l_new)
    lam = _lambda_value(lam_ref, lam_init)
    o = accs[0] / ls[0] - lam * (accs[1] / ls[1])
    y = _unit_rms(o) * g_ref[...] * (1.0 - lam_init)
    o_ref[...] = jnp.concatenate([y[h * seq:(h + 1) * seq] for h in range(DIFF_HEADS)], axis=1)


def _mlp_and_sample_attention_kernel(pt_ref, x_ref, gpre_ref, wup_ref, wdn_ref, gpost_ref,
                                     lam_ref, bias_ref, q_ref, kn_ref, vn_ref, g_ref, ck_hbm, cv_hbm,
                                     y_ref, o_ref, h_scr, m_scr, l_scr, acc_scr, k_buf, v_buf, sem,
                                     *, layer, n_pages_step, attn_steps, seq, lam_init):
    j, n_j = pl.program_id(1), pl.num_programs(1)
    g = pl.program_id(0) * n_j + j
    total = pl.num_programs(0) * n_j
    step = lax.rem(g, attn_steps)
    slot = lax.rem(g, 2)

    def page_copies(at_step, into):
        b, s = lax.div(at_step, attn_steps), lax.rem(at_step, attn_steps)
        copies = []
        for n in range(n_pages_step):
            page = pt_ref[b, s * n_pages_step + n]
            copies.append(pltpu.make_async_copy(ck_hbm.at[layer, page], k_buf.at[into, n], sem.at[into]))
            copies.append(pltpu.make_async_copy(cv_hbm.at[layer, page], v_buf.at[into, n], sem.at[into]))
        return copies

    def start_all(copies):
        for c in copies:
            c.start(priority=PAGE_DMA_PRIORITY)

    pl.when(g == 0)(lambda: start_all(page_copies(g, slot)))
    pl.when(g + 1 < total)(lambda: start_all(page_copies(g + 1, 1 - slot)))
    pl.when(j == 0)(lambda: _mlp_begin(x_ref, gpre_ref, y_ref, h_scr))
    pl.when(step == 0)(lambda: _diff_sample_begin(m_scr, l_scr, acc_scr))
    for c in page_copies(g, slot):
        c.wait()
    k_refs = [k_buf.at[slot, n] for n in range(n_pages_step)]
    v_refs = [v_buf.at[slot, n] for n in range(n_pages_step)]
    _mlp_tile(wup_ref, wdn_ref, y_ref, h_scr)
    _diff_sample_pages(bias_ref, q_ref, k_refs, v_refs, m_scr, l_scr, acc_scr, seq=seq)
    pl.when(j == n_j - 1)(lambda: _mlp_finish(x_ref, gpost_ref, y_ref))
    pl.when(step == attn_steps - 1)(lambda: _diff_sample_finish(
        lam_ref, q_ref, kn_ref, vn_ref, g_ref, o_ref, m_scr, l_scr, acc_scr, seq=seq, lam_init=lam_init))


def _mlp_and_sample_attention(x2d, g_pre, w_up_bf, w_dn_bf, g_post, tm, tf,
                              dq, dk_new, dv_new, cache_k, cache_v, layer, page_table, lam_rows, subln_g,
                              batch, seq, lam_init, pages_per_step):
    T, D = x2d.shape
    FF = w_up_bf.shape[1]
    n_i, n_j = T // tm, FF // tf
    depth, n_pool, page = cache_k.shape[:3]
    n_pages = page_table.shape[1]
    attn_steps = n_pages // pages_per_step
    assert n_pages % pages_per_step == 0 and DIFF_DV == 2 * LANES
    assert n_i * n_j == batch * attn_steps, "the two jobs must have the same number of grid steps"
    k_rows = page * DIFF_HEADS * 2
    ck = cache_k.reshape(depth, n_pool, k_rows, DIFF_DK)
    rows = DIFF_HEADS * seq
    bias = _head_match_bias(rows, seq, page * DIFF_HEADS, DIFF_HEADS)

    vec = pl.BlockSpec((1, D), lambda i, j, pt: (0, 0))
    row = pl.BlockSpec((seq, SEG), lambda i, j, pt: ((i * n_j + j) // attn_steps, 0))
    hbm = pl.BlockSpec(memory_space=pl.ANY)
    grid_spec = pltpu.PrefetchScalarGridSpec(
        num_scalar_prefetch=1,
        grid=(n_i, n_j),
        in_specs=[pl.BlockSpec((tm, D), lambda i, j, pt: (i, 0)), vec,
                  pl.BlockSpec((D, tf), lambda i, j, pt: (0, j)),
                  pl.BlockSpec((tf, D), lambda i, j, pt: (j, 0)), vec,
                  pl.BlockSpec((4, DIFF_DK), lambda i, j, pt: (0, 0)),
                  pl.BlockSpec(bias.shape, lambda i, j, pt: (0, 0)),
                  row, row, row,
                  pl.BlockSpec((1, DIFF_DV), lambda i, j, pt: (0, 0)),
                  hbm, hbm],
        out_specs=[pl.BlockSpec((tm, D), lambda i, j, pt: (i, 0)), row],
        scratch_shapes=[pltpu.VMEM((tm, D), BF16),
                        pltpu.VMEM((2, rows, 1), F32), pltpu.VMEM((2, rows, 1), F32),
                        pltpu.VMEM((2, rows, DIFF_DV), F32),
                        pltpu.VMEM((2, pages_per_step, k_rows, DIFF_DK), F32),
                        pltpu.VMEM((2, pages_per_step, page, DIFF_HEADS, DIFF_DV), F32),
                        pltpu.SemaphoreType.DMA((2,))],
    )
    return pl.pallas_call(
        functools.partial(_mlp_and_sample_attention_kernel, layer=layer, n_pages_step=pages_per_step,
                          attn_steps=attn_steps, seq=seq, lam_init=lam_init),
        grid_spec=grid_spec,
        out_shape=[jax.ShapeDtypeStruct((T, D), F32), jax.ShapeDtypeStruct((batch * seq, SEG), F32)],
        compiler_params=_params(("arbitrary", "arbitrary")),
        name="mlp_and_sample_attention",
    )(page_table, x2d, g_pre.reshape(1, D), w_up_bf, w_dn_bf, g_post.reshape(1, D),
      lam_rows, bias, dq, dk_new, dv_new, subln_g.reshape(1, DIFF_DV), ck, cache_v)


def _norm_matmul_kernel(x_ref, g_ref, w_ref, o_ref, *, scale):
    h = _rms(x_ref[...], g_ref[...]).astype(BF16)
    z = jnp.dot(h, w_ref[...], preferred_element_type=F32)
    if scale != 1.0:
        z = z * scale
    o_ref[...] = z.astype(o_ref.dtype)


def _norm_matmul(x2d, g, w_bf, tm, out_dtype, scale=1.0):
    T, D = x2d.shape
    N = w_bf.shape[1]
    return pl.pallas_call(
        functools.partial(_norm_matmul_kernel, scale=scale),
        grid=(T // tm,),
        in_specs=[pl.BlockSpec((tm, D), lambda i: (i, 0)),
                  pl.BlockSpec((1, D), lambda i: (0, 0)),
                  pl.BlockSpec((D, N), lambda i: (0, 0))],
        out_specs=pl.BlockSpec((tm, N), lambda i: (i, 0)),
        out_shape=jax.ShapeDtypeStruct((T, N), out_dtype),
        compiler_params=_params(("parallel",)),
        name="norm_matmul",
    )(x2d, g.reshape(1, D), w_bf)


def _matmul_norm_res_kernel(*refs, n_in):
    a_refs = refs[:n_in]
    w_refs = refs[n_in:2 * n_in]
    g_ref, res_ref, o_ref = refs[2 * n_in:]
    z = None
    for a_ref, w_ref in zip(a_refs, w_refs):
        t = jnp.dot(a_ref[...].astype(BF16), w_ref[...], preferred_element_type=F32)
        z = t if z is None else z + t
    o_ref[...] = res_ref[...] + _rms(z, g_ref[...])


def _matmul_norm_residual(acts, ws_bf, g, res, tm):
    T, D = res.shape
    n_in = len(acts)
    in_specs = ([pl.BlockSpec((tm, a.shape[1]), lambda i: (i, 0)) for a in acts]
                + [pl.BlockSpec(w.shape, lambda i: (0, 0)) for w in ws_bf]
                + [pl.BlockSpec((1, D), lambda i: (0, 0)), pl.BlockSpec((tm, D), lambda i: (i, 0))])
    return pl.pallas_call(
        functools.partial(_matmul_norm_res_kernel, n_in=n_in),
        grid=(T // tm,),
        in_specs=in_specs,
        out_specs=pl.BlockSpec((tm, D), lambda i: (i, 0)),
        out_shape=jax.ShapeDtypeStruct((T, D), F32),
        compiler_params=_params(("parallel",)),
        name="matmul_norm_residual",
    )(*acts, *ws_bf, g.reshape(1, D), res)


def _softmax_rows(s):
    e = jnp.exp(s - jnp.max(s, axis=-1, keepdims=True))
    return e / jnp.sum(e, axis=-1, keepdims=True)


def _cross_sample_kernel(bias_ref, q_ref, mk_ref, mv_ref, o_ref, *, group, seq):
    bias = bias_ref[...]
    outs = []
    for b in range(group):
        qb = q_ref[b * seq:(b + 1) * seq, :]
        qh = jnp.concatenate([qb[:, h * MEM_DH:(h + 1) * MEM_DH] for h in range(MEM_HEADS)], axis=0)
        s = lax.dot_general(qh.astype(BF16), mk_ref[b].astype(BF16), _NT, preferred_element_type=F32) + bias
        p = _softmax_rows(s).astype(BF16)
        o = jnp.dot(p, mv_ref[b].astype(BF16), preferred_element_type=F32)
        outs.append(jnp.concatenate([o[h * seq:(h + 1) * seq] for h in range(MEM_HEADS)], axis=1))
    o_ref[...] = jnp.concatenate(outs, axis=0)


def _cross_attention_sample(q, mem_k, mem_v, layer, batch, seq, group):
    depth, _, M = mem_k.shape[:3]
    W = MEM_HEADS * MEM_DH
    mk = mem_k.reshape(depth, batch, M * MEM_HEADS, MEM_DH)
    mv = mem_v.reshape(depth, batch, M * MEM_HEADS, MEM_DH)
    bias = _head_match_bias(MEM_HEADS * seq, seq, M * MEM_HEADS, MEM_HEADS)
    kv = pl.BlockSpec((None, group, M * MEM_HEADS, MEM_DH), lambda i: (layer, i, 0, 0))
    row = pl.BlockSpec((group * seq, W), lambda i: (i, 0))
    return pl.pallas_call(
        functools.partial(_cross_sample_kernel, group=group, seq=seq),
        grid=(batch // group,),
        in_specs=[pl.BlockSpec(bias.shape, lambda i: (0, 0)), row, kv, kv],
        out_specs=row,
        out_shape=jax.ShapeDtypeStruct((batch * seq, W), F32),
        compiler_params=_params(("parallel",)),
        name="cross_attention_sample",
    )(bias, q, mk, mv)


def _mlp_begin(x_ref, gpre_ref, o_ref, h_scr):
    h_scr[...] = _rms(x_ref[...], gpre_ref[...]).astype(BF16)
    o_ref[...] = jnp.zeros_like(o_ref)


def _mlp_tile(wup_ref, wdn_ref, o_ref, h_scr):
    u = jnp.maximum(jnp.dot(h_scr[...], wup_ref[...], preferred_element_type=F32), 0.0)
    o_ref[...] += jnp.dot((u * u).astype(BF16), wdn_ref[...], preferred_element_type=F32)


def _mlp_finish(x_ref, gpost_ref, o_ref):
    o_ref[...] = x_ref[...] + _rms(o_ref[...], gpost_ref[...])


def _mlp_kernel(x_ref, gpre_ref, wup_ref, wdn_ref, gpost_ref, o_ref, h_scr):
    j = pl.program_id(1)
    pl.when(j == 0)(lambda: _mlp_begin(x_ref, gpre_ref, o_ref, h_scr))
    _mlp_tile(wup_ref, wdn_ref, o_ref, h_scr)
    pl.when(j == pl.num_programs(1) - 1)(lambda: _mlp_finish(x_ref, gpost_ref, o_ref))


def _mlp(x2d, g_pre, w_up_bf, w_dn_bf, g_post, tm, tf):
    T, D = x2d.shape
    FF = w_up_bf.shape[1]
    vec = pl.BlockSpec((1, D), lambda i, j: (0, 0))
    return pl.pallas_call(
        _mlp_kernel,
        grid=(T // tm, FF // tf),
        in_specs=[pl.BlockSpec((tm, D), lambda i, j: (i, 0)), vec,
                  pl.BlockSpec((D, tf), lambda i, j: (0, j)),
                  pl.BlockSpec((tf, D), lambda i, j: (j, 0)), vec],
        out_specs=pl.BlockSpec((tm, D), lambda i, j: (i, 0)),
        out_shape=jax.ShapeDtypeStruct((T, D), F32),
        scratch_shapes=[pltpu.VMEM((tm, D), BF16)],
        compiler_params=_params(("parallel", "arbitrary")),
        name="mlp",
    )(x2d, g_pre.reshape(1, D), w_up_bf, w_dn_bf, g_post.reshape(1, D))


def _position_tables(pos, tm):
    reps = max(1, tm // pos.shape[0])
    pos = jnp.tile(pos, reps)
    cr, srl, srh = _rope_tables(pos, RET_DK, RET_THETA)
    cd, sdl, sdh = _rope_tables(pos, ROPE_DIM, ROPE_THETA)
    return (cr, srl + srh, cd, sdl, sdh), pos.shape[0] // tm


def _layer_tail(x2d, mix_parts, wts, cross_fn, tm):
    (w_out_parts, g_mix_post, g_mem_pre, w_mem_q, w_mem_o, g_mem_post,
     g_mlp_pre, w_up, w_down, g_mlp_post) = wts
    x1 = _matmul_norm_residual(mix_parts, w_out_parts, g_mix_post, x2d, tm)
    q = cross_fn(x1, g_mem_pre, w_mem_q)
    x2 = _matmul_norm_residual([q], [w_mem_o], g_mem_post, x1, tm)
    return _mlp(x2, g_mlp_pre, w_up, w_down, g_mlp_post, tm, MLP_FF_TILE)


def _out_cross_kernel(mr_ref, md_ref, x_ref, wo1_ref, wo2_ref, g1_ref, gpre_ref, wq_ref, mk_ref, mv_ref,
                      wmo_ref, g2_ref, o_ref):
    z = (jnp.dot(mr_ref[...], wo1_ref[...], preferred_element_type=F32)
         + jnp.dot(md_ref[...], wo2_ref[...], preferred_element_type=F32))
    x1 = x_ref[...] + _rms(z, g1_ref[...])
    h = _rms(x1, gpre_ref[...]).astype(BF16)
    q = (jnp.dot(h, wq_ref[...], preferred_element_type=F32) * (MEM_DH ** -0.5)).astype(BF16)
    outs = []
    for hd in range(MEM_HEADS):
        sl = slice(hd * MEM_DH, (hd + 1) * MEM_DH)
        s = lax.dot_general(q[:, sl], mk_ref[0, :, sl], _NT, preferred_element_type=F32)
        p = _softmax_rows(s).astype(BF16)
        outs.append(jnp.dot(p, mv_ref[0, :, sl], preferred_element_type=F32).astype(BF16))
    y = jnp.dot(jnp.concatenate(outs, axis=1), wmo_ref[...], preferred_element_type=F32)
    o_ref[...] = x1 + _rms(y, g2_ref[...])


def _out_proj_cross_block(mix_parts, x2d, wts, mk_bf, mv_bf, batch, seq, tm):
    (w_out_parts, g_mix_post, g_mem_pre, w_mem_q, w_mem_o, g_mem_post) = wts[:6]
    T, D = x2d.shape
    n = seq // tm
    M, W = mk_bf.shape[1], mk_bf.shape[2]
    row = lambda width: pl.BlockSpec((tm, width), lambda b, i: (b * n + i, 0))
    const = lambda shape: pl.BlockSpec(shape, lambda b, i: (0,) * len(shape), pipeline_mode=pl.Buffered(1))
    kv = pl.BlockSpec((1, M, W), lambda b, i: (b, 0, 0))
    vec = lambda g: g.reshape(1, D)
    return pl.pallas_call(
        _out_cross_kernel,
        grid=(batch, n),
        in_specs=[row(mix_parts[0].shape[1]), row(mix_parts[1].shape[1]), row(D),
                  const(w_out_parts[0].shape), const(w_out_parts[1].shape), const((1, D)), const((1, D)),
                  const(w_mem_q.shape), kv, kv, const(w_mem_o.shape), const((1, D))],
        out_specs=row(D),
        out_shape=jax.ShapeDtypeStruct((T, D), F32),
        compiler_params=_params(("parallel", "parallel")),
        name="out_proj_cross_block",
    )(mix_parts[0], mix_parts[1], x2d, w_out_parts[0], w_out_parts[1], vec(g_mix_post), vec(g_mem_pre),
      w_mem_q, mk_bf, mv_bf, w_mem_o, vec(g_mem_post))


def kernel(x_prompt, x_sample, mem_prompt, state_ret, cache_diff_k, cache_diff_v, cache_mem_k, cache_mem_v, page_table, w_in, w_out, diff_lambda_q1, diff_lambda_k1, diff_lambda_q2, diff_lambda_k2, diff_subln_g, norm_mix_pre, norm_mix_post, norm_mem_pre, norm_mem_post, norm_mlp_pre, norm_mlp_post, mem_norm_g, w_mem_q, w_mem_k, w_mem_v, w_mem_o, w_mlp_up, w_mlp_down):
    depth = w_in.shape[0]
    B, L_p, D = x_prompt.shape
    B_s, L_s, _ = x_sample.shape
    n_pages, page = page_table.shape[1], cache_diff_k.shape[2]
    past_len = n_pages * page
    M = mem_prompt.shape[1]
    W_MEM = MEM_HEADS * MEM_DH
    TM = 512
    ret_chunk_p = math.gcd(L_p, RET_CHUNK)

    tabs_p, per_p = _position_tables(jnp.arange(L_p, dtype=F32), TM)
    tabs_s, per_s = _position_tables(past_len + jnp.arange(L_s, dtype=F32), TM)

    yp = x_prompt.reshape(B * L_p, D)
    ys = x_sample.reshape(B_s * L_s, D)
    mem2d = mem_prompt.reshape(B * M, D)
    outs = {k: [] for k in ("rp", "kp", "vp", "mkp", "mvp", "rs", "ks", "vs")}

    for i in range(depth):
        lam_init = 0.8 - 0.6 * math.exp(-0.3 * i)
        lam_rows = jnp.stack([diff_lambda_q1[i], diff_lambda_k1[i], diff_lambda_q2[i], diff_lambda_k2[i]])
        w_in_bf = w_in[i].astype(BF16)
        half = w_out.shape[1] // 2
        tail_w = ([w_out[i, :half].astype(BF16), w_out[i, half:].astype(BF16)], norm_mix_post[i],
                  norm_mem_pre[i], w_mem_q[i].astype(BF16), w_mem_o[i].astype(BF16), norm_mem_post[i],
                  norm_mlp_pre[i], w_mlp_up[i].astype(BF16), w_mlp_down[i].astype(BF16), norm_mlp_post[i])

        mk_p = _norm_matmul(mem2d, mem_norm_g[i], w_mem_k[i].astype(BF16), B * M, F32)
        mv_p = _norm_matmul(mem2d, mem_norm_g[i], w_mem_v[i].astype(BF16), B * M, F32)
        rq, rk, rv, rg, dq_t, dk, dv, dv_t = _in_projection(
            yp, norm_mix_pre[i], w_in_bf, tabs_p, per_p, TM, BF16, DIFF_DK ** -0.5 * LOG2E, True)
        mix_ret, s_p = _retention_prompt(rq, rk, rv, rg, B, L_p, ret_chunk_p)
        mix_diff = _diff_attention_prompt(dq_t, dk, dv_t, lam_rows, diff_subln_g[i], B, L_p, lam_init)
        mk_bf = mk_p.astype(BF16).reshape(B, M, W_MEM)
        mv_bf = mv_p.astype(BF16).reshape(B, M, W_MEM)

        x2 = _out_proj_cross_block([mix_ret, mix_diff], yp, tail_w, mk_bf, mv_bf, B, L_p, TM)
        outs["rp"].append(s_p.astype(state_ret.dtype))
        halves = DIFF_DV // LANES
        outs["kp"].append(dk.reshape(B, L_p, DIFF_HEADS, 2, DIFF_DK))
        outs["vp"].append(dv.reshape(B, L_p, halves, DIFF_HEADS, LANES).transpose(0, 1, 3, 2, 4)
                          .reshape(B, L_p, DIFF_HEADS, DIFF_DV))
        outs["mkp"].append(mk_p.reshape(B, M, MEM_HEADS, MEM_DH))
        outs["mvp"].append(mv_p.reshape(B, M, MEM_HEADS, MEM_DH))

        rq, rk, rv, rg, dq, dk, dv = _in_projection(ys, norm_mix_pre[i], w_in_bf, tabs_s, per_s, TM, F32,
                                                    DIFF_DK ** -0.5, False)
        mix_ret, s_s = _retention_sample(rq, rk, rv, rg, state_ret, i, B_s, L_s, 4)
        yp, mix_diff = _mlp_and_sample_attention(
            x2, norm_mlp_pre[i], tail_w[7], tail_w[8], norm_mlp_post[i], TM, MLP_FF_TILE_FUSED,
            dq, dk, dv, cache_diff_k, cache_diff_v, i, page_table, lam_rows, diff_subln_g[i],
            B_s, L_s, lam_init, PAGES_PER_STEP)
        def cross_s(x1, g, wq):
            q = _norm_matmul(x1, g, wq, TM, F32, scale=MEM_DH ** -0.5)
            return _cross_attention_sample(q, cache_mem_k, cache_mem_v, i, B_s, L_s, 8)

        ys = _layer_tail(ys, [mix_ret, mix_diff], tail_w, cross_s, TM)
        outs["rs"].append(s_s.astype(state_ret.dtype))
        outs["ks"].append(dk.reshape(B_s, L_s, DIFF_HEADS, 2, DIFF_DK))
        outs["vs"].append(dv.reshape(B_s, L_s, DIFF_HEADS, DIFF_DV))

    st = lambda k: outs[k][0][None] if depth == 1 else jnp.stack(outs[k])
    return (yp.reshape(B, L_p, D), ys.reshape(B_s, L_s, D), st("rp"), st("kp"), st("vp"),
            st("mkp"), st("mvp"), st("rs"), st("ks"), st("vs"))
```

```python
import functools
import math

import jax
import jax.numpy as jnp
from jax import lax
from jax.experimental import pallas as pl
from jax.experimental.pallas import tpu as pltpu

F32 = jnp.float32
BF16 = jnp.bfloat16

LANES = 128
MXU_COLS = 256
RET_HEADS = 8
RET_DK = 128
RET_DV = 128
RET_THETA = 10000.0
RET_CHUNK = 2 * 128
DIFF_HEADS = 4
DIFF_DK = 128
DIFF_DV = 256
ROPE_THETA = 500000.0
ROPE_DIM = DIFF_DK // 4
MEM_HEADS = 4
MEM_DH = 128
Q_BLOCK = 128
EPS = 1e-6
LOG2E = math.log2(math.e)
SEG = 1024
N_SEG = 7
MLP_FF_TILE = 1024
MLP_FF_TILE_FUSED = 512
PAGES_PER_STEP = 8
PAGE_DMA_PRIORITY = 1

VMEM_LIMIT = 56 * 1024 * 1024

_NT = (((1,), (1,)), ((), ()))
_TN = (((0,), (0,)), ((), ()))


def _params(sem, vmem=VMEM_LIMIT):
    return pltpu.CompilerParams(dimension_semantics=sem, vmem_limit_bytes=vmem)


def _rms(x, g):
    return x * lax.rsqrt(jnp.mean(x * x, axis=-1, keepdims=True) + EPS) * g


def _unit_rms(o):
    return o * lax.rsqrt(jnp.mean(o * o, axis=-1, keepdims=True) + EPS)


def _rope_tables(pos, rot_dim, theta):
    half = rot_dim // 2
    inv = jnp.exp(-math.log(theta) * (2.0 * jnp.arange(half, dtype=F32) / rot_dim))
    ang = pos[:, None] * inv[None, :]
    cos, sin = jnp.cos(ang), jnp.sin(ang)
    n = pos.shape[0]
    rest = LANES - rot_dim
    c = jnp.concatenate([cos, cos, jnp.ones((n, rest), F32)], axis=-1)
    s_lo = jnp.concatenate([-sin, jnp.zeros((n, LANES - half), F32)], axis=-1)
    s_hi = jnp.concatenate([jnp.zeros((n, half), F32), sin, jnp.zeros((n, rest), F32)], axis=-1)
    return c, s_lo, s_hi


def _inproj_kernel(x_ref, g_ref, w_ref, cr_ref, sr_ref, cd_ref, sdl_ref, sdh_ref, *rest,
                   dq_scale, transposed):
    if transposed:
        rq_ref, rk_ref, rv_ref, rg_ref, dq_ref, dk_ref, dv_ref, dvt_ref, h_scr = rest
    else:
        rq_ref, rk_ref, rv_ref, rg_ref, dq_ref, dk_ref, dv_ref, h_scr = rest
    j = pl.program_id(1)
    heads = [slice(h * LANES, (h + 1) * LANES) for h in range(SEG // LANES)]

    @pl.when(j == 0)
    def _():
        h_scr[...] = _rms(x_ref[...], g_ref[...]).astype(BF16)

    def project():
        return jnp.dot(h_scr[...], w_ref[...], preferred_element_type=F32)

    def ret_rot(out_ref, scale):
        z_all = project()
        c, s = cr_ref[...], sr_ref[...]
        for sl in heads:
            z = z_all[:, sl]
            r = z * c + pltpu.roll(z, RET_DK // 2, 1) * s
            if scale != 1.0:
                r = r * scale
            out_ref[:, sl] = r.astype(out_ref.dtype)

    def diff_rot(z):
        half = ROPE_DIM // 2
        return (z * cd_ref[...] + pltpu.roll(z, LANES - half, 1) * sdl_ref[...]
                + pltpu.roll(z, half, 1) * sdh_ref[...])

    @pl.when(j == 0)
    def _():
        ret_rot(rq_ref, 1.0)

    @pl.when(j == 1)
    def _():
        ret_rot(rk_ref, RET_DK ** -0.5)

    @pl.when(j == 2)
    def _():
        rv_ref[...] = project().astype(rv_ref.dtype)

    @pl.when(j == 3)
    def _():
        rg_ref[...] = project()

    @pl.when(j == 4)
    def _():
        z_all = project()
        for sl in heads:
            r = diff_rot(z_all[:, sl]) * dq_scale
            if transposed:
                dq_ref[sl, :] = r.T.astype(dq_ref.dtype)
            else:
                dq_ref[:, sl] = r.astype(dq_ref.dtype)

    tm = x_ref.shape[0]
    n_heads = len(heads)

    @pl.when(j == 5)
    def _():
        z_all = project()
        for i, sl in enumerate(heads):
            r = diff_rot(z_all[:, sl])
            if transposed:
                dk_ref[pl.ds(i, tm, stride=n_heads), :] = r
            else:
                dk_ref[:, sl] = r

    @pl.when(j == 6)
    def _():
        z_all = project()
        if transposed:
            for i, sl in enumerate(heads):
                h, e = divmod(i, DIFF_DV // LANES)
                dv_ref[pl.ds(e * DIFF_HEADS + h, tm, stride=n_heads), :] = z_all[:, sl]
                dvt_ref[sl, :] = z_all[:, sl].T.astype(dvt_ref.dtype)
        else:
            dv_ref[...] = z_all


def _in_projection(x2d, g, w_bf, tabs, period_blocks, tm, act_dtype, dq_scale, transposed):
    T, D = x2d.shape
    cr, sr, cd, sdl, sdh = tabs
    tab_spec = pl.BlockSpec((tm, LANES), lambda i, j: (i % period_blocks, 0))
    row_spec = pl.BlockSpec((tm, SEG), lambda i, j: (i, 0))
    col_spec = pl.BlockSpec((SEG, tm), lambda i, j: (0, i))
    sds = lambda dt: jax.ShapeDtypeStruct((T, SEG), dt)
    sds_t = lambda dt: jax.ShapeDtypeStruct((SEG, T), dt)
    n_heads = SEG // LANES
    cache_spec = pl.BlockSpec((tm * n_heads, LANES), lambda i, j: (i, 0))
    sds_c = jax.ShapeDtypeStruct((T * n_heads, LANES), F32)
    if transposed:
        out_specs = [row_spec] * 4 + [col_spec, cache_spec, cache_spec, col_spec]
        out_shape = [sds(act_dtype), sds(act_dtype), sds(act_dtype), sds(F32),
                     sds_t(act_dtype), sds_c, sds_c, sds_t(act_dtype)]
    else:
        out_specs = [pl.BlockSpec((tm, SEG), lambda i, j: (i, 0), pipeline_mode=pl.Buffered(1))] * N_SEG
        out_shape = [sds(act_dtype), sds(act_dtype), sds(act_dtype), sds(F32), sds(act_dtype), sds(F32), sds(F32)]
    return pl.pallas_call(
        functools.partial(_inproj_kernel, dq_scale=dq_scale, transposed=transposed),
        grid=(T // tm, N_SEG),
        in_specs=[
            pl.BlockSpec((tm, D), lambda i, j: (i, 0)),
            pl.BlockSpec((1, D), lambda i, j: (0, 0)),
            pl.BlockSpec((D, SEG), lambda i, j: (0, j)),
            tab_spec, tab_spec, tab_spec, tab_spec, tab_spec,
        ],
        out_specs=out_specs,
        out_shape=out_shape,
        scratch_shapes=[pltpu.VMEM((tm, D), BF16)],
        compiler_params=_params(("parallel", "arbitrary")),
        name="in_projection",
    )(x2d, g.reshape(1, D), w_bf, cr, sr, cd, sdl, sdh)


def _ret_decay_tables(chunk):
    lg = jnp.log1p(-jnp.exp2(-5.0 - jnp.arange(RET_HEADS, dtype=F32)))
    idx = jnp.arange(chunk, dtype=F32)
    rel = idx[:, None] - idx[None, :]
    dmat = jnp.where(rel[None] >= 0, jnp.exp(jnp.maximum(rel, 0.0)[None] * lg[:, None, None]), 0.0)
    q_decay = jnp.exp((idx + 1.0)[:, None] * lg[None, :])
    k_decay = jnp.exp((chunk - 1.0 - idx)[:, None] * lg[None, :])
    chunk_decay = jnp.exp(chunk * lg)
    widen = lambda t: jnp.repeat(t, RET_DK, axis=1)
    return dmat, widen(q_decay), widen(k_decay), chunk_decay


def _ret_head(q, k, v, s, dmat, qd, kd, cd, gate):
    qb, kb, vb = q.astype(BF16), k.astype(BF16), v.astype(BF16)
    att = lax.dot_general(qb, kb, _NT, preferred_element_type=F32) * dmat
    o = (jnp.dot(att.astype(BF16), vb, preferred_element_type=F32)
         + jnp.dot(qb, s.astype(BF16), preferred_element_type=F32) * qd)
    kdk = (k.astype(F32) * kd).astype(BF16)
    s_new = s * cd + lax.dot_general(kdk, vb, _TN, preferred_element_type=F32)
    y = _unit_rms(o) * (gate * jax.nn.sigmoid(gate))
    return y, s_new


def _ret_prompt_kernel(cd_ref, q_ref, k_ref, v_ref, g_ref, dmat_ref, qd_ref, kd_ref, mix_ref, s_ref):
    @pl.when(pl.program_id(1) == 0)
    def _():
        s_ref[...] = jnp.zeros_like(s_ref)

    for h in range(RET_HEADS):
        sl = slice(h * RET_DK, (h + 1) * RET_DK)
        y, s_new = _ret_head(q_ref[:, sl], k_ref[:, sl], v_ref[:, sl], s_ref[0, h], dmat_ref[h],
                             qd_ref[:, sl], kd_ref[:, sl], cd_ref[h], g_ref[:, sl])
        s_ref[0, h] = s_new
        mix_ref[:, sl] = y.astype(mix_ref.dtype)


def _retention_prompt(rq, rk, rv, rg, batch, seq, chunk):
    n = seq // chunk
    dmat, qd, kd, cd = _ret_decay_tables(chunk)
    row = pl.BlockSpec((chunk, SEG), lambda b, c: (b * n + c, 0))
    full2 = pl.BlockSpec((chunk, SEG), lambda b, c: (0, 0))
    return pl.pallas_call(
        _ret_prompt_kernel,
        grid=(batch, n),
        in_specs=[
            pl.BlockSpec(memory_space=pltpu.SMEM),
            row, row, row, row,
            pl.BlockSpec((RET_HEADS, chunk, chunk), lambda b, c: (0, 0, 0)),
            full2, full2,
        ],
        out_specs=[row, pl.BlockSpec((1, RET_HEADS, RET_DK, RET_DV), lambda b, c: (b, 0, 0, 0))],
        out_shape=[jax.ShapeDtypeStruct((batch * seq, SEG), BF16),
                   jax.ShapeDtypeStruct((batch, RET_HEADS, RET_DK, RET_DV), F32)],
        compiler_params=_params(("parallel", "arbitrary")),
        name="retention_prompt",
    )(cd, rq, rk, rv, rg, dmat, qd, kd)


def _ret_sample_kernel(cd_ref, q_ref, k_ref, v_ref, g_ref, s0_ref, dmat_ref, qd_ref, kd_ref,
                       mix_ref, s_ref, *, group, seq):
    rows = []
    for b in range(group):
        r = slice(b * seq, (b + 1) * seq)
        heads = []
        for h in range(RET_HEADS):
            sl = slice(h * RET_DK, (h + 1) * RET_DK)
            y, s_new = _ret_head(q_ref[r, sl], k_ref[r, sl], v_ref[r, sl], s0_ref[b, h], dmat_ref[h],
                                 qd_ref[:, sl], kd_ref[:, sl], cd_ref[h], g_ref[r, sl])
            s_ref[b, h] = s_new
            heads.append(y)
        rows.append(jnp.concatenate(heads, axis=1))
    mix_ref[...] = jnp.concatenate(rows, axis=0)


def _retention_sample(rq, rk, rv, rg, state, layer, batch, seq, group):
    dmat, qd, kd, cd = _ret_decay_tables(seq)
    row = pl.BlockSpec((group * seq, SEG), lambda i: (i, 0))
    tab = pl.BlockSpec((seq, SEG), lambda i: (0, 0))
    st = pl.BlockSpec((group, RET_HEADS, RET_DK, RET_DV), lambda i: (i, 0, 0, 0))
    st_in = pl.BlockSpec((None, group, RET_HEADS, RET_DK, RET_DV), lambda i: (layer, i, 0, 0, 0))
    return pl.pallas_call(
        functools.partial(_ret_sample_kernel, group=group, seq=seq),
        grid=(batch // group,),
        in_specs=[
            pl.BlockSpec(memory_space=pltpu.SMEM),
            row, row, row, row, st_in,
            pl.BlockSpec((RET_HEADS, seq, seq), lambda i: (0, 0, 0)),
            tab, tab,
        ],
        out_specs=[row, st],
        out_shape=[jax.ShapeDtypeStruct((batch * seq, SEG), F32),
                   jax.ShapeDtypeStruct((batch, RET_HEADS, RET_DK, RET_DV), F32)],
        compiler_params=_params(("parallel",)),
        name="retention_sample",
    )(cd, rq, rk, rv, rg, state, dmat, qd, kd)


def _lambda_value(lam_ref, lam_init):
    a = jnp.sum(lam_ref[0:1, :] * lam_ref[1:2, :], axis=-1, keepdims=True)
    b = jnp.sum(lam_ref[2:3, :] * lam_ref[3:4, :], axis=-1, keepdims=True)
    return jnp.exp(a) - jnp.exp(b) + lam_init


def _softmax_update(s, m_old, l_old):
    m_new = jnp.maximum(m_old, jnp.max(s, axis=-1, keepdims=True))
    alpha = jnp.exp(m_old - m_new)
    p = jnp.exp(s - m_new)
    l_new = alpha * l_old + jnp.sum(p, axis=-1, keepdims=True)
    return p, alpha, m_new, l_new


def _diff_prompt_kernel(qi_tab, ki_tab, lam_ref, qt_ref, k_ref, vt_ref, g_ref, o_ref, m_scr, l_scr, acc_scr,
                        *, tq, tk, lam_init):
    t = pl.program_id(1)
    qi, ki = qi_tab[t], ki_tab[t]
    n_sub = 2 * DIFF_HEADS

    @pl.when(ki == 0)
    def _():
        m_scr[...] = jnp.full_like(m_scr, -jnp.inf)
        l_scr[...] = jnp.zeros_like(l_scr)
        acc_scr[...] = jnp.zeros_like(acc_scr)

    def step(masked):
        if masked:
            kpos = ki * tk + lax.broadcasted_iota(jnp.int32, (tk, tq), 0)
            qpos = qi * tq + lax.broadcasted_iota(jnp.int32, (tk, tq), 1)
            keep = qpos >= kpos
        for h in range(DIFF_HEADS):
            vt = vt_ref[h * DIFF_DV:(h + 1) * DIFF_DV, :]
            for m in range(2):
                i = 2 * h + m
                sl = slice(i * DIFF_DK, (i + 1) * DIFF_DK)
                kb = k_ref[pl.ds(i, tk, stride=n_sub), :].astype(BF16)
                for c in range(tq // MXU_COLS):
                    cs = slice(c * MXU_COLS, (c + 1) * MXU_COLS)
                    st = jnp.dot(kb, qt_ref[sl, cs], preferred_element_type=F32)
                    if masked:
                        st = jnp.where(keep[:, cs], st, -jnp.inf)
                    m_old = m_scr[i, :, cs]
                    m_new = jnp.maximum(m_old, jnp.max(st, axis=0, keepdims=True))
                    alpha = jnp.exp2(m_old - m_new)
                    p = jnp.exp2(st - m_new)
                    l_scr[i, :, cs] = alpha * l_scr[i, :, cs] + jnp.sum(p, axis=0, keepdims=True)
                    m_scr[i, :, cs] = m_new
                    acc_scr[i, :, cs] = (alpha * acc_scr[i, :, cs]
                                         + jnp.dot(vt, p.astype(BF16), preferred_element_type=F32))

    @pl.when(ki < qi)
    def _():
        step(False)

    @pl.when(ki == qi)
    def _():
        step(True)
        lam = _lambda_value(lam_ref, lam_init)
        for h in range(DIFF_HEADS):
            a, b = 2 * h, 2 * h + 1
            ot = acc_scr[a] * (1.0 / l_scr[a]) - lam * (acc_scr[b] * (1.0 / l_scr[b]))
            yt = ot * lax.rsqrt(jnp.mean(ot * ot, axis=0, keepdims=True) + EPS)
            o_ref[:, h * DIFF_DV:(h + 1) * DIFF_DV] = (yt.T * g_ref[...] * (1.0 - lam_init)).astype(o_ref.dtype)


def _diff_attention_prompt(dq_t, dk, dv_t, lam_rows, subln_g, batch, seq, lam_init, tq=512, tk=512):
    assert tq == tk
    nq = seq // tq
    pairs = [(qi, ki) for qi in range(nq) for ki in range(qi + 1)]
    qi_tab = jnp.asarray([p[0] for p in pairs], jnp.int32)
    ki_tab = jnp.asarray([p[1] for p in pairs], jnp.int32)
    n_sub = 2 * DIFF_HEADS
    grid_spec = pltpu.PrefetchScalarGridSpec(
        num_scalar_prefetch=2,
        grid=(batch, len(pairs)),
        in_specs=[
            pl.BlockSpec((4, DIFF_DK), lambda b, t, qt, kt: (0, 0)),
            pl.BlockSpec((SEG, tq), lambda b, t, qt, kt: (0, b * nq + qt[t])),
            pl.BlockSpec((tk * n_sub, DIFF_DK), lambda b, t, qt, kt: (b * nq + kt[t], 0)),
            pl.BlockSpec((SEG, tk), lambda b, t, qt, kt: (0, b * nq + kt[t])),
            pl.BlockSpec((1, DIFF_DV), lambda b, t, qt, kt: (0, 0)),
        ],
        out_specs=pl.BlockSpec((tq, SEG), lambda b, t, qt, kt: (b * nq + qt[t], 0)),
        scratch_shapes=[pltpu.VMEM((n_sub, 1, tq), F32), pltpu.VMEM((n_sub, 1, tq), F32),
                        pltpu.VMEM((n_sub, DIFF_DV, tq), F32)],
    )
    return pl.pallas_call(
        functools.partial(_diff_prompt_kernel, tq=tq, tk=tk, lam_init=lam_init),
        grid_spec=grid_spec,
        out_shape=jax.ShapeDtypeStruct((batch * seq, SEG), BF16),
        compiler_params=_params(("parallel", "arbitrary")),
        name="diff_attention_prompt",
    )(qi_tab, ki_tab, lam_rows, dq_t, dk, dv_t, subln_g.reshape(1, DIFF_DV))


def _head_match_bias(n_rows, seq, n_cols, heads):
    row_h = jnp.arange(n_rows, dtype=jnp.int32)[:, None] // seq
    col_h = jnp.arange(n_cols, dtype=jnp.int32)[None, :] % heads
    return jnp.where(row_h == col_h, 0.0, -jnp.inf).astype(F32)


def _stack_heads(x, width, offset, stride):
    return jnp.concatenate(
        [x[:, offset + h * stride: offset + h * stride + width] for h in range(DIFF_HEADS)], axis=0)


def _stacked_queries(q_ref):
    q = q_ref[...]
    return [_stack_heads(q, DIFF_DK, m * DIFF_DK, 2 * DIFF_DK).astype(BF16) for m in range(2)]


def _diff_sample_begin(m_scr, l_scr, acc_scr):
    m_scr[...] = jnp.full_like(m_scr, -jnp.inf)
    l_scr[...] = jnp.zeros_like(l_scr)
    acc_scr[...] = jnp.zeros_like(acc_scr)


def _diff_sample_pages(bias_ref, q_ref, k_refs, v_refs, m_scr, l_scr, acc_scr, *, seq):
    rows = DIFF_HEADS * seq
    kv_rows = k_refs[0].shape[0] // 2
    qs = _stacked_queries(q_ref)
    bias = bias_ref[...]
    ps, alphas = [], []
    for m in range(2):
        s = jnp.concatenate(
            [lax.dot_general(qs[m], kr[pl.ds(m, kv_rows, stride=2), :].astype(BF16), _NT,
                             preferred_element_type=F32) + bias for kr in k_refs], axis=1)
        p, alpha, m_new, l_new = _softmax_update(s, m_scr[m], l_scr[m])
        m_scr[m] = m_new
        l_scr[m] = l_new
        ps.append(p)
        alphas.append(alpha)
    p = jnp.concatenate(ps, axis=0).astype(BF16)
    for e in range(2):
        pv = None
        for n in range(len(v_refs)):
            v = v_refs[n][:, :, e * LANES:(e + 1) * LANES].reshape(kv_rows, LANES).astype(BF16)
            t = jnp.dot(p[:, n * kv_rows:(n + 1) * kv_rows], v, preferred_element_type=F32)
            pv = t if pv is None else pv + t
        for m in range(2):
            cols = slice(e * LANES, (e + 1) * LANES)
            acc_scr[m, :, cols] = alphas[m] * acc_scr[m, :, cols] + pv[m * rows:(m + 1) * rows]


def _diff_sample_finish(lam_ref, q_ref, kn_ref, vn_ref, g_ref, o_ref, m_scr, l_scr, acc_scr, *, seq, lam_init):
    qs = _stacked_queries(q_ref)
    kn, vn = kn_ref[...], vn_ref[...]
    causal = (lax.broadcasted_iota(jnp.int32, (seq, seq), 0)
              >= lax.broadcasted_iota(jnp.int32, (seq, seq), 1))
    accs, ls = [], []
    for m in range(2):
        s = jnp.concatenate(
            [jnp.where(causal,
                       lax.dot_general(qs[m][h * seq:(h + 1) * seq],
                                       kn[:, (2 * h + m) * DIFF_DK:(2 * h + m + 1) * DIFF_DK].astype(BF16),
                                       _NT, preferred_element_type=F32),
                       -jnp.inf) for h in range(DIFF_HEADS)], axis=0)
        p, alpha, _, l_new = _softmax_update(s, m_scr[m], l_scr[m])
        pn = p.astype(BF16)
        pv = jnp.concatenate(
            [jnp.dot(pn[h * seq:(h + 1) * seq], vn[:, h * DIFF_DV:(h + 1) * DIFF_DV].astype(BF16),
                     preferred_element_type=F32) for h in range(DIFF_HEADS)], axis=0)
        accs.append(alpha * acc_scr[m] + pv)
        ls.append(l_new)
    lam = _lambda_value(lam_ref, lam_init)
    o = accs[0] / ls[0] - lam * (accs[1] / ls[1])
    y = _unit_rms(o) * g_ref[...] * (1.0 - lam_init)
    o_ref[...] = jnp.concatenate([y[h * seq:(h + 1) * seq] for h in range(DIFF_HEADS)], axis=1)


def _mlp_and_sample_attention_kernel(pt_ref, x_ref, gpre_ref, wup_ref, wdn_ref, gpost_ref,
                                     lam_ref, bias_ref, q_ref, kn_ref, vn_ref, g_ref, ck_hbm, cv_hbm,
                                     y_ref, o_ref, h_scr, m_scr, l_scr, acc_scr, k_buf, v_buf, sem,
                                     *, layer, n_pages_step, attn_steps, seq, lam_init):
    j, n_j = pl.program_id(1), pl.num_programs(1)
    g = pl.program_id(0) * n_j + j
    total = pl.num_programs(0) * n_j
    step = lax.rem(g, attn_steps)
    slot = lax.rem(g, 2)

    def page_copies(at_step, into):
        b, s = lax.div(at_step, attn_steps), lax.rem(at_step, attn_steps)
        copies = []
        for n in range(n_pages_step):
            page = pt_ref[b, s * n_pages_step + n]
            copies.append(pltpu.make_async_copy(ck_hbm.at[layer, page], k_buf.at[into, n], sem.at[into]))
            copies.append(pltpu.make_async_copy(cv_hbm.at[layer, page], v_buf.at[into, n], sem.at[into]))
        return copies

    def start_all(copies):
        for c in copies:
            c.start(priority=PAGE_DMA_PRIORITY)

    pl.when(g == 0)(lambda: start_all(page_copies(g, slot)))
    pl.when(g + 1 < total)(lambda: start_all(page_copies(g + 1, 1 - slot)))
    pl.when(j == 0)(lambda: _mlp_begin(x_ref, gpre_ref, y_ref, h_scr))
    pl.when(step == 0)(lambda: _diff_sample_begin(m_scr, l_scr, acc_scr))
    for c in page_copies(g, slot):
        c.wait()
    k_refs = [k_buf.at[slot, n] for n in range(n_pages_step)]
    v_refs = [v_buf.at[slot, n] for n in range(n_pages_step)]
    _mlp_tile(wup_ref, wdn_ref, y_ref, h_scr)
    _diff_sample_pages(bias_ref, q_ref, k_refs, v_refs, m_scr, l_scr, acc_scr, seq=seq)
    pl.when(j == n_j - 1)(lambda: _mlp_finish(x_ref, gpost_ref, y_ref))
    pl.when(step == attn_steps - 1)(lambda: _diff_sample_finish(
        lam_ref, q_ref, kn_ref, vn_ref, g_ref, o_ref, m_scr, l_scr, acc_scr, seq=seq, lam_init=lam_init))


def _mlp_and_sample_attention(x2d, g_pre, w_up_bf, w_dn_bf, g_post, tm, tf,
                              dq, dk_new, dv_new, cache_k, cache_v, layer, page_table, lam_rows, subln_g,
                              batch, seq, lam_init, pages_per_step):
    T, D = x2d.shape
    FF = w_up_bf.shape[1]
    n_i, n_j = T // tm, FF // tf
    depth, n_pool, page = cache_k.shape[:3]
    n_pages = page_table.shape[1]
    attn_steps = n_pages // pages_per_step
    assert n_pages % pages_per_step == 0 and DIFF_DV == 2 * LANES
    assert n_i * n_j == batch * attn_steps, "the two jobs must have the same number of grid steps"
    k_rows = page * DIFF_HEADS * 2
    ck = cache_k.reshape(depth, n_pool, k_rows, DIFF_DK)
    rows = DIFF_HEADS * seq
    bias = _head_match_bias(rows, seq, page * DIFF_HEADS, DIFF_HEADS)

    vec = pl.BlockSpec((1, D), lambda i, j, pt: (0, 0))
    row = pl.BlockSpec((seq, SEG), lambda i, j, pt: ((i * n_j + j) // attn_steps, 0))
    hbm = pl.BlockSpec(memory_space=pl.ANY)
    grid_spec = pltpu.PrefetchScalarGridSpec(
        num_scalar_prefetch=1,
        grid=(n_i, n_j),
        in_specs=[pl.BlockSpec((tm, D), lambda i, j, pt: (i, 0)), vec,
                  pl.BlockSpec((D, tf), lambda i, j, pt: (0, j)),
                  pl.BlockSpec((tf, D), lambda i, j, pt: (j, 0)), vec,
                  pl.BlockSpec((4, DIFF_DK), lambda i, j, pt: (0, 0)),
                  pl.BlockSpec(bias.shape, lambda i, j, pt: (0, 0)),
                  row, row, row,
                  pl.BlockSpec((1, DIFF_DV), lambda i, j, pt: (0, 0)),
                  hbm, hbm],
        out_specs=[pl.BlockSpec((tm, D), lambda i, j, pt: (i, 0)), row],
        scratch_shapes=[pltpu.VMEM((tm, D), BF16),
                        pltpu.VMEM((2, rows, 1), F32), pltpu.VMEM((2, rows, 1), F32),
                        pltpu.VMEM((2, rows, DIFF_DV), F32),
                        pltpu.VMEM((2, pages_per_step, k_rows, DIFF_DK), F32),
                        pltpu.VMEM((2, pages_per_step, page, DIFF_HEADS, DIFF_DV), F32),
                        pltpu.SemaphoreType.DMA((2,))],
    )
    return pl.pallas_call(
        functools.partial(_mlp_and_sample_attention_kernel, layer=layer, n_pages_step=pages_per_step,
                          attn_steps=attn_steps, seq=seq, lam_init=lam_init),
        grid_spec=grid_spec,
        out_shape=[jax.ShapeDtypeStruct((T, D), F32), jax.ShapeDtypeStruct((batch * seq, SEG), F32)],
        compiler_params=_params(("arbitrary", "arbitrary")),
        name="mlp_and_sample_attention",
    )(page_table, x2d, g_pre.reshape(1, D), w_up_bf, w_dn_bf, g_post.reshape(1, D),
      lam_rows, bias, dq, dk_new, dv_new, subln_g.reshape(1, DIFF_DV), ck, cache_v)


def _norm_matmul_kernel(x_ref, g_ref, w_ref, o_ref, *, scale):
    h = _rms(x_ref[...], g_ref[...]).astype(BF16)
    z = jnp.dot(h, w_ref[...], preferred_element_type=F32)
    if scale != 1.0:
        z = z * scale
    o_ref[...] = z.astype(o_ref.dtype)


def _norm_matmul(x2d, g, w_bf, tm, out_dtype, scale=1.0):
    T, D = x2d.shape
    N = w_bf.shape[1]
    return pl.pallas_call(
        functools.partial(_norm_matmul_kernel, scale=scale),
        grid=(T // tm,),
        in_specs=[pl.BlockSpec((tm, D), lambda i: (i, 0)),
                  pl.BlockSpec((1, D), lambda i: (0, 0)),
                  pl.BlockSpec((D, N), lambda i: (0, 0))],
        out_specs=pl.BlockSpec((tm, N), lambda i: (i, 0)),
        out_shape=jax.ShapeDtypeStruct((T, N), out_dtype),
        compiler_params=_params(("parallel",)),
        name="norm_matmul",
    )(x2d, g.reshape(1, D), w_bf)


def _matmul_norm_res_kernel(*refs, n_in):
    a_refs = refs[:n_in]
    w_refs = refs[n_in:2 * n_in]
    g_ref, res_ref, o_ref = refs[2 * n_in:]
    z = None
    for a_ref, w_ref in zip(a_refs, w_refs):
        t = jnp.dot(a_ref[...].astype(BF16), w_ref[...], preferred_element_type=F32)
        z = t if z is None else z + t
    o_ref[...] = res_ref[...] + _rms(z, g_ref[...])


def _matmul_norm_residual(acts, ws_bf, g, res, tm):
    T, D = res.shape
    n_in = len(acts)
    in_specs = ([pl.BlockSpec((tm, a.shape[1]), lambda i: (i, 0)) for a in acts]
                + [pl.BlockSpec(w.shape, lambda i: (0, 0)) for w in ws_bf]
                + [pl.BlockSpec((1, D), lambda i: (0, 0)), pl.BlockSpec((tm, D), lambda i: (i, 0))])
    return pl.pallas_call(
        functools.partial(_matmul_norm_res_kernel, n_in=n_in),
        grid=(T // tm,),
        in_specs=in_specs,
        out_specs=pl.BlockSpec((tm, D), lambda i: (i, 0)),
        out_shape=jax.ShapeDtypeStruct((T, D), F32),
        compiler_params=_params(("parallel",)),
        name="matmul_norm_residual",
    )(*acts, *ws_bf, g.reshape(1, D), res)


def _softmax_rows(s):
    e = jnp.exp(s - jnp.max(s, axis=-1, keepdims=True))
    return e / jnp.sum(e, axis=-1, keepdims=True)


def _cross_sample_kernel(bias_ref, q_ref, mk_ref, mv_ref, o_ref, *, group, seq):
    bias = bias_ref[...]
    outs = []
    for b in range(group):
        qb = q_ref[b * seq:(b + 1) * seq, :]
        qh = jnp.concatenate([qb[:, h * MEM_DH:(h + 1) * MEM_DH] for h in range(MEM_HEADS)], axis=0)
        s = lax.dot_general(qh.astype(BF16), mk_ref[b].astype(BF16), _NT, preferred_element_type=F32) + bias
        p = _softmax_rows(s).astype(BF16)
        o = jnp.dot(p, mv_ref[b].astype(BF16), preferred_element_type=F32)
        outs.append(jnp.concatenate([o[h * seq:(h + 1) * seq] for h in range(MEM_HEADS)], axis=1))
    o_ref[...] = jnp.concatenate(outs, axis=0)


def _cross_attention_sample(q, mem_k, mem_v, layer, batch, seq, group):
    depth, _, M = mem_k.shape[:3]
    W = MEM_HEADS * MEM_DH
    mk = mem_k.reshape(depth, batch, M * MEM_HEADS, MEM_DH)
    mv = mem_v.reshape(depth, batch, M * MEM_HEADS, MEM_DH)
    bias = _head_match_bias(MEM_HEADS * seq, seq, M * MEM_HEADS, MEM_HEADS)
    kv = pl.BlockSpec((None, group, M * MEM_HEADS, MEM_DH), lambda i: (layer, i, 0, 0))
    row = pl.BlockSpec((group * seq, W), lambda i: (i, 0))
    return pl.pallas_call(
        functools.partial(_cross_sample_kernel, group=group, seq=seq),
        grid=(batch // group,),
        in_specs=[pl.BlockSpec(bias.shape, lambda i: (0, 0)), row, kv, kv],
        out_specs=row,
        out_shape=jax.ShapeDtypeStruct((batch * seq, W), F32),
        compiler_params=_params(("parallel",)),
        name="cross_attention_sample",
    )(bias, q, mk, mv)


def _mlp_begin(x_ref, gpre_ref, o_ref, h_scr):
    h_scr[...] = _rms(x_ref[...], gpre_ref[...]).astype(BF16)
    o_ref[...] = jnp.zeros_like(o_ref)


def _mlp_tile(wup_ref, wdn_ref, o_ref, h_scr):
    u = jnp.maximum(jnp.dot(h_scr[...], wup_ref[...], preferred_element_type=F32), 0.0)
    o_ref[...] += jnp.dot((u * u).astype(BF16), wdn_ref[...], preferred_element_type=F32)


def _mlp_finish(x_ref, gpost_ref, o_ref):
    o_ref[...] = x_ref[...] + _rms(o_ref[...], gpost_ref[...])


def _mlp_kernel(x_ref, gpre_ref, wup_ref, wdn_ref, gpost_ref, o_ref, h_scr):
    j = pl.program_id(1)
    pl.when(j == 0)(lambda: _mlp_begin(x_ref, gpre_ref, o_ref, h_scr))
    _mlp_tile(wup_ref, wdn_ref, o_ref, h_scr)
    pl.when(j == pl.num_programs(1) - 1)(lambda: _mlp_finish(x_ref, gpost_ref, o_ref))


def _mlp(x2d, g_pre, w_up_bf, w_dn_bf, g_post, tm, tf):
    T, D = x2d.shape
    FF = w_up_bf.shape[1]
    vec = pl.BlockSpec((1, D), lambda i, j: (0, 0))
    return pl.pallas_call(
        _mlp_kernel,
        grid=(T // tm, FF // tf),
        in_specs=[pl.BlockSpec((tm, D), lambda i, j: (i, 0)), vec,
                  pl.BlockSpec((D, tf), lambda i, j: (0, j)),
                  pl.BlockSpec((tf, D), lambda i, j: (j, 0)), vec],
        out_specs=pl.BlockSpec((tm, D), lambda i, j: (i, 0)),
        out_shape=jax.ShapeDtypeStruct((T, D), F32),
        scratch_shapes=[pltpu.VMEM((tm, D), BF16)],
        compiler_params=_params(("parallel", "arbitrary")),
        name="mlp",
    )(x2d, g_pre.reshape(1, D), w_up_bf, w_dn_bf, g_post.reshape(1, D))


def _position_tables(pos, tm):
    reps = max(1, tm // pos.shape[0])
    pos = jnp.tile(pos, reps)
    cr, srl, srh = _rope_tables(pos, RET_DK, RET_THETA)
    cd, sdl, sdh = _rope_tables(pos, ROPE_DIM, ROPE_THETA)
    return (cr, srl + srh, cd, sdl, sdh), pos.shape[0] // tm


def _layer_tail(x2d, mix_parts, wts, cross_fn, tm):
    (w_out_parts, g_mix_post, g_mem_pre, w_mem_q, w_mem_o, g_mem_post,
     g_mlp_pre, w_up, w_down, g_mlp_post) = wts
    x1 = _matmul_norm_residual(mix_parts, w_out_parts, g_mix_post, x2d, tm)
    q = cross_fn(x1, g_mem_pre, w_mem_q)
    x2 = _matmul_norm_residual([q], [w_mem_o], g_mem_post, x1, tm)
    return _mlp(x2, g_mlp_pre, w_up, w_down, g_mlp_post, tm, MLP_FF_TILE)


def _out_cross_kernel(mr_ref, md_ref, x_ref, wo1_ref, wo2_ref, g1_ref, gpre_ref, wq_ref, mk_ref, mv_ref,
                      wmo_ref, g2_ref, o_ref):
    z = (jnp.dot(mr_ref[...], wo1_ref[...], preferred_element_type=F32)
         + jnp.dot(md_ref[...], wo2_ref[...], preferred_element_type=F32))
    x1 = x_ref[...] + _rms(z, g1_ref[...])
    h = _rms(x1, gpre_ref[...]).astype(BF16)
    q = (jnp.dot(h, wq_ref[...], preferred_element_type=F32) * (MEM_DH ** -0.5)).astype(BF16)
    outs = []
    for hd in range(MEM_HEADS):
        sl = slice(hd * MEM_DH, (hd + 1) * MEM_DH)
        s = lax.dot_general(q[:, sl], mk_ref[0, :, sl], _NT, preferred_element_type=F32)
        p = _softmax_rows(s).astype(BF16)
        outs.append(jnp.dot(p, mv_ref[0, :, sl], preferred_element_type=F32).astype(BF16))
    y = jnp.dot(jnp.concatenate(outs, axis=1), wmo_ref[...], preferred_element_type=F32)
    o_ref[...] = x1 + _rms(y, g2_ref[...])


def _out_proj_cross_block(mix_parts, x2d, wts, mk_bf, mv_bf, batch, seq, tm):
    (w_out_parts, g_mix_post, g_mem_pre, w_mem_q, w_mem_o, g_mem_post) = wts[:6]
    T, D = x2d.shape
    n = seq // tm
    M, W = mk_bf.shape[1], mk_bf.shape[2]
    row = lambda width: pl.BlockSpec((tm, width), lambda b, i: (b * n + i, 0))
    const = lambda shape: pl.BlockSpec(shape, lambda b, i: (0,) * len(shape), pipeline_mode=pl.Buffered(1))
    kv = pl.BlockSpec((1, M, W), lambda b, i: (b, 0, 0))
    vec = lambda g: g.reshape(1, D)
    return pl.pallas_call(
        _out_cross_kernel,
        grid=(batch, n),
        in_specs=[row(mix_parts[0].shape[1]), row(mix_parts[1].shape[1]), row(D),
                  const(w_out_parts[0].shape), const(w_out_parts[1].shape), const((1, D)), const((1, D)),
                  const(w_mem_q.shape), kv, kv, const(w_mem_o.shape), const((1, D))],
        out_specs=row(D),
        out_shape=jax.ShapeDtypeStruct((T, D), F32),
        compiler_params=_params(("parallel", "parallel")),
        name="out_proj_cross_block",
    )(mix_parts[0], mix_parts[1], x2d, w_out_parts[0], w_out_parts[1], vec(g_mix_post), vec(g_mem_pre),
      w_mem_q, mk_bf, mv_bf, w_mem_o, vec(g_mem_post))


def kernel(x_prompt, x_sample, mem_prompt, state_ret, cache_diff_k, cache_diff_v, cache_mem_k, cache_mem_v, page_table, w_in, w_out, diff_lambda_q1, diff_lambda_k1, diff_lambda_q2, diff_lambda_k2, diff_subln_g, norm_mix_pre, norm_mix_post, norm_mem_pre, norm_mem_post, norm_mlp_pre, norm_mlp_post, mem_norm_g, w_mem_q, w_mem_k, w_mem_v, w_mem_o, w_mlp_up, w_mlp_down):
    depth = w_in.shape[0]
    B, L_p, D = x_prompt.shape
    B_s, L_s, _ = x_sample.shape
    n_pages, page = page_table.shape[1], cache_diff_k.shape[2]
    past_len = n_pages * page
    M = mem_prompt.shape[1]
    W_MEM = MEM_HEADS * MEM_DH
    TM = 512
    ret_chunk_p = math.gcd(L_p, RET_CHUNK)

    tabs_p, per_p = _position_tables(jnp.arange(L_p, dtype=F32), TM)
    tabs_s, per_s = _position_tables(past_len + jnp.arange(L_s, dtype=F32), TM)

    yp = x_prompt.reshape(B * L_p, D)
    ys = x_sample.reshape(B_s * L_s, D)
    mem2d = mem_prompt.reshape(B * M, D)
    outs = {k: [] for k in ("rp", "kp", "vp", "mkp", "mvp", "rs", "ks", "vs")}

    for i in range(depth):
        lam_init = 0.8 - 0.6 * math.exp(-0.3 * i)
        lam_rows = jnp.stack([diff_lambda_q1[i], diff_lambda_k1[i], diff_lambda_q2[i], diff_lambda_k2[i]])
        w_in_bf = w_in[i].astype(BF16)
        half = w_out.shape[1] // 2
        tail_w = ([w_out[i, :half].astype(BF16), w_out[i, half:].astype(BF16)], norm_mix_post[i],
                  norm_mem_pre[i], w_mem_q[i].astype(BF16), w_mem_o[i].astype(BF16), norm_mem_post[i],
                  norm_mlp_pre[i], w_mlp_up[i].astype(BF16), w_mlp_down[i].astype(BF16), norm_mlp_post[i])

        mk_p = _norm_matmul(mem2d, mem_norm_g[i], w_mem_k[i].astype(BF16), B * M, F32)
        mv_p = _norm_matmul(mem2d, mem_norm_g[i], w_mem_v[i].astype(BF16), B * M, F32)
        rq, rk, rv, rg, dq_t, dk, dv, dv_t = _in_projection(
            yp, norm_mix_pre[i], w_in_bf, tabs_p, per_p, TM, BF16, DIFF_DK ** -0.5 * LOG2E, True)
        mix_ret, s_p = _retention_prompt(rq, rk, rv, rg, B, L_p, ret_chunk_p)
        mix_diff = _diff_attention_prompt(dq_t, dk, dv_t, lam_rows, diff_subln_g[i], B, L_p, lam_init)
        mk_bf = mk_p.astype(BF16).reshape(B, M, W_MEM)
        mv_bf = mv_p.astype(BF16).reshape(B, M, W_MEM)

        x2 = _out_proj_cross_block([mix_ret, mix_diff], yp, tail_w, mk_bf, mv_bf, B, L_p, TM)
        outs["rp"].append(s_p.astype(state_ret.dtype))
        halves = DIFF_DV // LANES
        outs["kp"].append(dk.reshape(B, L_p, DIFF_HEADS, 2, DIFF_DK))
        outs["vp"].append(dv.reshape(B, L_p, halves, DIFF_HEADS, LANES).transpose(0, 1, 3, 2, 4)
                          .reshape(B, L_p, DIFF_HEADS, DIFF_DV))
        outs["mkp"].append(mk_p.reshape(B, M, MEM_HEADS, MEM_DH))
        outs["mvp"].append(mv_p.reshape(B, M, MEM_HEADS, MEM_DH))

        rq, rk, rv, rg, dq, dk, dv = _in_projection(ys, norm_mix_pre[i], w_in_bf, tabs_s, per_s, TM, F32,
                                                    DIFF_DK ** -0.5, False)
        mix_ret, s_s = _retention_sample(rq, rk, rv, rg, state_ret, i, B_s, L_s, 8)
        yp, mix_diff = _mlp_and_sample_attention(
            x2, norm_mlp_pre[i], tail_w[7], tail_w[8], norm_mlp_post[i], TM, MLP_FF_TILE_FUSED,
            dq, dk, dv, cache_diff_k, cache_diff_v, i, page_table, lam_rows, diff_subln_g[i],
            B_s, L_s, lam_init, PAGES_PER_STEP)
        def cross_s(x1, g, wq):
            q = _norm_matmul(x1, g, wq, TM, F32, scale=MEM_DH ** -0.5)
            return _cross_attention_sample(q, cache_mem_k, cache_mem_v, i, B_s, L_s, 8)

        ys = _layer_tail(ys, [mix_ret, mix_diff], tail_w, cross_s, TM)
        outs["rs"].append(s_s.astype(state_ret.dtype))
        outs["ks"].append(dk.reshape(B_s, L_s, DIFF_HEADS, 2, DIFF_DK))
        outs["vs"].append(dv.reshape(B_s, L_s, DIFF_HEADS, DIFF_DV))

    st = lambda k: outs[k][0][None] if depth == 1 else jnp.stack(outs[k])
    return (yp.reshape(B, L_p, D), ys.reshape(B_s, L_s, D), st("rp"), st("kp"), st("vp"),
            st("mkp"), st("mvp"), st("rs"), st("ks"), st("vs"))
```
